```python
import jax, jax.numpy as jnp
from jax import lax
import numpy as np

D_MODEL = 2048
BATCH = 8
SEQ = 2048
DEPTH = 2

N_EVEN = (DEPTH + 1) // 2
N_ODD = DEPTH // 2
EPS = 1e-6
D_PLE = 256
D_HGRN = D_MODEL // 2
HGRN_HEADS = 8
HGRN_HEAD_DIM = D_HGRN // HGRN_HEADS
CHUNK = 64
D_CONV = D_MODEL // 2
CONV_WIDTH = 3
D_IN_PROJ = 4 * D_HGRN + 3 * D_CONV
POOL_WINDOWS = (2, 4, 8, 16)
N_POOL_GROUPS = len(POOL_WINDOWS)
POOL_GROUP_DIM = D_MODEL // N_POOL_GROUPS
D_FF = 5632
N_EXPERTS = 8
TOP_K = 2
D_FF_EXPERT = 1024

kernel_name = 'hybrid_hgrn2_shortconv_pool_moe_trunk'


def rms_norm(x, gain):
    x32 = x.astype(jnp.float32)
    y = x32 * lax.rsqrt(jnp.mean(x32 * x32, axis=-1, keepdims=True) + EPS)
    return (y * gain.astype(jnp.float32)).astype(x.dtype)


def hgrn2_chunked(q, k, v, log_f):
    b_, s_, h_, dk = q.shape
    dv = v.shape[-1]
    n_chunks = s_ // CHUNK

    def to_chunks(t):
        return t.reshape(b_, n_chunks, CHUNK, h_, t.shape[-1]).transpose(0, 3, 1, 2, 4)

    q, k, v, log_f = to_chunks(q), to_chunks(k), to_chunks(v), to_chunks(log_f)
    cum = jnp.cumsum(log_f, axis=3)
    ref = cum[:, :, :, CHUNK // 2:CHUNK // 2 + 1, :]
    last = cum[:, :, :, -1:, :]
    scores = jnp.einsum('bhncd,bhnsd->bhncs', q * jnp.exp(cum - ref), k * jnp.exp(ref - cum))
    causal = jnp.tril(jnp.ones((CHUNK, CHUNK), dtype=bool))
    scores = jnp.where(causal, scores, 0.0)
    o_intra = jnp.einsum('bhncs,bhnsv->bhncv', scores, v)
    kv = jnp.einsum('bhncd,bhncv->bhndv', k * jnp.exp(last - cum), v)
    decay = jnp.exp(last[:, :, :, 0, :])

    def step(state, inp):
        dec, kv_c = inp
        return dec[..., None] * state + kv_c, state

    s0 = jnp.zeros((b_, h_, dk, dv), jnp.float32)
    _, s_prev = lax.scan(step, s0, (jnp.moveaxis(decay, 2, 0), jnp.moveaxis(kv, 2, 0)))
    o_inter = jnp.einsum('bhncd,nbhdv->bhncv', q * jnp.exp(cum), s_prev)
    o = o_intra + o_inter
    return o.transpose(0, 2, 3, 1, 4).reshape(b_, s_, h_, dv)


def hgrn2_shortconv_mixer(h, w_in, lb, hgrn_gain, conv_w, w_out):
    b_, s_, _ = h.shape
    u = h @ w_in
    q, f_raw, i_val, g, gate_b, gate_c, v_conv = jnp.split(
        u, [D_HGRN, 2 * D_HGRN, 3 * D_HGRN, 4 * D_HGRN,
            4 * D_HGRN + D_CONV, 4 * D_HGRN + 2 * D_CONV], axis=-1)

    def heads(t):
        return t.reshape(b_, s_, HGRN_HEADS, HGRN_HEAD_DIM)

    f = lb + (1.0 - lb) * jax.nn.sigmoid(f_raw.astype(jnp.float32))
    o = hgrn2_chunked(heads(jax.nn.silu(q.astype(jnp.float32))), heads(1.0 - f),
                      heads(i_val.astype(jnp.float32)), heads(jnp.log(f)))
    o = o * lax.rsqrt(jnp.mean(o * o, axis=-1, keepdims=True) + EPS)
    o = o.reshape(b_, s_, D_HGRN) * hgrn_gain.astype(jnp.float32) * jax.nn.silu(g.astype(jnp.float32))
    a_out = o.astype(h.dtype)

    t = gate_c * v_conv
    padded = jnp.pad(t, ((0, 0), (CONV_WIDTH - 1, 0), (0, 0)))
    conv = sum(padded[:, j:j + s_] * conv_w[j] for j in range(CONV_WIDTH))
    b_out = gate_b * conv

    return jnp.concatenate([a_out, b_out], axis=-1) @ w_out


def pool_mixer(h, pool_w, pool_scale):
    b_, s_, _ = h.shape
    hg = h.astype(jnp.float32).reshape(b_, s_, N_POOL_GROUPS, POOL_GROUP_DIM)
    csum = jnp.cumsum(hg, axis=1)
    pos = jnp.arange(1, s_ + 1, dtype=jnp.float32)
    diffs = []
    for gi, w in enumerate(POOL_WINDOWS):
        c = csum[:, :, gi]
        prev = jnp.pad(c, ((0, 0), (w, 0), (0, 0)))[:, :s_]
        mean = (c - prev) / jnp.minimum(pos, float(w))[None, :, None]
        diffs.append(mean - hg[:, :, gi])
    d = jnp.stack(diffs, axis=2).astype(h.dtype)
    y = jnp.einsum('bsgc,gce->bsge', d, pool_w).reshape(b_, s_, D_MODEL)
    return y * pool_scale


def swiglu(h, w_gate, w_up, w_down):
    return (jax.nn.silu(h @ w_gate) * (h @ w_up)) @ w_down


def moe_swiglu(h, w_router, w_gate_e, w_up_e, w_down_e):
    logits = (h @ w_router).astype(jnp.float32)
    probs = jax.nn.softmax(logits, axis=-1)
    top_p, top_i = lax.top_k(probs, TOP_K)
    top_p = top_p / jnp.sum(top_p, axis=-1, keepdims=True)
    gates = jnp.sum(jax.nn.one_hot(top_i, N_EXPERTS, dtype=jnp.float32) * top_p[..., None], axis=-2)
    gates = gates.astype(h.dtype)
    y = jnp.zeros_like(h)
    for e in range(N_EXPERTS):
        y = y + gates[..., e:e + 1] * swiglu(h, w_gate_e[e], w_up_e[e], w_down_e[e])
    return y


def per_layer_embedding(x, p_i, norm_gain, w_gate, w_proj):
    gate = jax.nn.sigmoid(rms_norm(x, norm_gain) @ w_gate)
    return gate * (p_i @ w_proj)


def setup_inputs(seed: int = 0) -> dict:
    key = jax.random.key(seed)
    ks = jax.random.split(key, 32)
    f32 = jnp.float32

    def nrm(k, shape, fan_in):
        return jax.random.normal(k, shape, f32) * (fan_in ** -0.5)

    def gain(k, shape):
        return 1.0 + 0.02 * jax.random.normal(k, shape, f32)

    return {
        'x': jax.random.normal(ks[0], (BATCH, SEQ, D_MODEL), f32),
        'p': jax.random.normal(ks[1], (DEPTH, BATCH, SEQ, D_PLE), f32),
        'final_norm_gain': gain(ks[2], (D_MODEL,)),
        'lb_table': 1.0 + 0.1 * jax.random.normal(ks[3], (DEPTH + 1, D_HGRN), f32),
        'mix_norm_even': gain(ks[4], (N_EVEN, D_MODEL)),
        'w_in_even': nrm(ks[5], (N_EVEN, D_MODEL, D_IN_PROJ), D_MODEL),
        'hgrn_norm_gain': gain(ks[6], (N_EVEN, D_HGRN)),
        'conv_w': nrm(ks[7], (N_EVEN, CONV_WIDTH, D_CONV), CONV_WIDTH),
        'w_out_even': nrm(ks[8], (N_EVEN, D_MODEL, D_MODEL), D_MODEL),
        'ffn_norm_even': gain(ks[9], (N_EVEN, D_MODEL)),
        'w_gate_dense': nrm(ks[10], (N_EVEN, D_MODEL, D_FF), D_MODEL),
        'w_up_dense': nrm(ks[11], (N_EVEN, D_MODEL, D_FF), D_MODEL),
        'w_down_dense': nrm(ks[12], (N_EVEN, D_FF, D_MODEL), D_FF),
        'mix_norm_odd': gain(ks[13], (N_ODD, D_MODEL)),
        'pool_w': nrm(ks[14], (N_ODD, N_POOL_GROUPS, POOL_GROUP_DIM, POOL_GROUP_DIM), POOL_GROUP_DIM),
        'pool_scale': gain(ks[15], (N_ODD, D_MODEL)),
        'ffn_norm_odd': gain(ks[16], (N_ODD, D_MODEL)),
        'w_router': nrm(ks[17], (N_ODD, D_MODEL, N_EXPERTS), D_MODEL),
        'w_gate_exp': nrm(ks[18], (N_ODD, N_EXPERTS, D_MODEL, D_FF_EXPERT), D_MODEL),
        'w_up_exp': nrm(ks[19], (N_ODD, N_EXPERTS, D_MODEL, D_FF_EXPERT), D_MODEL),
        'w_down_exp': nrm(ks[20], (N_ODD, N_EXPERTS, D_FF_EXPERT, D_MODEL), D_FF_EXPERT),
        'ple_norm': gain(ks[21], (DEPTH, D_MODEL)),
        'ple_gate_w': nrm(ks[22], (DEPTH, D_MODEL, D_MODEL), D_MODEL),
        'ple_proj': nrm(ks[23], (DEPTH, D_PLE, D_MODEL), D_PLE),
    }


def reference(x, p, final_norm_gain, lb_table, mix_norm_even, w_in_even, hgrn_norm_gain,
              conv_w, w_out_even, ffn_norm_even, w_gate_dense, w_up_dense, w_down_dense,
              mix_norm_odd, pool_w, pool_scale, ffn_norm_odd, w_router, w_gate_exp,
              w_up_exp, w_down_exp, ple_norm, ple_gate_w, ple_proj):
    lb_all = jnp.cumsum(jax.nn.softmax(lb_table.astype(jnp.float32), axis=0), axis=0)
    for i in range(DEPTH):
        j = i // 2
        if i % 2 == 0:
            x = x + hgrn2_shortconv_mixer(rms_norm(x, mix_norm_even[j]), w_in_even[j], lb_all[i],
                                          hgrn_norm_gain[j], conv_w[j], w_out_even[j])
            x = x + swiglu(rms_norm(x, ffn_norm_even[j]), w_gate_dense[j], w_up_dense[j], w_down_dense[j])
        else:
            x = x + pool_mixer(rms_norm(x, mix_norm_odd[j]), pool_w[j], pool_scale[j])
            x = x + moe_swiglu(rms_norm(x, ffn_norm_odd[j]), w_router[j], w_gate_exp[j],
                               w_up_exp[j], w_down_exp[j])
        x = x + per_layer_embedding(x, p[i], ple_norm[i], ple_gate_w[i], ple_proj[i])
    return rms_norm(x, final_norm_gain)
```

```python
import functools

import jax
import jax.numpy as jnp
from jax import lax
from jax.experimental import pallas as pl
from jax.experimental.pallas import tpu as pltpu

F32 = jnp.float32
BF16 = jnp.bfloat16

EPS = 1e-6
HGRN_HEADS = 8
HGRN_CHUNK = 64
CONV_WIDTH = 3
POOL_WINDOWS = (2, 4, 8, 16)
POOL_HALO = 16
TOP_K = 2
LANES = 128
VMEM_LIMIT_BYTES = 56 * 1024 * 1024


def _params(*semantics):
    return pltpu.CompilerParams(dimension_semantics=semantics,
                                vmem_limit_bytes=VMEM_LIMIT_BYTES)


def _rms_norm(x, gain):
    ms = jnp.mean(x * x, axis=-1, keepdims=True)
    return x * lax.rsqrt(ms + EPS) * gain


def _sigmoid(x):
    return 1.0 / (1.0 + jnp.exp(-x))


def _dot(a, b):
    return jnp.dot(a, b, preferred_element_type=F32)


def _dot_nt(a, b):
    return lax.dot_general(a, b, (((1,), (1,)), ((), ())), preferred_element_type=F32)


def _norm_matmul_kernel(x_ref, g_ref, w_ref, o_ref, h_ref):
    @pl.when(pl.program_id(1) == 0)
    def _():
        h_ref[...] = _rms_norm(x_ref[...], g_ref[...]).astype(BF16)

    o_ref[...] = _dot(h_ref[...], w_ref[...])


def _norm_matmul(x, gain, w, *, tm, tn):
    t, d = x.shape
    n = w.shape[1]
    return pl.pallas_call(
        _norm_matmul_kernel,
        grid=(t // tm, n // tn),
        in_specs=[pl.BlockSpec((tm, d), lambda i, j: (i, 0)),
                  pl.BlockSpec((1, d), lambda i, j: (0, 0)),
                  pl.BlockSpec((d, tn), lambda i, j: (0, j))],
        out_specs=pl.BlockSpec((tm, tn), lambda i, j: (i, j)),
        out_shape=jax.ShapeDtypeStruct((t, n), F32),
        scratch_shapes=[pltpu.VMEM((tm, d), BF16)],
        compiler_params=_params("parallel", "arbitrary"),
        name="norm_matmul",
    )(x, gain.reshape(1, d), w)


def _hgrn_conv_kernel(tri_ref, q_ref, f_ref, i_ref, g_ref, gb_ref, gc_ref, vc_ref,
                      lbt_ref, gain_ref, cw_ref, a_ref, b_ref, st_ref, *, rows, layer):
    s_len, dh = q_ref.shape
    ch = HGRN_CHUNK
    n_sub = rows // ch

    lbt = lbt_ref[...]
    e = jnp.exp(lbt - jnp.max(lbt, axis=0, keepdims=True))
    lb = (jnp.sum(e[0:layer + 1, :], axis=0, keepdims=True)
          / jnp.sum(e, axis=0, keepdims=True))
    gain = gain_ref[...]

    st_ref[...] = jnp.zeros_like(st_ref)
    tri = tri_ref[...]
    row = lax.broadcasted_iota(jnp.int32, (ch, ch), 0)
    col = lax.broadcasted_iota(jnp.int32, (ch, ch), 1)
    causal = row >= col

    def body(t, carry):
        r0 = pl.multiple_of(t * rows, rows)
        qr = q_ref[pl.ds(r0, rows), :]
        q = qr * _sigmoid(qr)
        f = lb + (1.0 - lb) * _sigmoid(f_ref[pl.ds(r0, rows), :])
        k = 1.0 - f
        v = i_ref[pl.ds(r0, rows), :]
        lf = jnp.log(f)
        hi = lf.astype(BF16)
        r1 = lf - hi.astype(F32)
        mid = r1.astype(BF16)
        lo = (r1 - mid.astype(F32)).astype(BF16)
        cum = _dot(tri, hi) + _dot(tri, mid) + _dot(tri, lo)

        outs = []
        for c in range(n_sub):
            sl = slice(c * ch, (c + 1) * ch)
            cum_c = cum[sl]
            ref = cum_c[ch // 2:ch // 2 + 1, :]
            last = cum_c[ch - 1:ch, :]
            q_c, k_c, v_c = q[sl], k[sl], v[sl]
            v_b = v_c.astype(BF16)
            qe = (q_c * jnp.exp(cum_c - ref)).astype(BF16)
            ke = (k_c * jnp.exp(ref - cum_c)).astype(BF16)
            scores = jnp.where(causal, _dot_nt(qe, ke), 0.0)
            o_c = _dot(scores.astype(BF16), v_b)
            st = st_ref[...]
            qc = (q_c * jnp.exp(cum_c)).astype(BF16)
            o_c = o_c + _dot_nt(qc, st.astype(BF16))
            kd = (k_c * jnp.exp(last - cum_c)).astype(BF16)
            kv_t = _dot(v_c.T.astype(BF16), kd)
            st_ref[...] = jnp.exp(last) * st + kv_t
            outs.append(o_c)
        o = jnp.concatenate(outs, axis=0) if n_sub > 1 else outs[0]
        o = o * lax.rsqrt(jnp.mean(o * o, axis=-1, keepdims=True) + EPS)
        gr = g_ref[pl.ds(r0, rows), :]
        a_ref[pl.ds(r0, rows), :] = (o * gain * (gr * _sigmoid(gr))).astype(BF16)
        return carry

    lax.fori_loop(0, s_len // rows, body, 0)

    tt = gc_ref[...] * vc_ref[...]
    ridx = lax.broadcasted_iota(jnp.int32, tt.shape, 0)
    cw = cw_ref[...]
    conv = tt * cw[CONV_WIDTH - 1:CONV_WIDTH, :]
    for back in range(1, CONV_WIDTH):
        shifted = jnp.where(ridx >= back, pltpu.roll(tt, back, 0), 0.0)
        conv = conv + shifted * cw[CONV_WIDTH - 1 - back:CONV_WIDTH - back, :]
    b_ref[...] = (gb_ref[...] * conv).astype(BF16)


def _hgrn_conv(u, lb_table, hgrn_gain, conv_w, *, batch, rows, layer):
    t, n_in = u.shape
    s_len = t // batch
    d_hgrn = hgrn_gain.shape[0]
    dh = d_hgrn // HGRN_HEADS
    d_conv = conv_w.shape[1]
    nh = HGRN_HEADS
    assert d_conv // dh == nh and n_in == 4 * d_hgrn + 3 * d_conv
    u3 = u.reshape(batch, s_len, n_in)

    idx = jnp.arange(rows)
    tri = ((idx[:, None] >= idx[None, :])
           & (idx[:, None] // HGRN_CHUNK == idx[None, :] // HGRN_CHUNK)).astype(BF16)

    def col(off):
        return pl.BlockSpec((None, s_len, dh), lambda b, h, off=off: (b, 0, off + h))

    n_lb = lb_table.shape[0]
    out_spec = pl.BlockSpec((None, s_len, dh), lambda b, h: (b, 0, h))
    return pl.pallas_call(
        functools.partial(_hgrn_conv_kernel, rows=rows, layer=layer),
        grid=(batch, nh),
        in_specs=[pl.BlockSpec((rows, rows), lambda b, h: (0, 0)),
                  col(0), col(nh), col(2 * nh), col(3 * nh),
                  col(4 * nh), col(5 * nh), col(6 * nh),
                  pl.BlockSpec((n_lb, dh), lambda b, h: (0, h)),
                  pl.BlockSpec((1, dh), lambda b, h: (0, h)),
                  pl.BlockSpec((CONV_WIDTH, dh), lambda b, h: (0, h))],
        out_specs=[out_spec, out_spec],
        out_shape=[jax.ShapeDtypeStruct((batch, s_len, d_hgrn), BF16),
                   jax.ShapeDtypeStruct((batch, s_len, d_conv), BF16)],
        scratch_shapes=[pltpu.VMEM((dh, dh), F32)],
        compiler_params=_params("parallel", "parallel"),
        name="hgrn_conv",
    )(tri, u3, u3, u3, u3, u3, u3, u3, lb_table, hgrn_gain.reshape(1, d_hgrn), conv_w)


def _out_proj_kernel(x_ref, a_ref, b_ref, wa_ref, wb_ref, o_ref):
    o_ref[...] = x_ref[...] + _dot(a_ref[...], wa_ref[...]) + _dot(b_ref[...], wb_ref[...])


def _out_proj(x, a, b, w, *, tm, tn):
    t, d = x.shape
    ka, kb = a.shape[1], b.shape[1]
    assert ka == kb
    return pl.pallas_call(
        _out_proj_kernel,
        grid=(t // tm, d // tn),
        in_specs=[pl.BlockSpec((tm, tn), lambda i, j: (i, j)),
                  pl.BlockSpec((tm, ka), lambda i, j: (i, 0)),
                  pl.BlockSpec((tm, kb), lambda i, j: (i, 0)),
                  pl.BlockSpec((ka, tn), lambda i, j: (0, j)),
                  pl.BlockSpec((kb, tn), lambda i, j: (1, j))],
        out_specs=pl.BlockSpec((tm, tn), lambda i, j: (i, j)),
        out_shape=jax.ShapeDtypeStruct((t, d), F32),
        compiler_params=_params("parallel", "arbitrary"),
        name="out_proj",
    )(x, a, b, w, w)


def _swiglu_kernel(x_ref, g_ref, wg_ref, wu_ref, wd_ref, o_ref, h_ref, acc_ref):
    f = pl.program_id(1)

    @pl.when(f == 0)
    def _():
        h_ref[...] = _rms_norm(x_ref[...], g_ref[...]).astype(BF16)
        acc_ref[...] = jnp.zeros_like(acc_ref)

    h = h_ref[...]
    gate = _dot(h, wg_ref[...])
    up = _dot(h, wu_ref[...])
    act = (gate * _sigmoid(gate) * up).astype(BF16)
    acc_ref[...] += _dot(act, wd_ref[...])

    @pl.when(f == pl.num_programs(1) - 1)
    def _():
        o_ref[...] = x_ref[...] + acc_ref[...]


def _swiglu(x, gain, wg, wu, wd, *, tm, tf):
    t, d = x.shape
    ff = wg.shape[1]
    return pl.pallas_call(
        _swiglu_kernel,
        grid=(t // tm, ff // tf),
        in_specs=[pl.BlockSpec((tm, d), lambda i, f: (i, 0)),
                  pl.BlockSpec((1, d), lambda i, f: (0, 0)),
                  pl.BlockSpec((d, tf), lambda i, f: (0, f)),
                  pl.BlockSpec((d, tf), lambda i, f: (0, f)),
                  pl.BlockSpec((tf, d), lambda i, f: (f, 0))],
        out_specs=pl.BlockSpec((tm, d), lambda i, f: (i, 0)),
        out_shape=jax.ShapeDtypeStruct((t, d), F32),
        scratch_shapes=[pltpu.VMEM((tm, d), BF16), pltpu.VMEM((tm, d), F32)],
        compiler_params=_params("parallel", "arbitrary"),
        name="swiglu",
    )(x, gain.reshape(1, d), wg, wu, wd)


def _ple_kernel(x_ref, g_ref, wg_ref, p_ref, wp_ref, fg_ref, o_ref, *, final_norm):
    x = x_ref[...]
    h = _rms_norm(x, g_ref[...]).astype(BF16)
    gate = _sigmoid(_dot(h, wg_ref[...]))
    y = x + gate * _dot(p_ref[...].astype(BF16), wp_ref[...])
    if final_norm:
        y = _rms_norm(y, fg_ref[...])
    o_ref[...] = y


def _ple(x, gain, wg, p, wp, final_gain, *, tm, final_norm):
    t, d = x.shape
    dp = p.shape[1]
    return pl.pallas_call(
        functools.partial(_ple_kernel, final_norm=final_norm),
        grid=(t // tm,),
        in_specs=[pl.BlockSpec((tm, d), lambda i: (i, 0)),
                  pl.BlockSpec((1, d), lambda i: (0, 0)),
                  pl.BlockSpec((d, d), lambda i: (0, 0)),
                  pl.BlockSpec((tm, dp), lambda i: (i, 0)),
                  pl.BlockSpec((dp, d), lambda i: (0, 0)),
                  pl.BlockSpec((1, d), lambda i: (0, 0))],
        out_specs=pl.BlockSpec((tm, d), lambda i: (i, 0)),
        out_shape=jax.ShapeDtypeStruct((t, d), F32),
        compiler_params=_params("parallel"),
        name="ple",
    )(x, gain.reshape(1, d), wg, p, wp, final_gain.reshape(1, d))


def _pool_router_kernel(x_ref, g_ref, pw_ref, ps_ref, fg_ref, wr_ref,
                        x1_ref, h2_ref, gates_ref, carry_ref, *, n_experts):
    s = pl.program_id(1)
    ts, d = x_ref.shape
    n_groups = len(POOL_WINDOWS)
    gd = d // n_groups

    @pl.when(s == 0)
    def _():
        carry_ref[...] = jnp.zeros_like(carry_ref)

    x = x_ref[...]
    h = _rms_norm(x, g_ref[...])
    ext = jnp.concatenate([carry_ref[...], h], axis=0)
    carry_ref[...] = h[ts - POOL_HALO:, :]

    pos = (s * ts + 1 + lax.broadcasted_iota(jnp.int32, (ts, 1), 0)).astype(F32)
    ys = []
    for gi, w in enumerate(POOL_WINDOWS):
        cols = slice(gi * gd, (gi + 1) * gd)
        acc = ext[:, cols]
        span = 1
        while span < w:
            acc = acc + pltpu.roll(acc, span, 0)
            span *= 2
        mean = acc[POOL_HALO:, :] / jnp.minimum(pos, float(w))
        diff = (mean - h[:, cols]).astype(BF16)
        ys.append(_dot(diff, pw_ref[gi]))
    y = jnp.concatenate(ys, axis=1)
    x1 = x + y * ps_ref[...]
    x1_ref[...] = x1

    h2 = _rms_norm(x1, fg_ref[...])
    h2_ref[...] = h2.astype(BF16)
    logits = jnp.dot(h2, wr_ref[...], preferred_element_type=F32,
                     precision=lax.Precision.HIGHEST)
    lane = lax.broadcasted_iota(jnp.int32, logits.shape, 1)
    valid = lane < n_experts
    logits = jnp.where(valid, logits, -jnp.inf)
    ex = jnp.exp(logits - jnp.max(logits, axis=-1, keepdims=True))
    probs = ex / jnp.sum(ex, axis=-1, keepdims=True)
    gates = jnp.zeros_like(probs)
    rest = jnp.where(valid, probs, -1.0)
    tops = []
    for _ in range(TOP_K):
        m = jnp.max(rest, axis=-1, keepdims=True)
        first = jnp.min(jnp.where(rest == m, lane, LANES), axis=-1, keepdims=True)
        sel = lane == first
        tops.append((m, sel))
        rest = jnp.where(sel, -1.0, rest)
    total = tops[0][0]
    for m, _ in tops[1:]:
        total = total + m
    for m, sel in tops:
        gates = jnp.where(sel, m / total, gates)
    gates_ref[...] = gates


def _pool_router(x, gain, pool_w, pool_scale, ffn_gain, w_router, *, batch, ts):
    t, d = x.shape
    s_len = t // batch
    n_s = s_len // ts
    n_groups, gd, _ = pool_w.shape
    n_experts = w_router.shape[1]
    wr = jnp.zeros((d, LANES), F32).at[:, :n_experts].set(w_router)
    row = lambda b, s: (b * n_s + s, 0)
    const = lambda b, s: (0, 0)
    return pl.pallas_call(
        functools.partial(_pool_router_kernel, n_experts=n_experts),
        grid=(batch, n_s),
        in_specs=[pl.BlockSpec((ts, d), row),
                  pl.BlockSpec((1, d), const),
                  pl.BlockSpec((n_groups, gd, gd), lambda b, s: (0, 0, 0)),
                  pl.BlockSpec((1, d), const),
                  pl.BlockSpec((1, d), const),
                  pl.BlockSpec((d, LANES), const)],
        out_specs=[pl.BlockSpec((ts, d), row),
                   pl.BlockSpec((ts, d), row),
                   pl.BlockSpec((ts, LANES), row)],
        out_shape=[jax.ShapeDtypeStruct((t, d), F32),
                   jax.ShapeDtypeStruct((t, d), BF16),
                   jax.ShapeDtypeStruct((t, LANES), F32)],
        scratch_shapes=[pltpu.VMEM((POOL_HALO, d), F32)],
        compiler_params=_params("parallel", "arbitrary"),
        name="pool_router",
    )(x, gain.reshape(1, d), pool_w, pool_scale.reshape(1, d), ffn_gain.reshape(1, d), wr)


def _moe_kernel(x_ref, h_ref, gates_ref, wg_ref, wu_ref, wd_ref, o_ref, acc_ref):
    e = pl.program_id(1)
    f = pl.program_id(2)

    @pl.when((e == 0) & (f == 0))
    def _():
        acc_ref[...] = jnp.zeros_like(acc_ref)

    h = h_ref[...]
    gates = gates_ref[...]
    lane = lax.broadcasted_iota(jnp.int32, gates.shape, 1)
    ge = jnp.sum(jnp.where(lane == e, gates, 0.0), axis=-1, keepdims=True)
    gate = _dot(h, wg_ref[...])
    up = _dot(h, wu_ref[...])
    act = (ge * (gate * _sigmoid(gate) * up)).astype(BF16)
    acc_ref[...] += _dot(act, wd_ref[...])

    @pl.when((e == pl.num_programs(1) - 1) & (f == pl.num_programs(2) - 1))
    def _():
        o_ref[...] = x_ref[...] + acc_ref[...]


def _moe(x, h, gates, wg, wu, wd, *, tm, tf):
    t, d = x.shape
    n_experts, _, ffe = wg.shape
    return pl.pallas_call(
        _moe_kernel,
        grid=(t // tm, n_experts, ffe // tf),
        in_specs=[pl.BlockSpec((tm, d), lambda i, e, f: (i, 0)),
                  pl.BlockSpec((tm, d), lambda i, e, f: (i, 0)),
                  pl.BlockSpec((tm, LANES), lambda i, e, f: (i, 0)),
                  pl.BlockSpec((None, d, tf), lambda i, e, f: (e, 0, f)),
                  pl.BlockSpec((None, d, tf), lambda i, e, f: (e, 0, f)),
                  pl.BlockSpec((None, tf, d), lambda i, e, f: (e, f, 0))],
        out_specs=pl.BlockSpec((tm, d), lambda i, e, f: (i, 0)),
        out_shape=jax.ShapeDtypeStruct((t, d), F32),
        scratch_shapes=[pltpu.VMEM((tm, d), F32)],
        compiler_params=_params("parallel", "arbitrary", "arbitrary"),
        name="moe",
    )(x, h, gates, wg, wu, wd)


def kernel(x, p, final_norm_gain, lb_table, mix_norm_even, w_in_even, hgrn_norm_gain, conv_w, w_out_even, ffn_norm_even, w_gate_dense, w_up_dense, w_down_dense, mix_norm_odd, pool_w, pool_scale, ffn_norm_odd, w_router, w_gate_exp, w_up_exp, w_down_exp, ple_norm, ple_gate_w, ple_proj):
    batch, s_len, d = x.shape
    t = batch * s_len
    depth = p.shape[0]
    bf = lambda w: w.astype(BF16)
    xs = x.reshape(t, d)
    for i in range(depth):
        j = i // 2
        if i % 2 == 0:
            u = _norm_matmul(xs, mix_norm_even[j], bf(w_in_even[j]), tm=1024, tn=1024)
            a, b = _hgrn_conv(u, lb_table, hgrn_norm_gain[j], conv_w[j], batch=batch, rows=128,
                              layer=i)
            a = a.reshape(t, -1)
            b = b.reshape(t, -1)
            xs = _out_proj(xs, a, b, bf(w_out_even[j]), tm=1024, tn=1024)
            xs = _swiglu(xs, ffn_norm_even[j], bf(w_gate_dense[j]), bf(w_up_dense[j]),
                         bf(w_down_dense[j]), tm=512, tf=512)
        else:
            xs, h2, gates = _pool_router(xs, mix_norm_odd[j], bf(pool_w[j]), pool_scale[j],
                                         ffn_norm_odd[j], w_router[j], batch=batch, ts=512)
            xs = _moe(xs, h2, gates, bf(w_gate_exp[j]), bf(w_up_exp[j]), bf(w_down_exp[j]),
                      tm=512, tf=512)
        xs = _ple(xs, ple_norm[i], bf(ple_gate_w[i]), p[i].reshape(t, -1), bf(ple_proj[i]),
                  final_norm_gain, tm=512, final_norm=(i == depth - 1))
    return xs.reshape(batch, s_len, d)
```

```python
import functools

import jax
import jax.numpy as jnp
from jax import lax
from jax.experimental import pallas as pl
from jax.experimental.pallas import tpu as pltpu

F32 = jnp.float32
BF16 = jnp.bfloat16

EPS = 1e-6
HGRN_HEADS = 8
HGRN_CHUNK = 64
CONV_WIDTH = 3
POOL_WINDOWS = (2, 4, 8, 16)
POOL_HALO = 16
TOP_K = 2
EXPERT_ROW_TILE = 512
LANES = 128
SUBLANES = 8
VMEM_LIMIT_BYTES = 56 * 1024 * 1024


def _params(*semantics):
    return pltpu.CompilerParams(dimension_semantics=semantics,
                                vmem_limit_bytes=VMEM_LIMIT_BYTES)


def _rms_norm(x, gain):
    ms = jnp.mean(x * x, axis=-1, keepdims=True)
    return x * lax.rsqrt(ms + EPS) * gain


def _sigmoid(x):
    return 1.0 / (1.0 + jnp.exp(-x))


def _dot(a, b):
    return jnp.dot(a, b, preferred_element_type=F32)


def _dot_nt(a, b):
    return lax.dot_general(a, b, (((1,), (1,)), ((), ())), preferred_element_type=F32)


def _norm_matmul_kernel(x_ref, g_ref, w_ref, o_ref, h_ref):
    @pl.when(pl.program_id(1) == 0)
    def _():
        h_ref[...] = _rms_norm(x_ref[...], g_ref[...]).astype(BF16)

    o_ref[...] = _dot(h_ref[...], w_ref[...])


def _norm_matmul(x, gain, w, *, tm, tn):
    t, d = x.shape
    n = w.shape[1]
    return pl.pallas_call(
        _norm_matmul_kernel,
        grid=(t // tm, n // tn),
        in_specs=[pl.BlockSpec((tm, d), lambda i, j: (i, 0)),
                  pl.BlockSpec((1, d), lambda i, j: (0, 0)),
                  pl.BlockSpec((d, tn), lambda i, j: (0, j))],
        out_specs=pl.BlockSpec((tm, tn), lambda i, j: (i, j)),
        out_shape=jax.ShapeDtypeStruct((t, n), F32),
        scratch_shapes=[pltpu.VMEM((tm, d), BF16)],
        compiler_params=_params("parallel", "arbitrary"),
        name="norm_matmul",
    )(x, gain.reshape(1, d), w)


def _hgrn_conv_kernel(tri_ref, q_ref, f_ref, i_ref, g_ref, gb_ref, gc_ref, vc_ref,
                      lbt_ref, gain_ref, cw_ref, a_ref, b_ref, *, rows, layer):
    s_len, dh = q_ref.shape
    ch = HGRN_CHUNK
    ct = tri_ref.shape[0]

    lbt = lbt_ref[...]
    e = jnp.exp(lbt - jnp.max(lbt, axis=0, keepdims=True))
    lb = (jnp.sum(e[0:layer + 1, :], axis=0, keepdims=True)
          / jnp.sum(e, axis=0, keepdims=True))
    gain = gain_ref[...]

    tri = tri_ref[...]
    row = lax.broadcasted_iota(jnp.int32, (ch, ch), 0)
    col = lax.broadcasted_iota(jnp.int32, (ch, ch), 1)
    causal = row >= col

    def body(t, st):
        r0 = pl.multiple_of(t * rows, rows)
        qr = q_ref[pl.ds(r0, rows), :]
        q = qr * _sigmoid(qr)
        f = lb + (1.0 - lb) * _sigmoid(f_ref[pl.ds(r0, rows), :])
        k = 1.0 - f
        v = i_ref[pl.ds(r0, rows), :].astype(BF16)
        lf = jnp.log(f)
        hi = lf.astype(BF16)
        r1 = lf - hi.astype(F32)
        mid = r1.astype(BF16)
        lo = (r1 - mid.astype(F32)).astype(BF16)
        parts = jnp.concatenate([hi, mid, lo], axis=1)
        cums = []
        for j in range(rows // ct):
            c3 = _dot(tri, parts[j * ct:(j + 1) * ct])
            cums.append(c3[:, :dh] + c3[:, dh:2 * dh] + c3[:, 2 * dh:])
        cum = jnp.concatenate(cums, axis=0)

        outs = []
        for c in range(rows // ch):
            sl = slice(c * ch, (c + 1) * ch)
            cum_c = cum[sl]
            ref = cum_c[ch // 2:ch // 2 + 1, :]
            last = cum_c[ch - 1:ch, :]
            q_c, k_c, v_c = q[sl], k[sl], v[sl]
            qe = (q_c * jnp.exp(cum_c - ref)).astype(BF16)
            ke = (k_c * jnp.exp(ref - cum_c)).astype(BF16)
            scores = jnp.where(causal, _dot_nt(qe, ke), 0.0)
            qc = (q_c * jnp.exp(cum_c)).astype(BF16)
            o_c = _dot(scores.astype(BF16), v_c) + _dot_nt(qc, st.astype(BF16))
            kd = (k_c * jnp.exp(last - cum_c)).astype(BF16)
            kv_t = lax.dot_general(v_c, kd, (((0,), (0,)), ((), ())),
                                   preferred_element_type=F32)
            st = jnp.exp(last) * st + kv_t
            outs.append(o_c)
        o = jnp.concatenate(outs, axis=0)
        o = o * lax.rsqrt(jnp.mean(o * o, axis=-1, keepdims=True) + EPS)
        gr = g_ref[pl.ds(r0, rows), :]
        a_ref[pl.ds(r0, rows), :] = (o * gain * (gr * _sigmoid(gr))).astype(BF16)
        return st

    lax.fori_loop(0, s_len // rows, body, jnp.zeros((dh, dh), F32))

    tt = gc_ref[...] * vc_ref[...]
    ridx = lax.broadcasted_iota(jnp.int32, tt.shape, 0)
    cw = cw_ref[...]
    conv = tt * cw[CONV_WIDTH - 1:CONV_WIDTH, :]
    for back in range(1, CONV_WIDTH):
        shifted = jnp.where(ridx >= back, pltpu.roll(tt, back, 0), 0.0)
        conv = conv + shifted * cw[CONV_WIDTH - 1 - back:CONV_WIDTH - back, :]
    b_ref[...] = (gb_ref[...] * conv).astype(BF16)


def _hgrn_conv(u, lb_table, hgrn_gain, conv_w, *, batch, rows, layer):
    t, n_in = u.shape
    s_len = t // batch
    d_hgrn = hgrn_gain.shape[0]
    dh = d_hgrn // HGRN_HEADS
    d_conv = conv_w.shape[1]
    nh = HGRN_HEADS
    assert d_conv // dh == nh and n_in == 4 * d_hgrn + 3 * d_conv
    u3 = u.reshape(batch, s_len, n_in)

    idx = jnp.arange(2 * HGRN_CHUNK)
    tri = ((idx[:, None] >= idx[None, :])
           & (idx[:, None] // HGRN_CHUNK == idx[None, :] // HGRN_CHUNK)).astype(BF16)

    def col(off):
        return pl.BlockSpec((None, s_len, dh), lambda b, h, off=off: (b, 0, off + h))

    n_lb = lb_table.shape[0]
    out_spec = pl.BlockSpec((None, s_len, dh), lambda b, h: (b, 0, h))
    return pl.pallas_call(
        functools.partial(_hgrn_conv_kernel, rows=rows, layer=layer),
        grid=(batch, nh),
        in_specs=[pl.BlockSpec(tri.shape, lambda b, h: (0, 0)),
                  col(0), col(nh), col(2 * nh), col(3 * nh),
                  col(4 * nh), col(5 * nh), col(6 * nh),
                  pl.BlockSpec((n_lb, dh), lambda b, h: (0, h)),
                  pl.BlockSpec((1, dh), lambda b, h: (0, h)),
                  pl.BlockSpec((CONV_WIDTH, dh), lambda b, h: (0, h))],
        out_specs=[out_spec, out_spec],
        out_shape=[jax.ShapeDtypeStruct((batch, s_len, d_hgrn), BF16),
                   jax.ShapeDtypeStruct((batch, s_len, d_conv), BF16)],
        compiler_params=_params("parallel", "parallel"),
        name="hgrn_conv",
    )(tri, u3, u3, u3, u3, u3, u3, u3, lb_table, hgrn_gain.reshape(1, d_hgrn), conv_w)


def _out_proj_kernel(x_ref, a_ref, b_ref, wa_ref, wb_ref, o_ref):
    o_ref[...] = x_ref[...] + _dot(a_ref[...], wa_ref[...]) + _dot(b_ref[...], wb_ref[...])


def _out_proj(x, a, b, w, *, tm, tn):
    t, d = x.shape
    ka, kb = a.shape[1], b.shape[1]
    assert ka == kb
    return pl.pallas_call(
        _out_proj_kernel,
        grid=(t // tm, d // tn),
        in_specs=[pl.BlockSpec((tm, tn), lambda i, j: (i, j)),
                  pl.BlockSpec((tm, ka), lambda i, j: (i, 0)),
                  pl.BlockSpec((tm, kb), lambda i, j: (i, 0)),
                  pl.BlockSpec((ka, tn), lambda i, j: (0, j)),
                  pl.BlockSpec((kb, tn), lambda i, j: (1, j))],
        out_specs=pl.BlockSpec((tm, tn), lambda i, j: (i, j)),
        out_shape=jax.ShapeDtypeStruct((t, d), F32),
        compiler_params=_params("parallel", "arbitrary"),
        name="out_proj",
    )(x, a, b, w, w)


def _swiglu_kernel(x_ref, g_ref, wg_ref, wu_ref, wd_ref, o_ref, h_ref, acc_ref):
    f = pl.program_id(1)

    @pl.when(f == 0)
    def _():
        h_ref[...] = _rms_norm(x_ref[...], g_ref[...]).astype(BF16)
        acc_ref[...] = jnp.zeros_like(acc_ref)

    h = h_ref[...]
    gate = _dot(h, wg_ref[...])
    up = _dot(h, wu_ref[...])
    act = (gate * _sigmoid(gate) * up).astype(BF16)
    acc_ref[...] += _dot(act, wd_ref[...])

    @pl.when(f == pl.num_programs(1) - 1)
    def _():
        o_ref[...] = x_ref[...] + acc_ref[...]


def _swiglu(x, gain, wg, wu, wd, *, tm, tf):
    t, d = x.shape
    ff = wg.shape[1]
    return pl.pallas_call(
        _swiglu_kernel,
        grid=(t // tm, ff // tf),
        in_specs=[pl.BlockSpec((tm, d), lambda i, f: (i, 0)),
                  pl.BlockSpec((1, d), lambda i, f: (0, 0)),
                  pl.BlockSpec((d, tf), lambda i, f: (0, f)),
                  pl.BlockSpec((d, tf), lambda i, f: (0, f)),
                  pl.BlockSpec((tf, d), lambda i, f: (f, 0))],
        out_specs=pl.BlockSpec((tm, d), lambda i, f: (i, 0)),
        out_shape=jax.ShapeDtypeStruct((t, d), F32),
        scratch_shapes=[pltpu.VMEM((tm, d), BF16), pltpu.VMEM((tm, d), F32)],
        compiler_params=_params("parallel", "arbitrary"),
        name="swiglu",
    )(x, gain.reshape(1, d), wg, wu, wd)


def _ple_math(x, g_ref, wg_ref, p_ref, wp_ref, fg_ref, final_norm):
    h = _rms_norm(x, g_ref[...]).astype(BF16)
    gate = _sigmoid(_dot(h, wg_ref[...]))
    y = x + gate * _dot(p_ref[...].astype(BF16), wp_ref[...])
    if final_norm:
        y = _rms_norm(y, fg_ref[...])
    return y


def _ple_kernel(x_ref, g_ref, wg_ref, p_ref, wp_ref, fg_ref, o_ref, *, final_norm):
    o_ref[...] = _ple_math(x_ref[...], g_ref, wg_ref, p_ref, wp_ref, fg_ref, final_norm)


def _ple(x, gain, wg, p, wp, final_gain, *, tm, final_norm):
    t, d = x.shape
    dp = p.shape[1]
    return pl.pallas_call(
        functools.partial(_ple_kernel, final_norm=final_norm),
        grid=(t // tm,),
        in_specs=[pl.BlockSpec((tm, d), lambda i: (i, 0)),
                  pl.BlockSpec((1, d), lambda i: (0, 0)),
                  pl.BlockSpec((d, d), lambda i: (0, 0)),
                  pl.BlockSpec((tm, dp), lambda i: (i, 0)),
                  pl.BlockSpec((dp, d), lambda i: (0, 0)),
                  pl.BlockSpec((1, d), lambda i: (0, 0))],
        out_specs=pl.BlockSpec((tm, d), lambda i: (i, 0)),
        out_shape=jax.ShapeDtypeStruct((t, d), F32),
        compiler_params=_params("parallel"),
        name="ple",
    )(x, gain.reshape(1, d), wg, p, wp, final_gain.reshape(1, d))


def _pool_router_kernel(x_ref, g_ref, pw_ref, ps_ref, fg_ref, wr_ref, lt_ref,
                        x1_ref, hg_ref, route_ref, counts_ref, ext_ref, cnt_ref,
                        *, n_experts):
    s = pl.program_id(1)
    ts, d = x_ref.shape
    n_groups = len(POOL_WINDOWS)
    gd = d // n_groups

    @pl.when(s == 0)
    def _():
        ext_ref[0:POOL_HALO, :] = jnp.zeros((POOL_HALO, d), F32)

    @pl.when(s > 0)
    def _():
        ext_ref[0:POOL_HALO, :] = ext_ref[ts:ts + POOL_HALO, :]

    x = x_ref[...]
    h = _rms_norm(x, g_ref[...])
    ext_ref[POOL_HALO:, :] = h

    def window_sum(cols, first_row, n_rows, w):
        if w <= SUBLANES:
            acc = ext_ref[pl.ds(first_row, n_rows), cols]
            for j in range(1, w):
                acc = acc + ext_ref[pl.ds(first_row - j, n_rows), cols]
            return acc
        half = w // 2
        part = window_sum(cols, first_row - half, n_rows + half, half)
        return part[half:] + part[:n_rows]

    pos = (s * ts + 1 + lax.broadcasted_iota(jnp.int32, (ts, 1), 0)).astype(F32)
    ys = []
    for gi, w in enumerate(POOL_WINDOWS):
        cols = slice(gi * gd, (gi + 1) * gd)
        mean = window_sum(cols, POOL_HALO, ts, w) / jnp.minimum(pos, float(w))
        diff = (mean - h[:, cols]).astype(BF16)
        ys.append(_dot(diff, pw_ref[gi]))
    y = jnp.concatenate(ys, axis=1)
    x1 = x + y * ps_ref[...]
    x1_ref[...] = x1

    h2 = _rms_norm(x1, fg_ref[...])
    hg_ref[:, 0:d] = h2

    hi = h2.astype(BF16)
    mid = (h2 - hi.astype(F32)).astype(BF16)
    p_hi = _dot(hi, wr_ref[...])
    p_mid = _dot(mid, wr_ref[...])
    logits = (p_hi + pltpu.roll(p_hi, LANES - n_experts, 1)
              + pltpu.roll(p_hi, LANES - 2 * n_experts, 1)
              + p_mid + pltpu.roll(p_mid, LANES - n_experts, 1))
    lane = lax.broadcasted_iota(jnp.int32, logits.shape, 1)
    valid = lane < n_experts
    logits = jnp.where(valid, logits, -jnp.inf)
    ex = jnp.exp(logits - jnp.max(logits, axis=-1, keepdims=True))
    probs = ex / jnp.sum(ex, axis=-1, keepdims=True)
    gates = jnp.zeros_like(probs)
    rest = jnp.where(valid, probs, -1.0)
    tops = []
    for _ in range(TOP_K):
        m = jnp.max(rest, axis=-1, keepdims=True)
        first = jnp.min(jnp.where(rest == m, lane, LANES), axis=-1, keepdims=True)
        sel = lane == first
        tops.append((m, sel))
        rest = jnp.where(sel, -1.0, rest)
    total = tops[0][0]
    for m, _ in tops[1:]:
        total = total + m
    for m, sel in tops:
        gates = jnp.where(sel, m / total, gates)
    hg_ref[:, d:d + LANES] = gates

    first_step = (pl.program_id(0) == 0) & (s == 0)

    @pl.when(first_step)
    def _():
        cnt_ref[...] = jnp.zeros_like(cnt_ref)

    chosen = tops[0][1]
    for _, sel in tops[1:]:
        chosen = chosen | sel
    chosen = jnp.where(chosen, 1.0, 0.0)
    before = cnt_ref[...] + _dot(lt_ref[...], chosen.astype(BF16))
    cnt_ref[...] = cnt_ref[...] + jnp.sum(chosen, axis=0, keepdims=True)
    counts_ref[...] = cnt_ref[...]
    lane_f = lane.astype(F32)
    route = jnp.zeros((ts, LANES), F32)
    for k, (_, sel) in enumerate(tops):
        expert_k = jnp.sum(jnp.where(sel, lane_f, 0.0), axis=-1, keepdims=True)
        order_k = jnp.sum(jnp.where(sel, before, 0.0), axis=-1, keepdims=True)
        route = jnp.where(lane == k, expert_k, route)
        route = jnp.where(lane == TOP_K + k, order_k, route)
    route_ref[...] = route.astype(jnp.int32)


def _pool_router(x, gain, pool_w, pool_scale, ffn_gain, w_router, *, batch, ts):
    t, d = x.shape
    s_len = t // batch
    n_s = s_len // ts
    n_groups, gd, _ = pool_w.shape
    n_experts = w_router.shape[1]
    assert 3 * n_experts <= LANES
    w_hi = w_router.astype(BF16)
    r1 = w_router - w_hi.astype(F32)
    w_mid = r1.astype(BF16)
    w_lo = (r1 - w_mid.astype(F32)).astype(BF16)
    wr = jnp.zeros((d, LANES), BF16).at[:, :3 * n_experts].set(
        jnp.concatenate([w_hi, w_mid, w_lo], axis=1))
    idx = jnp.arange(ts)
    lower = (idx[None, :] < idx[:, None]).astype(BF16)
    row = lambda b, s: (b * n_s + s, 0)
    const = lambda b, s: (0, 0)
    return pl.pallas_call(
        functools.partial(_pool_router_kernel, n_experts=n_experts),
        grid=(batch, n_s),
        in_specs=[pl.BlockSpec((ts, d), row),
                  pl.BlockSpec((1, d), const),
                  pl.BlockSpec((n_groups, gd, gd), lambda b, s: (0, 0, 0)),
                  pl.BlockSpec((1, d), const),
                  pl.BlockSpec((1, d), const),
                  pl.BlockSpec((d, LANES), const),
                  pl.BlockSpec((ts, ts), const)],
        out_specs=[pl.BlockSpec((ts, d), row),
                   pl.BlockSpec((ts, d + LANES), row),
                   pl.BlockSpec((ts, LANES), row),
                   pl.BlockSpec((1, LANES), const)],
        out_shape=[jax.ShapeDtypeStruct((t, d), F32),
                   jax.ShapeDtypeStruct((t, d + LANES), F32),
                   jax.ShapeDtypeStruct((t, LANES), jnp.int32),
                   jax.ShapeDtypeStruct((1, LANES), F32)],
        scratch_shapes=[pltpu.VMEM((POOL_HALO + ts, d), F32),
                        pltpu.VMEM((1, LANES), F32)],
        compiler_params=_params("arbitrary", "arbitrary"),
        name="pool_router",
    )(x, gain.reshape(1, d), pool_w, pool_scale.reshape(1, d), ffn_gain.reshape(1, d), wr, lower)


def _dispatch_kernel(slots_ref, bounds_ref, hg_ref, xs_ref, zero_ref, sem, zero_sem,
                     *, n_experts, tile):
    i = pl.program_id(0)
    td = hg_ref.shape[0]

    def row_copy(r, k):
        slot = slots_ref[(i * td + r) * TOP_K + k]
        return pltpu.make_async_copy(hg_ref.at[pl.ds(r, 1), :],
                                     xs_ref.at[pl.ds(slot, 1), :], sem)

    def start_row(r, carry):
        for k in range(TOP_K):
            row_copy(r, k).start()
        return carry

    def wait_row(r, carry):
        for k in range(TOP_K):
            row_copy(r, k).wait()
        return carry

    lax.fori_loop(0, td, start_row, 0, unroll=8)

    def zero_copies():
        copies = []

        def block(first, size):
            return pltpu.make_async_copy(zero_ref.at[pl.ds(0, size), :],
                                         xs_ref.at[pl.ds(first, size), :], zero_sem)

        for e in range(n_experts):
            first_unused = bounds_ref[e]
            end = bounds_ref[n_experts + e]
            aligned = (first_unused + SUBLANES - 1) // SUBLANES * SUBLANES
            for r in range(SUBLANES - 1):
                copies.append((first_unused + r < aligned, block(first_unused + r, 1)))
            left = end - aligned
            pos = aligned
            size = tile // 2
            while size >= SUBLANES:
                needed = (left & size) != 0
                copies.append((needed, block(pl.multiple_of(pos, SUBLANES), size)))
                pos = pos + jnp.where(needed, size, 0)
                size //= 2
        total = bounds_ref[2 * n_experts]
        for b in range(n_experts):
            first = pl.multiple_of(total + b * tile, tile)
            copies.append((first < xs_ref.shape[0], block(first, tile)))
        return copies

    @pl.when(i == 0)
    def _():
        zero_ref[...] = jnp.zeros_like(zero_ref)
        for action in ("start", "wait"):
            for needed, copy in zero_copies():
                pl.when(needed)(getattr(copy, action))

    lax.fori_loop(0, td, wait_row, 0, unroll=8)


def _dispatch(slots_flat, bounds, hg, *, td, n_experts, tile):
    t, width = hg.shape
    n_rows = t * TOP_K + n_experts * tile
    return pl.pallas_call(
        functools.partial(_dispatch_kernel, n_experts=n_experts, tile=tile),
        grid_spec=pltpu.PrefetchScalarGridSpec(
            num_scalar_prefetch=2,
            grid=(t // td,),
            in_specs=[pl.BlockSpec((td, width), lambda i, slots, bounds: (i, 0))],
            out_specs=pl.BlockSpec(memory_space=pl.ANY),
            scratch_shapes=[pltpu.VMEM((tile, width), F32),
                            pltpu.SemaphoreType.DMA,
                            pltpu.SemaphoreType.DMA]),
        out_shape=jax.ShapeDtypeStruct((n_rows, width), F32),
        compiler_params=_params("arbitrary"),
        name="moe_dispatch",
    )(slots_flat, bounds, hg)


def _experts_kernel(te_ref, nt_ref, xs_ref, wg_ref, wu_ref, wd_ref, ys_ref):
    i = pl.program_id(0)
    d = wg_ref.shape[0]

    @pl.when(i < nt_ref[0])
    def _():
        rows = xs_ref[:, 0:d].astype(BF16)
        gates = xs_ref[:, d:d + LANES]
        lane = lax.broadcasted_iota(jnp.int32, gates.shape, 1)
        ge = jnp.sum(jnp.where(lane == te_ref[i], gates, 0.0), axis=-1, keepdims=True)
        gate = _dot(rows, wg_ref[...])
        up = _dot(rows, wu_ref[...])
        act = (ge * (gate * _sigmoid(gate) * up)).astype(BF16)
        ys_ref[...] = _dot(act, wd_ref[...])

    @pl.when(i >= nt_ref[0])
    def _():
        ys_ref[...] = jnp.zeros_like(ys_ref)


def _experts(tile_expert, n_tiles, xs, wg, wu, wd, *, tile):
    n_rows, width = xs.shape
    n_experts, d, ffe = wg.shape
    row_map = lambda i, te, nt: (i, 0)
    w_map = lambda i, te, nt: (te[i], 0, 0)
    return pl.pallas_call(
        _experts_kernel,
        grid_spec=pltpu.PrefetchScalarGridSpec(
            num_scalar_prefetch=2,
            grid=(n_rows // tile,),
            in_specs=[pl.BlockSpec((tile, width), row_map),
                      pl.BlockSpec((None, d, ffe), w_map),
                      pl.BlockSpec((None, d, ffe), w_map),
                      pl.BlockSpec((None, ffe, d), w_map)],
            out_specs=pl.BlockSpec((tile, d), row_map)),
        out_shape=jax.ShapeDtypeStruct((n_rows, d), F32),
        compiler_params=_params("arbitrary"),
        name="moe_experts",
    )(tile_expert, n_tiles, xs, wg, wu, wd)


def _routing_tables(route, counts, *, n_experts, tile):
    t = route.shape[0]
    counts = counts[0, :n_experts].astype(jnp.int32)
    padded = (counts + tile - 1) // tile * tile
    ends = jnp.cumsum(padded)
    starts = ends - padded
    expert, order = route[:, :TOP_K], route[:, TOP_K:2 * TOP_K]
    slots = (starts[expert] + order).reshape(-1)
    bounds = jnp.concatenate([starts + counts, ends, ends[-1:]]).astype(jnp.int32)
    n_tiles_max = (t * TOP_K) // tile + n_experts
    first_row = jnp.arange(n_tiles_max, dtype=jnp.int32) * tile
    tile_expert = jnp.minimum(jnp.sum(first_row[:, None] >= ends[None, :], axis=1),
                              n_experts - 1).astype(jnp.int32)
    n_tiles = (ends[-1:] // tile).astype(jnp.int32)
    return slots.astype(jnp.int32), bounds, tile_expert, n_tiles


def _ple_combine_kernel(slots_ref, x_ref, g_ref, wg_ref, p_ref, wp_ref, fg_ref, ys_ref,
                        o_ref, ybuf, sem, *, final_norm):
    i = pl.program_id(0)
    n = pl.num_programs(0)
    tm = x_ref.shape[0]

    def row_copy(step, r, k):
        slot = slots_ref[(step * tm + r) * TOP_K + k]
        buf = step % 2
        return pltpu.make_async_copy(ys_ref.at[pl.ds(slot, 1), :],
                                     ybuf.at[buf, k, pl.ds(r, 1), :], sem.at[buf])

    def start_tile(step):
        def body(r, carry):
            for k in range(TOP_K):
                row_copy(step, r, k).start()
            return carry
        lax.fori_loop(0, tm, body, 0, unroll=8)

    def wait_tile(step):
        def body(r, carry):
            for k in range(TOP_K):
                row_copy(step, r, k).wait()
            return carry
        lax.fori_loop(0, tm, body, 0, unroll=8)

    @pl.when(i == 0)
    def _():
        start_tile(i)

    @pl.when(i + 1 < n)
    def _():
        start_tile(i + 1)

    wait_tile(i)
    x = x_ref[...]
    for k in range(TOP_K):
        x = x + ybuf[i % 2, k]
    o_ref[...] = _ple_math(x, g_ref, wg_ref, p_ref, wp_ref, fg_ref, final_norm)


def _ple_combine(slots_flat, x, gain, wg, p, wp, final_gain, ys, *, tm, final_norm):
    t, d = x.shape
    dp = p.shape[1]
    row = lambda i, slots: (i, 0)
    const = lambda i, slots: (0, 0)
    return pl.pallas_call(
        functools.partial(_ple_combine_kernel, final_norm=final_norm),
        grid_spec=pltpu.PrefetchScalarGridSpec(
            num_scalar_prefetch=1,
            grid=(t // tm,),
            in_specs=[pl.BlockSpec((tm, d), row),
                      pl.BlockSpec((1, d), const),
                      pl.BlockSpec((d, d), const),
                      pl.BlockSpec((tm, dp), row),
                      pl.BlockSpec((dp, d), const),
                      pl.BlockSpec((1, d), const),
                      pl.BlockSpec(memory_space=pl.ANY)],
            out_specs=pl.BlockSpec((tm, d), row),
            scratch_shapes=[pltpu.VMEM((2, TOP_K, tm, d), F32),
                            pltpu.SemaphoreType.DMA((2,))]),
        out_shape=jax.ShapeDtypeStruct((t, d), F32),
        compiler_params=_params("arbitrary"),
        name="ple_combine",
    )(slots_flat, x, gain.reshape(1, d), wg, p, wp, final_gain.reshape(1, d), ys)


def kernel(x, p, final_norm_gain, lb_table, mix_norm_even, w_in_even, hgrn_norm_gain, conv_w, w_out_even, ffn_norm_even, w_gate_dense, w_up_dense, w_down_dense, mix_norm_odd, pool_w, pool_scale, ffn_norm_odd, w_router, w_gate_exp, w_up_exp, w_down_exp, ple_norm, ple_gate_w, ple_proj):
    batch, s_len, d = x.shape
    t = batch * s_len
    depth = p.shape[0]
    bf = lambda w: w.astype(BF16)

    def ple_args(i):
        return (ple_norm[i], bf(ple_gate_w[i]), p[i].reshape(t, -1), bf(ple_proj[i]),
                final_norm_gain)

    xs = x.reshape(t, d)
    for i in range(depth):
        j = i // 2
        if i % 2 == 0:
            u = _norm_matmul(xs, mix_norm_even[j], bf(w_in_even[j]), tm=1024, tn=1024)
            a, b = _hgrn_conv(u, lb_table, hgrn_norm_gain[j], conv_w[j], batch=batch, rows=512,
                              layer=i)
            a = a.reshape(t, -1)
            b = b.reshape(t, -1)
            xs = _out_proj(xs, a, b, bf(w_out_even[j]), tm=1024, tn=1024)
            xs = _swiglu(xs, ffn_norm_even[j], bf(w_gate_dense[j]), bf(w_up_dense[j]),
                         bf(w_down_dense[j]), tm=512, tf=512)
            xs = _ple(xs, *ple_args(i), tm=512, final_norm=(i == depth - 1))
        else:
            n_experts = w_router.shape[-1]
            tile = EXPERT_ROW_TILE
            xs, hg, route, counts = _pool_router(
                xs, mix_norm_odd[j], bf(pool_w[j]), pool_scale[j], ffn_norm_odd[j],
                w_router[j], batch=batch, ts=512)
            slots, bounds, tile_expert, n_tiles = _routing_tables(
                route, counts, n_experts=n_experts, tile=tile)
            rows = _dispatch(slots, bounds, hg, td=512, n_experts=n_experts, tile=tile)
            ys = _experts(tile_expert, n_tiles, rows, bf(w_gate_exp[j]), bf(w_up_exp[j]),
                          bf(w_down_exp[j]), tile=tile)
            xs = _ple_combine(slots, xs, *ple_args(i), ys, tm=256,
                              final_norm=(i == depth - 1))
    return xs.reshape(batch, s_len, d)
```

```python
import functools
import math

import jax
import jax.numpy as jnp
from jax import lax
from jax.experimental import pallas as pl
from jax.experimental.pallas import tpu as pltpu

F32 = jnp.float32
BF16 = jnp.bfloat16

EPS = 1e-6
HGRN_HEADS = 8
HGRN_CHUNK = 64
CONV_WIDTH = 3
POOL_WINDOWS = (2, 4, 8, 16)
POOL_HALO = 16
TOP_K = 2
EXPERT_ROW_TILE = 512
COMBINE_CHUNKS = 8
LANES = 128
SUBLANES = 8
BF16_SUBLANES = 16
VMEM_LIMIT_BYTES = 56 * 1024 * 1024


def _params(*semantics):
    return pltpu.CompilerParams(dimension_semantics=semantics,
                                vmem_limit_bytes=VMEM_LIMIT_BYTES)


def _rms_norm(x, gain):
    ms = jnp.mean(x * x, axis=-1, keepdims=True)
    return x * lax.rsqrt(ms + EPS) * gain


def _sigmoid(x):
    return 1.0 / (1.0 + jnp.exp(-x))


def _dot(a, b):
    return jnp.dot(a, b, preferred_element_type=F32)


def _dot_nt(a, b):
    return lax.dot_general(a, b, (((1,), (1,)), ((), ())), preferred_element_type=F32)


class _Casts:
    def __init__(self, weights, grid):
        n_steps = math.prod(grid)
        self.shapes = [w.shape for w in weights]
        self.views, self.specs, self.out_shapes = [], [], []
        for w in weights:
            cols = w.shape[-1]
            rows = w.size // cols
            block_rows = next(br for br in range(BF16_SUBLANES, rows + 1, BF16_SUBLANES)
                              if rows % br == 0 and rows // br <= n_steps)
            self.views.append(w.reshape(rows, cols))
            self.specs.append(pl.BlockSpec(
                (block_rows, cols), functools.partial(self._index_map, grid,
                                                      rows // block_rows)))
            self.out_shapes.append(jax.ShapeDtypeStruct((rows, cols), BF16))

    @staticmethod
    def _index_map(grid, n_blocks, *args):
        step = 0
        for size, idx in zip(grid, args):
            step = step * size + idx
        return jnp.minimum(step, n_blocks - 1), 0

    def __len__(self):
        return len(self.views)

    def wrap(self, body, n_in, n_out, n_prefetch=0):
        n = len(self)

        def kernel(*refs):
            ins_end = n_prefetch + n_in
            outs_start = ins_end + n
            outs_end = outs_start + n_out
            body(*refs[:ins_end], *refs[outs_start:outs_end], *refs[outs_end + n:])
            for src, dst in zip(refs[ins_end:outs_start], refs[outs_end:outs_end + n]):
                dst[...] = src[...].astype(BF16)

        return kernel

    def restore(self, outs):
        return [o.reshape(shape) for o, shape in zip(outs, self.shapes)]


def _norm_matmul_kernel(x_ref, g_ref, w_ref, o_ref, h_ref):
    @pl.when(pl.program_id(1) == 0)
    def _():
        h_ref[...] = _rms_norm(x_ref[...], g_ref[...]).astype(BF16)

    o_ref[...] = _dot(h_ref[...], w_ref[...])


def _norm_matmul(x, gain, w, cast_weights, *, tm, tn):
    t, d = x.shape
    n = w.shape[1]
    grid = (t // tm, n // tn)
    casts = _Casts(cast_weights, grid)
    out, *cast = pl.pallas_call(
        casts.wrap(_norm_matmul_kernel, n_in=3, n_out=1),
        grid=grid,
        in_specs=[pl.BlockSpec((tm, d), lambda i, j: (i, 0)),
                  pl.BlockSpec((1, d), lambda i, j: (0, 0)),
                  pl.BlockSpec((d, tn), lambda i, j: (0, j)),
                  *casts.specs],
        out_specs=[pl.BlockSpec((tm, tn), lambda i, j: (i, j)), *casts.specs],
        out_shape=[jax.ShapeDtypeStruct((t, n), F32), *casts.out_shapes],
        scratch_shapes=[pltpu.VMEM((tm, d), BF16)],
        compiler_params=_params("parallel", "arbitrary"),
        name="norm_matmul",
    )(x, gain.reshape(1, d), w, *casts.views)
    return out, casts.restore(cast)


def _hgrn_conv_kernel(tri_ref, q_ref, f_ref, i_ref, g_ref, gb_ref, gc_ref, vc_ref,
                      lbt_ref, gain_ref, cw_ref, a_ref, b_ref, *, rows, layer):
    s_len, dh = q_ref.shape
    ch = HGRN_CHUNK
    ct = tri_ref.shape[0]

    lbt = lbt_ref[...]
    e = jnp.exp(lbt - jnp.max(lbt, axis=0, keepdims=True))
    lb = (jnp.sum(e[0:layer + 1, :], axis=0, keepdims=True)
          / jnp.sum(e, axis=0, keepdims=True))
    gain = gain_ref[...]

    tri = tri_ref[...]
    row = lax.broadcasted_iota(jnp.int32, (ch, ch), 0)
    col = lax.broadcasted_iota(jnp.int32, (ch, ch), 1)
    causal = row >= col

    def body(t, st):
        r0 = pl.multiple_of(t * rows, rows)
        qr = q_ref[pl.ds(r0, rows), :]
        q = qr * _sigmoid(qr)
        f = lb + (1.0 - lb) * _sigmoid(f_ref[pl.ds(r0, rows), :])
        k = 1.0 - f
        v = i_ref[pl.ds(r0, rows), :].astype(BF16)
        lf = jnp.log(f)
        hi = lf.astype(BF16)
        r1 = lf - hi.astype(F32)
        mid = r1.astype(BF16)
        lo = (r1 - mid.astype(F32)).astype(BF16)
        parts = jnp.concatenate([hi, mid, lo], axis=1)
        cums = []
        for j in range(rows // ct):
            c3 = _dot(tri, parts[j * ct:(j + 1) * ct])
            cums.append(c3[:, :dh] + c3[:, dh:2 * dh] + c3[:, 2 * dh:])
        cum = jnp.concatenate(cums, axis=0)

        outs = []
        for c in range(rows // ch):
            sl = slice(c * ch, (c + 1) * ch)
            cum_c = cum[sl]
            ref = cum_c[ch // 2:ch // 2 + 1, :]
            last = cum_c[ch - 1:ch, :]
            q_c, k_c, v_c = q[sl], k[sl], v[sl]
            qe = (q_c * jnp.exp(cum_c - ref)).astype(BF16)
            ke = (k_c * jnp.exp(ref - cum_c)).astype(BF16)
            scores = jnp.where(causal, _dot_nt(qe, ke), 0.0)
            qc = (q_c * jnp.exp(cum_c)).astype(BF16)
            o_c = _dot(scores.astype(BF16), v_c) + _dot_nt(qc, st.astype(BF16))
            kd = (k_c * jnp.exp(last - cum_c)).astype(BF16)
            kv_t = lax.dot_general(v_c, kd, (((0,), (0,)), ((), ())),
                                   preferred_element_type=F32)
            st = jnp.exp(last) * st + kv_t
            outs.append(o_c)
        o = jnp.concatenate(outs, axis=0)
        o = o * lax.rsqrt(jnp.mean(o * o, axis=-1, keepdims=True) + EPS)
        gr = g_ref[pl.ds(r0, rows), :]
        a_ref[pl.ds(r0, rows), :] = (o * gain * (gr * _sigmoid(gr))).astype(BF16)
        return st

    lax.fori_loop(0, s_len // rows, body, jnp.zeros((dh, dh), F32))

    tt = gc_ref[...] * vc_ref[...]
    ridx = lax.broadcasted_iota(jnp.int32, tt.shape, 0)
    cw = cw_ref[...]
    conv = tt * cw[CONV_WIDTH - 1:CONV_WIDTH, :]
    for back in range(1, CONV_WIDTH):
        shifted = jnp.where(ridx >= back, pltpu.roll(tt, back, 0), 0.0)
        conv = conv + shifted * cw[CONV_WIDTH - 1 - back:CONV_WIDTH - back, :]
    b_ref[...] = (gb_ref[...] * conv).astype(BF16)


def _hgrn_conv(u, lb_table, hgrn_gain, conv_w, *, batch, rows, layer):
    t, n_in = u.shape
    s_len = t // batch
    d_hgrn = hgrn_gain.shape[0]
    dh = d_hgrn // HGRN_HEADS
    d_conv = conv_w.shape[1]
    nh = HGRN_HEADS
    assert d_conv // dh == nh and n_in == 4 * d_hgrn + 3 * d_conv
    u3 = u.reshape(batch, s_len, n_in)

    idx = jnp.arange(2 * HGRN_CHUNK)
    tri = ((idx[:, None] >= idx[None, :])
           & (idx[:, None] // HGRN_CHUNK == idx[None, :] // HGRN_CHUNK)).astype(BF16)

    def col(off):
        return pl.BlockSpec((None, s_len, dh), lambda b, h, off=off: (b, 0, off + h))

    n_lb = lb_table.shape[0]
    out_spec = pl.BlockSpec((None, s_len, dh), lambda b, h: (b, 0, h))
    return pl.pallas_call(
        functools.partial(_hgrn_conv_kernel, rows=rows, layer=layer),
        grid=(batch, nh),
        in_specs=[pl.BlockSpec(tri.shape, lambda b, h: (0, 0)),
                  col(0), col(nh), col(2 * nh), col(3 * nh),
                  col(4 * nh), col(5 * nh), col(6 * nh),
                  pl.BlockSpec((n_lb, dh), lambda b, h: (0, h)),
                  pl.BlockSpec((1, dh), lambda b, h: (0, h)),
                  pl.BlockSpec((CONV_WIDTH, dh), lambda b, h: (0, h))],
        out_specs=[out_spec, out_spec],
        out_shape=[jax.ShapeDtypeStruct((batch, s_len, d_hgrn), BF16),
                   jax.ShapeDtypeStruct((batch, s_len, d_conv), BF16)],
        compiler_params=_params("parallel", "parallel"),
        name="hgrn_conv",
    )(tri, u3, u3, u3, u3, u3, u3, u3, lb_table, hgrn_gain.reshape(1, d_hgrn), conv_w)


def _out_proj_kernel(x_ref, a_ref, b_ref, wa_ref, wb_ref, o_ref):
    o_ref[...] = x_ref[...] + _dot(a_ref[...], wa_ref[...]) + _dot(b_ref[...], wb_ref[...])


def _out_proj(x, a, b, w, cast_weights, *, tm, tn):
    t, d = x.shape
    ka, kb = a.shape[1], b.shape[1]
    assert ka == kb
    grid = (t // tm, d // tn)
    casts = _Casts(cast_weights, grid)
    out, *cast = pl.pallas_call(
        casts.wrap(_out_proj_kernel, n_in=5, n_out=1),
        grid=grid,
        in_specs=[pl.BlockSpec((tm, tn), lambda i, j: (i, j)),
                  pl.BlockSpec((tm, ka), lambda i, j: (i, 0)),
                  pl.BlockSpec((tm, kb), lambda i, j: (i, 0)),
                  pl.BlockSpec((ka, tn), lambda i, j: (0, j)),
                  pl.BlockSpec((kb, tn), lambda i, j: (1, j)),
                  *casts.specs],
        out_specs=[pl.BlockSpec((tm, tn), lambda i, j: (i, j)), *casts.specs],
        out_shape=[jax.ShapeDtypeStruct((t, d), F32), *casts.out_shapes],
        compiler_params=_params("parallel", "arbitrary"),
        name="out_proj",
    )(x, a, b, w, w, *casts.views)
    return out, casts.restore(cast)


def _swiglu_kernel(x_ref, g_ref, wg_ref, wu_ref, wd_ref, o_ref, h_ref, acc_ref):
    f = pl.program_id(1)

    @pl.when(f == 0)
    def _():
        h_ref[...] = _rms_norm(x_ref[...], g_ref[...]).astype(BF16)
        acc_ref[...] = jnp.zeros_like(acc_ref)

    h = h_ref[...]
    gate = _dot(h, wg_ref[...])
    up = _dot(h, wu_ref[...])
    act = (gate * _sigmoid(gate) * up).astype(BF16)
    acc_ref[...] += _dot(act, wd_ref[...])

    @pl.when(f == pl.num_programs(1) - 1)
    def _():
        o_ref[...] = x_ref[...] + acc_ref[...]


def _swiglu(x, gain, wg, wu, wd, cast_weights, *, tm, tf):
    t, d = x.shape
    ff = wg.shape[1]
    grid = (t // tm, ff // tf)
    casts = _Casts(cast_weights, grid)
    out, *cast = pl.pallas_call(
        casts.wrap(_swiglu_kernel, n_in=5, n_out=1),
        grid=grid,
        in_specs=[pl.BlockSpec((tm, d), lambda i, f: (i, 0)),
                  pl.BlockSpec((1, d), lambda i, f: (0, 0)),
                  pl.BlockSpec((d, tf), lambda i, f: (0, f)),
                  pl.BlockSpec((d, tf), lambda i, f: (0, f)),
                  pl.BlockSpec((tf, d), lambda i, f: (f, 0)),
                  *casts.specs],
        out_specs=[pl.BlockSpec((tm, d), lambda i, f: (i, 0)), *casts.specs],
        out_shape=[jax.ShapeDtypeStruct((t, d), F32), *casts.out_shapes],
        scratch_shapes=[pltpu.VMEM((tm, d), BF16), pltpu.VMEM((tm, d), F32)],
        compiler_params=_params("parallel", "arbitrary"),
        name="swiglu",
    )(x, gain.reshape(1, d), wg, wu, wd, *casts.views)
    return out, casts.restore(cast)


def _ple_math(x, g_ref, wg_ref, p_ref, wp_ref, fg_ref, final_norm):
    h = _rms_norm(x, g_ref[...]).astype(BF16)
    gate = _sigmoid(_dot(h, wg_ref[...]))
    y = x + gate * _dot(p_ref[...].astype(BF16), wp_ref[...])
    if final_norm:
        y = _rms_norm(y, fg_ref[...])
    return y


def _ple_kernel(x_ref, g_ref, wg_ref, p_ref, wp_ref, fg_ref, o_ref, *, final_norm):
    o_ref[...] = _ple_math(x_ref[...], g_ref, wg_ref, p_ref, wp_ref, fg_ref, final_norm)


def _ple(x, gain, wg, p, wp, final_gain, *, tm, final_norm):
    t, d = x.shape
    dp = p.shape[1]
    return pl.pallas_call(
        functools.partial(_ple_kernel, final_norm=final_norm),
        grid=(t // tm,),
        in_specs=[pl.BlockSpec((tm, d), lambda i: (i, 0)),
                  pl.BlockSpec((1, d), lambda i: (0, 0)),
                  pl.BlockSpec((d, d), lambda i: (0, 0)),
                  pl.BlockSpec((tm, dp), lambda i: (i, 0)),
                  pl.BlockSpec((dp, d), lambda i: (0, 0)),
                  pl.BlockSpec((1, d), lambda i: (0, 0))],
        out_specs=pl.BlockSpec((tm, d), lambda i: (i, 0)),
        out_shape=jax.ShapeDtypeStruct((t, d), F32),
        compiler_params=_params("parallel"),
        name="ple",
    )(x, gain.reshape(1, d), wg, p, wp, final_gain.reshape(1, d))


def _pool_router_kernel(x_ref, g_ref, pw_ref, ps_ref, fg_ref, wr_ref, lt_ref,
                        x1_ref, hg_ref, route_ref, counts_ref, ext_ref, cnt_ref,
                        *, n_experts):
    s = pl.program_id(1)
    ts, d = x_ref.shape
    n_groups = len(POOL_WINDOWS)
    gd = d // n_groups

    @pl.when(s == 0)
    def _():
        ext_ref[0:POOL_HALO, :] = jnp.zeros((POOL_HALO, d), F32)

    @pl.when(s > 0)
    def _():
        ext_ref[0:POOL_HALO, :] = ext_ref[ts:ts + POOL_HALO, :]

    x = x_ref[...]
    h = _rms_norm(x, g_ref[...])
    ext_ref[POOL_HALO:, :] = h

    def window_sum(cols, first_row, n_rows, w):
        if w <= SUBLANES:
            acc = ext_ref[pl.ds(first_row, n_rows), cols]
            for j in range(1, w):
                acc = acc + ext_ref[pl.ds(first_row - j, n_rows), cols]
            return acc
        half = w // 2
        part = window_sum(cols, first_row - half, n_rows + half, half)
        return part[half:] + part[:n_rows]

    pos = (s * ts + 1 + lax.broadcasted_iota(jnp.int32, (ts, 1), 0)).astype(F32)
    ys = []
    for gi, w in enumerate(POOL_WINDOWS):
        cols = slice(gi * gd, (gi + 1) * gd)
        mean = window_sum(cols, POOL_HALO, ts, w) / jnp.minimum(pos, float(w))
        diff = (mean - h[:, cols]).astype(BF16)
        ys.append(_dot(diff, pw_ref[gi]))
    y = jnp.concatenate(ys, axis=1)
    x1 = x + y * ps_ref[...]
    x1_ref[...] = x1

    h2 = _rms_norm(x1, fg_ref[...])
    hg_ref[:, 0:d] = h2

    hi = h2.astype(BF16)
    mid = (h2 - hi.astype(F32)).astype(BF16)
    p_hi = _dot(hi, wr_ref[...])
    p_mid = _dot(mid, wr_ref[...])
    logits = (p_hi + pltpu.roll(p_hi, LANES - n_experts, 1)
              + pltpu.roll(p_hi, LANES - 2 * n_experts, 1)
              + p_mid + pltpu.roll(p_mid, LANES - n_experts, 1))
    lane = lax.broadcasted_iota(jnp.int32, logits.shape, 1)
    valid = lane < n_experts
    logits = jnp.where(valid, logits, -jnp.inf)
    ex = jnp.exp(logits - jnp.max(logits, axis=-1, keepdims=True))
    probs = ex / jnp.sum(ex, axis=-1, keepdims=True)
    gates = jnp.zeros_like(probs)
    rest = jnp.where(valid, probs, -1.0)
    tops = []
    for _ in range(TOP_K):
        m = jnp.max(rest, axis=-1, keepdims=True)
        first = jnp.min(jnp.where(rest == m, lane, LANES), axis=-1, keepdims=True)
        sel = lane == first
        tops.append((m, sel))
        rest = jnp.where(sel, -1.0, rest)
    total = tops[0][0]
    for m, _ in tops[1:]:
        total = total + m
    for m, sel in tops:
        gates = jnp.where(sel, m / total, gates)
    hg_ref[:, d:d + LANES] = gates

    first_step = (pl.program_id(0) == 0) & (s == 0)

    @pl.when(first_step)
    def _():
        cnt_ref[...] = jnp.zeros_like(cnt_ref)

    chosen = tops[0][1]
    for _, sel in tops[1:]:
        chosen = chosen | sel
    chosen = jnp.where(chosen, 1.0, 0.0)
    before = cnt_ref[...] + _dot(lt_ref[...], chosen.astype(BF16))
    cnt_ref[...] = cnt_ref[...] + jnp.sum(chosen, axis=0, keepdims=True)
    counts_ref[...] = cnt_ref[...]
    lane_f = lane.astype(F32)
    route = jnp.zeros((ts, LANES), F32)
    for k, (_, sel) in enumerate(tops):
        expert_k = jnp.sum(jnp.where(sel, lane_f, 0.0), axis=-1, keepdims=True)
        order_k = jnp.sum(jnp.where(sel, before, 0.0), axis=-1, keepdims=True)
        route = jnp.where(lane == k, expert_k, route)
        route = jnp.where(lane == TOP_K + k, order_k, route)
    route_ref[...] = route.astype(jnp.int32)


def _pool_router(x, gain, pool_w, pool_scale, ffn_gain, w_router, *, batch, ts):
    t, d = x.shape
    s_len = t // batch
    n_s = s_len // ts
    n_groups, gd, _ = pool_w.shape
    n_experts = w_router.shape[1]
    assert 3 * n_experts <= LANES
    w_hi = w_router.astype(BF16)
    r1 = w_router - w_hi.astype(F32)
    w_mid = r1.astype(BF16)
    w_lo = (r1 - w_mid.astype(F32)).astype(BF16)
    wr = jnp.zeros((d, LANES), BF16).at[:, :3 * n_experts].set(
        jnp.concatenate([w_hi, w_mid, w_lo], axis=1))
    idx = jnp.arange(ts)
    lower = (idx[None, :] < idx[:, None]).astype(BF16)
    row = lambda b, s: (b * n_s + s, 0)
    const = lambda b, s: (0, 0)
    return pl.pallas_call(
        functools.partial(_pool_router_kernel, n_experts=n_experts),
        grid=(batch, n_s),
        in_specs=[pl.BlockSpec((ts, d), row),
                  pl.BlockSpec((1, d), const),
                  pl.BlockSpec((n_groups, gd, gd), lambda b, s: (0, 0, 0)),
                  pl.BlockSpec((1, d), const),
                  pl.BlockSpec((1, d), const),
                  pl.BlockSpec((d, LANES), const),
                  pl.BlockSpec((ts, ts), const)],
        out_specs=[pl.BlockSpec((ts, d), row),
                   pl.BlockSpec((ts, d + LANES), row),
                   pl.BlockSpec((ts, LANES), row),
                   pl.BlockSpec((1, LANES), const)],
        out_shape=[jax.ShapeDtypeStruct((t, d), F32),
                   jax.ShapeDtypeStruct((t, d + LANES), F32),
                   jax.ShapeDtypeStruct((t, LANES), jnp.int32),
                   jax.ShapeDtypeStruct((1, LANES), F32)],
        scratch_shapes=[pltpu.VMEM((POOL_HALO + ts, d), F32),
                        pltpu.VMEM((1, LANES), F32)],
        compiler_params=_params("arbitrary", "arbitrary"),
        name="pool_router",
    )(x, gain.reshape(1, d), pool_w, pool_scale.reshape(1, d), ffn_gain.reshape(1, d), wr, lower)


def _dispatch_kernel(slots_ref, bounds_ref, hg_ref, xs_ref, zero_ref, sem, zero_sem,
                     *, n_experts, tile):
    i = pl.program_id(0)
    td = hg_ref.shape[0]

    for r in range(td):
        for k in range(TOP_K):
            slot = slots_ref[(i * td + r) * TOP_K + k]
            pltpu.make_async_copy(hg_ref.at[pl.ds(r, 1), :],
                                  xs_ref.at[pl.ds(slot, 1), :], sem).start()

    def zero_copies():
        copies = []

        def block(first, size):
            return pltpu.make_async_copy(zero_ref.at[pl.ds(0, size), :],
                                         xs_ref.at[pl.ds(first, size), :], zero_sem)

        for e in range(n_experts):
            first_unused = bounds_ref[e]
            end = bounds_ref[n_experts + e]
            aligned = (first_unused + SUBLANES - 1) // SUBLANES * SUBLANES
            for r in range(SUBLANES - 1):
                copies.append((first_unused + r < aligned, block(first_unused + r, 1)))
            left = end - aligned
            pos = aligned
            size = tile // 2
            while size >= SUBLANES:
                needed = (left & size) != 0
                copies.append((needed, block(pl.multiple_of(pos, SUBLANES), size)))
                pos = pos + jnp.where(needed, size, 0)
                size //= 2
        total = bounds_ref[2 * n_experts]
        for b in range(n_experts):
            first = pl.multiple_of(total + b * tile, tile)
            copies.append((first < xs_ref.shape[0], block(first, tile)))
        return copies

    @pl.when(i == 0)
    def _():
        zero_ref[...] = jnp.zeros_like(zero_ref)
        for action in ("start", "wait"):
            for needed, copy in zero_copies():
                pl.when(needed)(getattr(copy, action))

    for _ in range(TOP_K):
        pltpu.make_async_copy(hg_ref, xs_ref.at[pl.ds(0, td), :], sem).wait()


def _dispatch(slots_flat, bounds, hg, *, td, n_experts, tile):
    t, width = hg.shape
    n_rows = t * TOP_K + n_experts * tile
    return pl.pallas_call(
        functools.partial(_dispatch_kernel, n_experts=n_experts, tile=tile),
        grid_spec=pltpu.PrefetchScalarGridSpec(
            num_scalar_prefetch=2,
            grid=(t // td,),
            in_specs=[pl.BlockSpec((td, width), lambda i, slots, bounds: (i, 0))],
            out_specs=pl.BlockSpec(memory_space=pl.ANY),
            scratch_shapes=[pltpu.VMEM((tile, width), F32),
                            pltpu.SemaphoreType.DMA,
                            pltpu.SemaphoreType.DMA]),
        out_shape=jax.ShapeDtypeStruct((n_rows, width), F32),
        compiler_params=_params("arbitrary"),
        name="moe_dispatch",
    )(slots_flat, bounds, hg)


def _experts_kernel(te_ref, nt_ref, xs_ref, wg_ref, wu_ref, wd_ref, ys_ref):
    i = pl.program_id(0)
    d = wg_ref.shape[0]

    @pl.when(i < nt_ref[0])
    def _():
        rows = xs_ref[:, 0:d].astype(BF16)
        gates = xs_ref[:, d:d + LANES]
        lane = lax.broadcasted_iota(jnp.int32, gates.shape, 1)
        ge = jnp.sum(jnp.where(lane == te_ref[i], gates, 0.0), axis=-1, keepdims=True)
        gate = _dot(rows, wg_ref[...])
        up = _dot(rows, wu_ref[...])
        act = (ge * (gate * _sigmoid(gate) * up)).astype(BF16)
        ys_ref[...] = _dot(act, wd_ref[...])

    @pl.when(i >= nt_ref[0])
    def _():
        ys_ref[...] = jnp.zeros_like(ys_ref)


def _experts(tile_expert, n_tiles, xs, wg, wu, wd, cast_weights, *, tile):
    n_rows, width = xs.shape
    n_experts, d, ffe = wg.shape
    row_map = lambda i, te, nt: (i, 0)
    w_map = lambda i, te, nt: (te[i], 0, 0)
    grid = (n_rows // tile,)
    casts = _Casts(cast_weights, grid)
    out, *cast = pl.pallas_call(
        casts.wrap(_experts_kernel, n_in=4, n_out=1, n_prefetch=2),
        grid_spec=pltpu.PrefetchScalarGridSpec(
            num_scalar_prefetch=2,
            grid=grid,
            in_specs=[pl.BlockSpec((tile, width), row_map),
                      pl.BlockSpec((None, d, ffe), w_map),
                      pl.BlockSpec((None, d, ffe), w_map),
                      pl.BlockSpec((None, ffe, d), w_map),
                      *casts.specs],
            out_specs=[pl.BlockSpec((tile, d), row_map), *casts.specs]),
        out_shape=[jax.ShapeDtypeStruct((n_rows, d), F32), *casts.out_shapes],
        compiler_params=_params("arbitrary"),
        name="moe_experts",
    )(tile_expert, n_tiles, xs, wg, wu, wd, *casts.views)
    return out, casts.restore(cast)


def _routing_tables(route, counts, *, n_experts, tile):
    t = route.shape[0]
    counts = counts[0, :n_experts].astype(jnp.int32)
    padded = (counts + tile - 1) // tile * tile
    ends = jnp.cumsum(padded)
    starts = ends - padded
    expert, order = route[:, :TOP_K], route[:, TOP_K:2 * TOP_K]
    slots = (starts[expert] + order).reshape(-1)
    bounds = jnp.concatenate([starts + counts, ends, ends[-1:]]).astype(jnp.int32)
    n_tiles_max = (t * TOP_K) // tile + n_experts
    first_row = jnp.arange(n_tiles_max, dtype=jnp.int32) * tile
    tile_expert = jnp.minimum(jnp.sum(first_row[:, None] >= ends[None, :], axis=1),
                              n_experts - 1).astype(jnp.int32)
    n_tiles = (ends[-1:] // tile).astype(jnp.int32)
    return slots.astype(jnp.int32), bounds, tile_expert, n_tiles


def _ple_combine_kernel(slots_ref, x_ref, g_ref, wg_ref, p_ref, wp_ref, fg_ref, ys_ref,
                        o_ref, ybuf, sem, *, final_norm):
    i = pl.program_id(0)
    n = pl.num_programs(0)
    tm = x_ref.shape[0]

    d = x_ref.shape[1]
    chunk_cols = d // COMBINE_CHUNKS
    chunk_rows = tm // COMBINE_CHUNKS

    def start_rows(step, buf, first, last):
        for r in range(first, last):
            for k in range(TOP_K):
                slot = slots_ref[(step * tm + r) * TOP_K + k]
                pltpu.make_async_copy(ys_ref.at[pl.ds(slot, 1), :],
                                      ybuf.at[buf, k, pl.ds(r, 1), :], sem.at[buf]).start()

    def wait_tile(buf):
        for k in range(TOP_K):
            pltpu.make_async_copy(ys_ref.at[pl.ds(0, tm), :], ybuf.at[buf, k],
                                  sem.at[buf]).wait()

    cur = i % 2

    @pl.when(i == 0)
    def _():
        start_rows(i, cur, 0, tm)

    wait_tile(cur)
    x = x_ref[...]
    for k in range(TOP_K):
        x = x + ybuf[cur, k]
    h = _rms_norm(x, g_ref[...]).astype(BF16)
    pb = p_ref[...].astype(BF16)
    nxt = jnp.minimum(i + 1, n - 1)
    for c in range(COMBINE_CHUNKS):
        cols = slice(c * chunk_cols, (c + 1) * chunk_cols)
        gate = _sigmoid(_dot(h, wg_ref[:, cols]))
        o_ref[:, cols] = x[:, cols] + gate * _dot(pb, wp_ref[:, cols])
        start_rows(nxt, 1 - cur, c * chunk_rows, (c + 1) * chunk_rows)
    if final_norm:
        o_ref[...] = _rms_norm(o_ref[...], fg_ref[...])

    @pl.when(i == n - 1)
    def _():
        wait_tile(1 - cur)


def _ple_combine(slots_flat, x, gain, wg, p, wp, final_gain, ys, *, tm, final_norm):
    t, d = x.shape
    dp = p.shape[1]
    row = lambda i, slots: (i, 0)
    const = lambda i, slots: (0, 0)
    return pl.pallas_call(
        functools.partial(_ple_combine_kernel, final_norm=final_norm),
        grid_spec=pltpu.PrefetchScalarGridSpec(
            num_scalar_prefetch=1,
            grid=(t // tm,),
            in_specs=[pl.BlockSpec((tm, d), row),
                      pl.BlockSpec((1, d), const),
                      pl.BlockSpec((d, d), const),
                      pl.BlockSpec((tm, dp), row),
                      pl.BlockSpec((dp, d), const),
                      pl.BlockSpec((1, d), const),
                      pl.BlockSpec(memory_space=pl.ANY)],
            out_specs=pl.BlockSpec((tm, d), row),
            scratch_shapes=[pltpu.VMEM((2, TOP_K, tm, d), F32),
                            pltpu.SemaphoreType.DMA((2,))]),
        out_shape=jax.ShapeDtypeStruct((t, d), F32),
        compiler_params=_params("arbitrary"),
        name="ple_combine",
    )(slots_flat, x, gain.reshape(1, d), wg, p, wp, final_gain.reshape(1, d), ys)


def kernel(x, p, final_norm_gain, lb_table, mix_norm_even, w_in_even, hgrn_norm_gain, conv_w, w_out_even, ffn_norm_even, w_gate_dense, w_up_dense, w_down_dense, mix_norm_odd, pool_w, pool_scale, ffn_norm_odd, w_router, w_gate_exp, w_up_exp, w_down_exp, ple_norm, ple_gate_w, ple_proj):
    batch, s_len, d = x.shape
    t = batch * s_len
    depth = p.shape[0]
    assert depth == 2, "the weight-cast schedule below is written for one layer pair"
    n_experts = w_router.shape[-1]
    tile = EXPERT_ROW_TILE
    xs = x.reshape(t, d)

    u, (w_out, w_gate, w_up, w_down) = _norm_matmul(
        xs, mix_norm_even[0], w_in_even[0].astype(BF16),
        [w_out_even[0], w_gate_dense[0], w_up_dense[0], w_down_dense[0]], tm=1024, tn=1024)
    a, b = _hgrn_conv(u, lb_table, hgrn_norm_gain[0], conv_w[0], batch=batch, rows=512, layer=0)
    xs, (ple_gate0, ple_proj0, pool_wb) = _out_proj(
        xs, a.reshape(t, -1), b.reshape(t, -1), w_out,
        [ple_gate_w[0], ple_proj[0], pool_w[0]], tm=512, tn=d)
    xs, (wg_exp, wu_exp, wd_exp) = _swiglu(
        xs, ffn_norm_even[0], w_gate, w_up, w_down,
        [w_gate_exp[0], w_up_exp[0], w_down_exp[0]], tm=512, tf=512)
    xs = _ple(xs, ple_norm[0], ple_gate0, p[0].reshape(t, -1), ple_proj0, final_norm_gain,
              tm=512, final_norm=False)

    xs, hg, route, counts = _pool_router(
        xs, mix_norm_odd[0], pool_wb, pool_scale[0], ffn_norm_odd[0], w_router[0],
        batch=batch, ts=512)
    slots, bounds, tile_expert, n_tiles = _routing_tables(
        route, counts, n_experts=n_experts, tile=tile)
    rows = _dispatch(slots, bounds, hg, td=512, n_experts=n_experts, tile=tile)
    ys, (ple_gate1, ple_proj1) = _experts(
        tile_expert, n_tiles, rows, wg_exp, wu_exp, wd_exp, [ple_gate_w[1], ple_proj[1]],
        tile=tile)
    xs = _ple_combine(slots, xs, ple_norm[1], ple_gate1, p[1].reshape(t, -1), ple_proj1,
                      final_norm_gain, ys, tm=256, final_norm=True)
    return xs.reshape(batch, s_len, d)
```

```python
import functools
import math

import jax
import jax.numpy as jnp
from jax import lax
from jax.experimental import pallas as pl
from jax.experimental.pallas import tpu as pltpu

F32 = jnp.float32
BF16 = jnp.bfloat16

EPS = 1e-6
HGRN_HEADS = 8
HGRN_CHUNK = 64
CONV_WIDTH = 3
POOL_WINDOWS = (2, 4, 8, 16)
POOL_HALO = 16
TOP_K = 2
EXPERT_ROW_TILE = 512
COMBINE_CHUNKS = 8
LANES = 128
SUBLANES = 8
BF16_SUBLANES = 16
VMEM_LIMIT_BYTES = 56 * 1024 * 1024


def _params(*semantics):
    return pltpu.CompilerParams(dimension_semantics=semantics,
                                vmem_limit_bytes=VMEM_LIMIT_BYTES)


def _rms_norm(x, gain):
    ms = jnp.mean(x * x, axis=-1, keepdims=True)
    return x * lax.rsqrt(ms + EPS) * gain


def _sigmoid(x):
    return 1.0 / (1.0 + jnp.exp(-x))


def _dot(a, b):
    return jnp.dot(a, b, preferred_element_type=F32)


def _dot_nt(a, b):
    return lax.dot_general(a, b, (((1,), (1,)), ((), ())), preferred_element_type=F32)


class _Casts:
    def __init__(self, weights, grid):
        n_steps = math.prod(grid)
        self.shapes = [w.shape for w in weights]
        self.views, self.specs, self.out_shapes = [], [], []
        for w in weights:
            cols = w.shape[-1]
            rows = w.size // cols
            block_rows = next(br for br in range(BF16_SUBLANES, rows + 1, BF16_SUBLANES)
                              if rows % br == 0 and rows // br <= n_steps)
            self.views.append(w.reshape(rows, cols))
            self.specs.append(pl.BlockSpec(
                (block_rows, cols), functools.partial(self._index_map, grid,
                                                      rows // block_rows)))
            self.out_shapes.append(jax.ShapeDtypeStruct((rows, cols), BF16))

    @staticmethod
    def _index_map(grid, n_blocks, *args):
        step = 0
        for size, idx in zip(grid, args):
            step = step * size + idx
        return jnp.minimum(step, n_blocks - 1), 0

    def __len__(self):
        return len(self.views)

    def wrap(self, body, n_in, n_out, n_prefetch=0):
        n = len(self)

        def kernel(*refs):
            ins_end = n_prefetch + n_in
            outs_start = ins_end + n
            outs_end = outs_start + n_out
            body(*refs[:ins_end], *refs[outs_start:outs_end], *refs[outs_end + n:])
            for src, dst in zip(refs[ins_end:outs_start], refs[outs_end:outs_end + n]):
                dst[...] = src[...].astype(BF16)

        return kernel

    def restore(self, outs):
        return [o.reshape(shape) for o, shape in zip(outs, self.shapes)]


def _norm_matmul_kernel(x_ref, g_ref, w_ref, o_ref, h_ref):
    @pl.when(pl.program_id(1) == 0)
    def _():
        h_ref[...] = _rms_norm(x_ref[...], g_ref[...]).astype(BF16)

    o_ref[...] = _dot(h_ref[...], w_ref[...])


def _norm_matmul(x, gain, w, cast_weights, *, tm, tn):
    t, d = x.shape
    n = w.shape[1]
    grid = (t // tm, n // tn)
    casts = _Casts(cast_weights, grid)
    out, *cast = pl.pallas_call(
        casts.wrap(_norm_matmul_kernel, n_in=3, n_out=1),
        grid=grid,
        in_specs=[pl.BlockSpec((tm, d), lambda i, j: (i, 0)),
                  pl.BlockSpec((1, d), lambda i, j: (0, 0)),
                  pl.BlockSpec((d, tn), lambda i, j: (0, j)),
                  *casts.specs],
        out_specs=[pl.BlockSpec((tm, tn), lambda i, j: (i, j)), *casts.specs],
        out_shape=[jax.ShapeDtypeStruct((t, n), F32), *casts.out_shapes],
        scratch_shapes=[pltpu.VMEM((tm, d), BF16)],
        compiler_params=_params("parallel", "arbitrary"),
        name="norm_matmul",
    )(x, gain.reshape(1, d), w, *casts.views)
    return out, casts.restore(cast)


def _hgrn_conv_kernel(tri_ref, q_ref, f_ref, i_ref, g_ref, gb_ref, gc_ref, vc_ref,
                      lbt_ref, gain_ref, cw_ref, a_ref, b_ref, *, rows, layer):
    s_len, dh = q_ref.shape
    ch = HGRN_CHUNK
    ct = tri_ref.shape[0]

    lbt = lbt_ref[...]
    e = jnp.exp(lbt - jnp.max(lbt, axis=0, keepdims=True))
    lb = (jnp.sum(e[0:layer + 1, :], axis=0, keepdims=True)
          / jnp.sum(e, axis=0, keepdims=True))
    gain = gain_ref[...]

    tri = tri_ref[...]
    row = lax.broadcasted_iota(jnp.int32, (ch, ch), 0)
    col = lax.broadcasted_iota(jnp.int32, (ch, ch), 1)
    causal = row >= col

    def body(t, st):
        r0 = pl.multiple_of(t * rows, rows)
        qr = q_ref[pl.ds(r0, rows), :]
        q = qr * _sigmoid(qr)
        f = lb + (1.0 - lb) * _sigmoid(f_ref[pl.ds(r0, rows), :])
        k = 1.0 - f
        v = i_ref[pl.ds(r0, rows), :].astype(BF16)
        lf = jnp.log(f)
        hi = lf.astype(BF16)
        r1 = lf - hi.astype(F32)
        mid = r1.astype(BF16)
        lo = (r1 - mid.astype(F32)).astype(BF16)
        parts = jnp.concatenate([hi, mid, lo], axis=1)
        cums = []
        for j in range(rows // ct):
            c3 = _dot(tri, parts[j * ct:(j + 1) * ct])
            cums.append(c3[:, :dh] + c3[:, dh:2 * dh] + c3[:, 2 * dh:])
        cum = jnp.concatenate(cums, axis=0)

        outs = []
        for c in range(rows // ch):
            sl = slice(c * ch, (c + 1) * ch)
            cum_c = cum[sl]
            ref = cum_c[ch // 2:ch // 2 + 1, :]
            last = cum_c[ch - 1:ch, :]
            q_c, k_c, v_c = q[sl], k[sl], v[sl]
            qe = (q_c * jnp.exp(cum_c - ref)).astype(BF16)
            ke = (k_c * jnp.exp(ref - cum_c)).astype(BF16)
            scores = jnp.where(causal, _dot_nt(qe, ke), 0.0)
            qc = (q_c * jnp.exp(cum_c)).astype(BF16)
            o_c = _dot(scores.astype(BF16), v_c) + _dot_nt(qc, st.astype(BF16))
            kd = (k_c * jnp.exp(last - cum_c)).astype(BF16)
            kv_t = lax.dot_general(v_c, kd, (((0,), (0,)), ((), ())),
                                   preferred_element_type=F32)
            st = jnp.exp(last) * st + kv_t
            outs.append(o_c)
        o = jnp.concatenate(outs, axis=0)
        o = o * lax.rsqrt(jnp.mean(o * o, axis=-1, keepdims=True) + EPS)
        gr = g_ref[pl.ds(r0, rows), :]
        a_ref[pl.ds(r0, rows), :] = (o * gain * (gr * _sigmoid(gr))).astype(BF16)
        return st

    lax.fori_loop(0, s_len // rows, body, jnp.zeros((dh, dh), F32))

    tt = gc_ref[...] * vc_ref[...]
    ridx = lax.broadcasted_iota(jnp.int32, tt.shape, 0)
    cw = cw_ref[...]
    conv = tt * cw[CONV_WIDTH - 1:CONV_WIDTH, :]
    for back in range(1, CONV_WIDTH):
        shifted = jnp.where(ridx >= back, pltpu.roll(tt, back, 0), 0.0)
        conv = conv + shifted * cw[CONV_WIDTH - 1 - back:CONV_WIDTH - back, :]
    b_ref[...] = (gb_ref[...] * conv).astype(BF16)


def _hgrn_conv(u, lb_table, hgrn_gain, conv_w, cast_weights, *, batch, rows, layer):
    t, n_in = u.shape
    s_len = t // batch
    d_hgrn = hgrn_gain.shape[0]
    dh = d_hgrn // HGRN_HEADS
    d_conv = conv_w.shape[1]
    nh = HGRN_HEADS
    assert d_conv // dh == nh and n_in == 4 * d_hgrn + 3 * d_conv
    u3 = u.reshape(batch, s_len, n_in)

    idx = jnp.arange(2 * HGRN_CHUNK)
    tri = ((idx[:, None] >= idx[None, :])
           & (idx[:, None] // HGRN_CHUNK == idx[None, :] // HGRN_CHUNK)).astype(BF16)

    def col(off):
        return pl.BlockSpec((None, s_len, dh), lambda b, h, off=off: (b, 0, off + h))

    n_lb = lb_table.shape[0]
    out_spec = pl.BlockSpec((None, s_len, dh), lambda b, h: (b, 0, h))
    grid = (batch, nh)
    casts = _Casts(cast_weights, grid)
    a, b, *cast = pl.pallas_call(
        casts.wrap(functools.partial(_hgrn_conv_kernel, rows=rows, layer=layer),
                   n_in=11, n_out=2),
        grid=grid,
        in_specs=[pl.BlockSpec(tri.shape, lambda b, h: (0, 0)),
                  col(0), col(nh), col(2 * nh), col(3 * nh),
                  col(4 * nh), col(5 * nh), col(6 * nh),
                  pl.BlockSpec((n_lb, dh), lambda b, h: (0, h)),
                  pl.BlockSpec((1, dh), lambda b, h: (0, h)),
                  pl.BlockSpec((CONV_WIDTH, dh), lambda b, h: (0, h)),
                  *casts.specs],
        out_specs=[out_spec, out_spec, *casts.specs],
        out_shape=[jax.ShapeDtypeStruct((batch, s_len, d_hgrn), BF16),
                   jax.ShapeDtypeStruct((batch, s_len, d_conv), BF16),
                   *casts.out_shapes],
        compiler_params=_params("parallel", "parallel"),
        name="hgrn_conv",
    )(tri, u3, u3, u3, u3, u3, u3, u3, lb_table, hgrn_gain.reshape(1, d_hgrn), conv_w,
      *casts.views)
    return a, b, casts.restore(cast)


def _out_proj_kernel(x_ref, a_ref, b_ref, wa_ref, wb_ref, o_ref):
    o_ref[...] = x_ref[...] + _dot(a_ref[...], wa_ref[...]) + _dot(b_ref[...], wb_ref[...])


def _out_proj(x, a, b, w, cast_weights, *, tm, tn):
    t, d = x.shape
    ka, kb = a.shape[1], b.shape[1]
    assert ka == kb
    grid = (t // tm, d // tn)
    casts = _Casts(cast_weights, grid)
    out, *cast = pl.pallas_call(
        casts.wrap(_out_proj_kernel, n_in=5, n_out=1),
        grid=grid,
        in_specs=[pl.BlockSpec((tm, tn), lambda i, j: (i, j)),
                  pl.BlockSpec((tm, ka), lambda i, j: (i, 0)),
                  pl.BlockSpec((tm, kb), lambda i, j: (i, 0)),
                  pl.BlockSpec((ka, tn), lambda i, j: (0, j)),
                  pl.BlockSpec((kb, tn), lambda i, j: (1, j)),
                  *casts.specs],
        out_specs=[pl.BlockSpec((tm, tn), lambda i, j: (i, j)), *casts.specs],
        out_shape=[jax.ShapeDtypeStruct((t, d), F32), *casts.out_shapes],
        compiler_params=_params("parallel", "arbitrary"),
        name="out_proj",
    )(x, a, b, w, w, *casts.views)
    return out, casts.restore(cast)


def _swiglu_kernel(x_ref, g_ref, wg_ref, wu_ref, wd_ref, o_ref, h_ref, acc_ref):
    f = pl.program_id(1)

    @pl.when(f == 0)
    def _():
        h_ref[...] = _rms_norm(x_ref[...], g_ref[...]).astype(BF16)
        acc_ref[...] = jnp.zeros_like(acc_ref)

    h = h_ref[...]
    gate = _dot(h, wg_ref[...])
    up = _dot(h, wu_ref[...])
    act = (gate * _sigmoid(gate) * up).astype(BF16)
    acc_ref[...] += _dot(act, wd_ref[...])

    @pl.when(f == pl.num_programs(1) - 1)
    def _():
        o_ref[...] = x_ref[...] + acc_ref[...]


def _swiglu(x, gain, wg, wu, wd, cast_weights, *, tm, tf):
    t, d = x.shape
    ff = wg.shape[1]
    grid = (t // tm, ff // tf)
    casts = _Casts(cast_weights, grid)
    out, *cast = pl.pallas_call(
        casts.wrap(_swiglu_kernel, n_in=5, n_out=1),
        grid=grid,
        in_specs=[pl.BlockSpec((tm, d), lambda i, f: (i, 0)),
                  pl.BlockSpec((1, d), lambda i, f: (0, 0)),
                  pl.BlockSpec((d, tf), lambda i, f: (0, f)),
                  pl.BlockSpec((d, tf), lambda i, f: (0, f)),
                  pl.BlockSpec((tf, d), lambda i, f: (f, 0)),
                  *casts.specs],
        out_specs=[pl.BlockSpec((tm, d), lambda i, f: (i, 0)), *casts.specs],
        out_shape=[jax.ShapeDtypeStruct((t, d), F32), *casts.out_shapes],
        scratch_shapes=[pltpu.VMEM((tm, d), BF16), pltpu.VMEM((tm, d), F32)],
        compiler_params=_params("parallel", "arbitrary"),
        name="swiglu",
    )(x, gain.reshape(1, d), wg, wu, wd, *casts.views)
    return out, casts.restore(cast)


def _ple_math(x, g_ref, wg_ref, p_ref, wp_ref, fg_ref, final_norm):
    h = _rms_norm(x, g_ref[...]).astype(BF16)
    gate = _sigmoid(_dot(h, wg_ref[...]))
    y = x + gate * _dot(p_ref[...].astype(BF16), wp_ref[...])
    if final_norm:
        y = _rms_norm(y, fg_ref[...])
    return y


def _ple_kernel(x_ref, g_ref, wg_ref, p_ref, wp_ref, fg_ref, o_ref, *, final_norm):
    o_ref[...] = _ple_math(x_ref[...], g_ref, wg_ref, p_ref, wp_ref, fg_ref, final_norm)


def _ple(x, gain, wg, p, layer, wp, final_gain, *, tm, final_norm):
    t, d = x.shape
    dp = p.shape[-1]
    return pl.pallas_call(
        functools.partial(_ple_kernel, final_norm=final_norm),
        grid=(t // tm,),
        in_specs=[pl.BlockSpec((tm, d), lambda i: (i, 0)),
                  pl.BlockSpec((1, d), lambda i: (0, 0)),
                  pl.BlockSpec((d, d), lambda i: (0, 0)),
                  pl.BlockSpec((None, tm, dp), lambda i: (layer, i, 0)),
                  pl.BlockSpec((dp, d), lambda i: (0, 0)),
                  pl.BlockSpec((1, d), lambda i: (0, 0))],
        out_specs=pl.BlockSpec((tm, d), lambda i: (i, 0)),
        out_shape=jax.ShapeDtypeStruct((t, d), F32),
        compiler_params=_params("parallel"),
        name="ple",
    )(x, gain.reshape(1, d), wg, p, wp, final_gain.reshape(1, d))


def _pool_router_kernel(x_ref, g_ref, pw_ref, ps_ref, fg_ref, wr_ref, lt_ref,
                        x1_ref, hg_ref, route_ref, counts_ref, ext_ref, cnt_ref,
                        *, n_experts):
    s = pl.program_id(1)
    ts, d = x_ref.shape
    n_groups = len(POOL_WINDOWS)
    gd = d // n_groups

    @pl.when(s == 0)
    def _():
        ext_ref[0:POOL_HALO, :] = jnp.zeros((POOL_HALO, d), F32)

    @pl.when(s > 0)
    def _():
        ext_ref[0:POOL_HALO, :] = ext_ref[ts:ts + POOL_HALO, :]

    x = x_ref[...]
    h = _rms_norm(x, g_ref[...])
    ext_ref[POOL_HALO:, :] = h

    def window_sum(cols, first_row, n_rows, w):
        if w <= SUBLANES:
            acc = ext_ref[pl.ds(first_row, n_rows), cols]
            for j in range(1, w):
                acc = acc + ext_ref[pl.ds(first_row - j, n_rows), cols]
            return acc
        half = w // 2
        part = window_sum(cols, first_row - half, n_rows + half, half)
        return part[half:] + part[:n_rows]

    pos = (s * ts + 1 + lax.broadcasted_iota(jnp.int32, (ts, 1), 0)).astype(F32)
    ys = []
    for gi, w in enumerate(POOL_WINDOWS):
        cols = slice(gi * gd, (gi + 1) * gd)
        mean = window_sum(cols, POOL_HALO, ts, w) / jnp.minimum(pos, float(w))
        diff = (mean - h[:, cols]).astype(BF16)
        ys.append(_dot(diff, pw_ref[gi]))
    y = jnp.concatenate(ys, axis=1)
    x1 = x + y * ps_ref[...]
    x1_ref[...] = x1

    h2 = _rms_norm(x1, fg_ref[...])
    hg_ref[:, 0:d] = h2

    hi = h2.astype(BF16)
    mid = (h2 - hi.astype(F32)).astype(BF16)
    p_hi = _dot(hi, wr_ref[...])
    p_mid = _dot(mid, wr_ref[...])
    logits = (p_hi + pltpu.roll(p_hi, LANES - n_experts, 1)
              + pltpu.roll(p_hi, LANES - 2 * n_experts, 1)
              + p_mid + pltpu.roll(p_mid, LANES - n_experts, 1))
    lane = lax.broadcasted_iota(jnp.int32, logits.shape, 1)
    valid = lane < n_experts
    logits = jnp.where(valid, logits, -jnp.inf)
    ex = jnp.exp(logits - jnp.max(logits, axis=-1, keepdims=True))
    probs = ex / jnp.sum(ex, axis=-1, keepdims=True)
    gates = jnp.zeros_like(probs)
    rest = jnp.where(valid, probs, -1.0)
    tops = []
    for _ in range(TOP_K):
        m = jnp.max(rest, axis=-1, keepdims=True)
        first = jnp.min(jnp.where(rest == m, lane, LANES), axis=-1, keepdims=True)
        sel = lane == first
        tops.append((m, sel))
        rest = jnp.where(sel, -1.0, rest)
    total = tops[0][0]
    for m, _ in tops[1:]:
        total = total + m
    for m, sel in tops:
        gates = jnp.where(sel, m / total, gates)
    hg_ref[:, d:d + LANES] = gates

    first_step = (pl.program_id(0) == 0) & (s == 0)

    @pl.when(first_step)
    def _():
        cnt_ref[...] = jnp.zeros_like(cnt_ref)

    chosen = tops[0][1]
    for _, sel in tops[1:]:
        chosen = chosen | sel
    chosen = jnp.where(chosen, 1.0, 0.0)
    before = cnt_ref[...] + _dot(lt_ref[...], chosen.astype(BF16))
    cnt_ref[...] = cnt_ref[...] + jnp.sum(chosen, axis=0, keepdims=True)
    counts_ref[...] = cnt_ref[...]
    lane_f = lane.astype(F32)
    route = jnp.zeros((ts, LANES), F32)
    for k, (_, sel) in enumerate(tops):
        expert_k = jnp.sum(jnp.where(sel, lane_f, 0.0), axis=-1, keepdims=True)
        order_k = jnp.sum(jnp.where(sel, before, 0.0), axis=-1, keepdims=True)
        route = jnp.where(lane == k, expert_k, route)
        route = jnp.where(lane == TOP_K + k, order_k, route)
    route_ref[...] = route.T[0:SUBLANES, :].astype(jnp.int32)


def _pool_router(x, gain, pool_w, pool_scale, ffn_gain, w_router, *, batch, ts):
    t, d = x.shape
    s_len = t // batch
    n_s = s_len // ts
    n_groups, gd, _ = pool_w.shape
    n_experts = w_router.shape[1]
    assert 3 * n_experts <= LANES
    w_hi = w_router.astype(BF16)
    r1 = w_router - w_hi.astype(F32)
    w_mid = r1.astype(BF16)
    w_lo = (r1 - w_mid.astype(F32)).astype(BF16)
    wr = jnp.zeros((d, LANES), BF16).at[:, :3 * n_experts].set(
        jnp.concatenate([w_hi, w_mid, w_lo], axis=1))
    idx = jnp.arange(ts)
    lower = (idx[None, :] < idx[:, None]).astype(BF16)
    row = lambda b, s: (b * n_s + s, 0)
    const = lambda b, s: (0, 0)
    return pl.pallas_call(
        functools.partial(_pool_router_kernel, n_experts=n_experts),
        grid=(batch, n_s),
        in_specs=[pl.BlockSpec((ts, d), row),
                  pl.BlockSpec((1, d), const),
                  pl.BlockSpec((n_groups, gd, gd), lambda b, s: (0, 0, 0)),
                  pl.BlockSpec((1, d), const),
                  pl.BlockSpec((1, d), const),
                  pl.BlockSpec((d, LANES), const),
                  pl.BlockSpec((ts, ts), const)],
        out_specs=[pl.BlockSpec((ts, d), row),
                   pl.BlockSpec((ts, d + LANES), row),
                   pl.BlockSpec((SUBLANES, ts), lambda b, s: (0, b * n_s + s)),
                   pl.BlockSpec((1, LANES), const)],
        out_shape=[jax.ShapeDtypeStruct((t, d), F32),
                   jax.ShapeDtypeStruct((t, d + LANES), F32),
                   jax.ShapeDtypeStruct((SUBLANES, t), jnp.int32),
                   jax.ShapeDtypeStruct((1, LANES), F32)],
        scratch_shapes=[pltpu.VMEM((POOL_HALO + ts, d), F32),
                        pltpu.VMEM((1, LANES), F32)],
        compiler_params=_params("arbitrary", "arbitrary"),
        name="pool_router",
    )(x, gain.reshape(1, d), pool_w, pool_scale.reshape(1, d), ffn_gain.reshape(1, d), wr, lower)


def _dispatch_kernel(slots_ref, bounds_ref, hg_ref, xs_ref, zero_ref, sem, zero_sem,
                     *, n_experts, tile):
    i = pl.program_id(0)
    td = hg_ref.shape[0]
    n_tokens = pl.num_programs(0) * td

    for r in range(td):
        for k in range(TOP_K):
            slot = slots_ref[k * n_tokens + i * td + r]
            pltpu.make_async_copy(hg_ref.at[pl.ds(r, 1), :],
                                  xs_ref.at[pl.ds(slot, 1), :], sem).start()

    def zero_copies():
        copies = []

        def block(first, size):
            return pltpu.make_async_copy(zero_ref.at[pl.ds(0, size), :],
                                         xs_ref.at[pl.ds(first, size), :], zero_sem)

        for e in range(n_experts):
            first_unused = bounds_ref[e]
            end = bounds_ref[n_experts + e]
            aligned = (first_unused + SUBLANES - 1) // SUBLANES * SUBLANES
            for r in range(SUBLANES - 1):
                copies.append((first_unused + r < aligned, block(first_unused + r, 1)))
            left = end - aligned
            pos = aligned
            size = tile // 2
            while size >= SUBLANES:
                needed = (left & size) != 0
                copies.append((needed, block(pl.multiple_of(pos, SUBLANES), size)))
                pos = pos + jnp.where(needed, size, 0)
                size //= 2
        total = bounds_ref[2 * n_experts]
        for b in range(n_experts):
            first = pl.multiple_of(total + b * tile, tile)
            copies.append((first < xs_ref.shape[0], block(first, tile)))
        return copies

    @pl.when(i == 0)
    def _():
        zero_ref[...] = jnp.zeros_like(zero_ref)
        for action in ("start", "wait"):
            for needed, copy in zero_copies():
                pl.when(needed)(getattr(copy, action))

    for _ in range(TOP_K):
        pltpu.make_async_copy(hg_ref, xs_ref.at[pl.ds(0, td), :], sem).wait()


def _dispatch(slots_flat, bounds, hg, *, td, n_experts, tile):
    t, width = hg.shape
    n_rows = t * TOP_K + n_experts * tile
    return pl.pallas_call(
        functools.partial(_dispatch_kernel, n_experts=n_experts, tile=tile),
        grid_spec=pltpu.PrefetchScalarGridSpec(
            num_scalar_prefetch=2,
            grid=(t // td,),
            in_specs=[pl.BlockSpec((td, width), lambda i, slots, bounds: (i, 0))],
            out_specs=pl.BlockSpec(memory_space=pl.ANY),
            scratch_shapes=[pltpu.VMEM((tile, width), F32),
                            pltpu.SemaphoreType.DMA,
                            pltpu.SemaphoreType.DMA]),
        out_shape=jax.ShapeDtypeStruct((n_rows, width), F32),
        compiler_params=_params("arbitrary"),
        name="moe_dispatch",
    )(slots_flat, bounds, hg)


def _experts_kernel(te_ref, nt_ref, xs_ref, wg_ref, wu_ref, wd_ref, ys_ref):
    i = pl.program_id(0)
    d = wg_ref.shape[0]

    @pl.when(i < nt_ref[0])
    def _():
        rows = xs_ref[:, 0:d].astype(BF16)
        gates = xs_ref[:, d:d + LANES]
        lane = lax.broadcasted_iota(jnp.int32, gates.shape, 1)
        ge = jnp.sum(jnp.where(lane == te_ref[i], gates, 0.0), axis=-1, keepdims=True)
        gate = _dot(rows, wg_ref[...])
        up = _dot(rows, wu_ref[...])
        act = (ge * (gate * _sigmoid(gate) * up)).astype(BF16)
        ys_ref[...] = _dot(act, wd_ref[...])

    @pl.when(i >= nt_ref[0])
    def _():
        ys_ref[...] = jnp.zeros_like(ys_ref)


def _experts(tile_expert, n_tiles, xs, wg, wu, wd, cast_weights, *, tile):
    n_rows, width = xs.shape
    n_experts, d, ffe = wg.shape
    row_map = lambda i, te, nt: (i, 0)
    w_map = lambda i, te, nt: (te[i], 0, 0)
    grid = (n_rows // tile,)
    casts = _Casts(cast_weights, grid)
    out, *cast = pl.pallas_call(
        casts.wrap(_experts_kernel, n_in=4, n_out=1, n_prefetch=2),
        grid_spec=pltpu.PrefetchScalarGridSpec(
            num_scalar_prefetch=2,
            grid=grid,
            in_specs=[pl.BlockSpec((tile, width), row_map),
                      pl.BlockSpec((None, d, ffe), w_map),
                      pl.BlockSpec((None, d, ffe), w_map),
                      pl.BlockSpec((None, ffe, d), w_map),
                      *casts.specs],
            out_specs=[pl.BlockSpec((tile, d), row_map), *casts.specs]),
        out_shape=[jax.ShapeDtypeStruct((n_rows, d), F32), *casts.out_shapes],
        compiler_params=_params("arbitrary"),
        name="moe_experts",
    )(tile_expert, n_tiles, xs, wg, wu, wd, *casts.views)
    return out, casts.restore(cast)


def _routing_tables(route, counts, *, n_experts, tile):
    t = route.shape[1]
    counts = counts[0, :n_experts].astype(jnp.int32)
    padded = (counts + tile - 1) // tile * tile
    ends = jnp.cumsum(padded)
    starts = ends - padded
    expert, order = route[:TOP_K], route[TOP_K:2 * TOP_K]
    slots = (starts[expert] + order).reshape(-1)
    bounds = jnp.concatenate([starts + counts, ends, ends[-1:]]).astype(jnp.int32)
    n_tiles_max = (t * TOP_K) // tile + n_experts
    first_row = jnp.arange(n_tiles_max, dtype=jnp.int32) * tile
    tile_expert = jnp.minimum(jnp.sum(first_row[:, None] >= ends[None, :], axis=1),
                              n_experts - 1).astype(jnp.int32)
    n_tiles = (ends[-1:] // tile).astype(jnp.int32)
    return slots.astype(jnp.int32), bounds, tile_expert, n_tiles


def _ple_combine_kernel(slots_ref, x_ref, g_ref, wg_ref, p_ref, wp_ref, fg_ref, ys_ref,
                        o_ref, ybuf, sem, *, final_norm):
    i = pl.program_id(0)
    n = pl.num_programs(0)
    tm = x_ref.shape[0] // 2
    d = x_ref.shape[1]
    n_tokens = n * 2 * tm
    chunk_cols = d // COMBINE_CHUNKS
    chunk_rows = tm // COMBINE_CHUNKS

    def start_rows(tile, buf, first, last):
        for r in range(first, last):
            for k in range(TOP_K):
                slot = slots_ref[k * n_tokens + tile * tm + r]
                pltpu.make_async_copy(ys_ref.at[pl.ds(slot, 1), :],
                                      ybuf.at[buf, k, pl.ds(r, 1), :], sem.at[buf]).start()

    def wait_tile(buf):
        for k in range(TOP_K):
            pltpu.make_async_copy(ys_ref.at[pl.ds(0, tm), :], ybuf.at[buf, k],
                                  sem.at[buf]).wait()

    @pl.when(i == 0)
    def _():
        start_rows(0, 0, 0, tm)

    last_tile = 2 * n - 1
    for buf in range(2):
        rows = slice(buf * tm, (buf + 1) * tm)
        wait_tile(buf)
        x = x_ref[rows, :]
        for k in range(TOP_K):
            x = x + ybuf[buf, k]
        h = _rms_norm(x, g_ref[...]).astype(BF16)
        pb = p_ref[rows, :].astype(BF16)
        nxt = jnp.minimum(2 * i + buf + 1, last_tile)
        for c in range(COMBINE_CHUNKS):
            cols = slice(c * chunk_cols, (c + 1) * chunk_cols)
            gate = _sigmoid(_dot(h, wg_ref[:, cols]))
            o_ref[rows, cols] = x[:, cols] + gate * _dot(pb, wp_ref[:, cols])
            start_rows(nxt, 1 - buf, c * chunk_rows, (c + 1) * chunk_rows)
        if final_norm:
            o_ref[rows, :] = _rms_norm(o_ref[rows, :], fg_ref[...])

    @pl.when(i == n - 1)
    def _():
        wait_tile(0)


def _ple_combine(slots_flat, x, gain, wg, p, layer, wp, final_gain, ys, *, tm, final_norm):
    t, d = x.shape
    dp = p.shape[-1]
    row = lambda i, slots: (i, 0)
    const = lambda i, slots: (0, 0)
    return pl.pallas_call(
        functools.partial(_ple_combine_kernel, final_norm=final_norm),
        grid_spec=pltpu.PrefetchScalarGridSpec(
            num_scalar_prefetch=1,
            grid=(t // (2 * tm),),
            in_specs=[pl.BlockSpec((2 * tm, d), row),
                      pl.BlockSpec((1, d), const),
                      pl.BlockSpec((d, d), const),
                      pl.BlockSpec((None, 2 * tm, dp), lambda i, slots: (layer, i, 0)),
                      pl.BlockSpec((dp, d), const),
                      pl.BlockSpec((1, d), const),
                      pl.BlockSpec(memory_space=pl.ANY)],
            out_specs=pl.BlockSpec((2 * tm, d), row),
            scratch_shapes=[pltpu.VMEM((2, TOP_K, tm, d), F32),
                            pltpu.SemaphoreType.DMA((2,))]),
        out_shape=jax.ShapeDtypeStruct((t, d), F32),
        compiler_params=_params("arbitrary"),
        name="ple_combine",
    )(slots_flat, x, gain.reshape(1, d), wg, p, wp, final_gain.reshape(1, d), ys)


def kernel(x, p, final_norm_gain, lb_table, mix_norm_even, w_in_even, hgrn_norm_gain, conv_w, w_out_even, ffn_norm_even, w_gate_dense, w_up_dense, w_down_dense, mix_norm_odd, pool_w, pool_scale, ffn_norm_odd, w_router, w_gate_exp, w_up_exp, w_down_exp, ple_norm, ple_gate_w, ple_proj):
    batch, s_len, d = x.shape
    t = batch * s_len
    depth = p.shape[0]
    assert depth == 2, "the weight-cast schedule below is written for one layer pair"
    n_experts = w_router.shape[-1]
    tile = EXPERT_ROW_TILE
    xs = x.reshape(t, d)

    pp = p.reshape(depth, t, -1)

    u, _ = _norm_matmul(xs, mix_norm_even[0], w_in_even[0].astype(BF16), [], tm=1024, tn=1792)
    a, b, (w_out, w_gate, w_up, w_down) = _hgrn_conv(
        u, lb_table, hgrn_norm_gain[0], conv_w[0],
        [w_out_even[0], w_gate_dense[0], w_up_dense[0], w_down_dense[0]],
        batch=batch, rows=512, layer=0)
    xs, (ple_gate0, ple_proj0, pool_wb) = _out_proj(
        xs, a.reshape(t, -1), b.reshape(t, -1), w_out,
        [ple_gate_w[0], ple_proj[0], pool_w[0]], tm=512, tn=d)
    xs, (wg_exp, wu_exp, wd_exp) = _swiglu(
        xs, ffn_norm_even[0], w_gate, w_up, w_down,
        [w_gate_exp[0], w_up_exp[0], w_down_exp[0]], tm=512, tf=512)
    xs = _ple(xs, ple_norm[0], ple_gate0, pp, 0, ple_proj0, final_norm_gain,
              tm=512, final_norm=False)

    xs, hg, route, counts = _pool_router(
        xs, mix_norm_odd[0], pool_wb, pool_scale[0], ffn_norm_odd[0], w_router[0],
        batch=batch, ts=512)
    slots, bounds, tile_expert, n_tiles = _routing_tables(
        route, counts, n_experts=n_experts, tile=tile)
    rows = _dispatch(slots, bounds, hg, td=512, n_experts=n_experts, tile=tile)
    ys, (ple_gate1, ple_proj1) = _experts(
        tile_expert, n_tiles, rows, wg_exp, wu_exp, wd_exp, [ple_gate_w[1], ple_proj[1]],
        tile=tile)
    xs = _ple_combine(slots, xs, ple_norm[1], ple_gate1, pp, 1, ple_proj1,
                      final_norm_gain, ys, tm=256, final_norm=True)
    return xs.reshape(batch, s_len, d)
```

```python
import functools
import math

import jax
import jax.numpy as jnp
from jax import lax
from jax.experimental import pallas as pl
from jax.experimental.pallas import tpu as pltpu

F32 = jnp.float32
BF16 = jnp.bfloat16

EPS = 1e-6
HGRN_HEADS = 8
HGRN_CHUNK = 64
CONV_WIDTH = 3
POOL_WINDOWS = (2, 4, 8, 16)
POOL_HALO = 16
TOP_K = 2
EXPERT_ROW_TILE = 512
COMBINE_CHUNKS = 8
LANES = 128
SUBLANES = 8
BF16_SUBLANES = 16
VMEM_LIMIT_BYTES = 56 * 1024 * 1024


def _params(*semantics):
    return pltpu.CompilerParams(dimension_semantics=semantics,
                                vmem_limit_bytes=VMEM_LIMIT_BYTES)


def _rms_norm(x, gain):
    ms = jnp.mean(x * x, axis=-1, keepdims=True)
    return x * lax.rsqrt(ms + EPS) * gain


def _sigmoid(x):
    return 1.0 / (1.0 + jnp.exp(-x))


def _dot(a, b):
    return jnp.dot(a, b, preferred_element_type=F32)


def _dot_nt(a, b):
    return lax.dot_general(a, b, (((1,), (1,)), ((), ())), preferred_element_type=F32)


class _Casts:
    def __init__(self, weights, grid):
        n_steps = math.prod(grid)
        self.shapes = [stacked.shape[1:] for stacked, _ in weights]
        self.views, self.in_specs, self.out_specs, self.out_shapes = [], [], [], []
        for stacked, layer in weights:
            cols = stacked.shape[-1]
            rows = stacked[0].size // cols
            block_rows = next(br for br in range(BF16_SUBLANES, rows + 1, BF16_SUBLANES)
                              if rows % br == 0 and rows // br <= n_steps)
            n_blocks = rows // block_rows
            self.views.append(stacked.reshape(stacked.shape[0] * rows, cols))
            for specs, first in ((self.in_specs, layer * n_blocks), (self.out_specs, 0)):
                specs.append(pl.BlockSpec(
                    (block_rows, cols),
                    functools.partial(self._index_map, grid, n_blocks, first)))
            self.out_shapes.append(jax.ShapeDtypeStruct((rows, cols), BF16))

    @staticmethod
    def _index_map(grid, n_blocks, first, *args):
        step = 0
        for size, idx in zip(grid, args):
            step = step * size + idx
        return first + jnp.minimum(step, n_blocks - 1), 0

    def __len__(self):
        return len(self.views)

    def wrap(self, body, n_in, n_out, n_prefetch=0):
        n = len(self)

        def kernel(*refs):
            ins_end = n_prefetch + n_in
            outs_start = ins_end + n
            outs_end = outs_start + n_out
            body(*refs[:ins_end], *refs[outs_start:outs_end], *refs[outs_end + n:])
            for src, dst in zip(refs[ins_end:outs_start], refs[outs_end:outs_end + n]):
                dst[...] = src[...].astype(BF16)

        return kernel

    def restore(self, outs):
        return [o.reshape(shape) for o, shape in zip(outs, self.shapes)]


def _norm_matmul_kernel(x_ref, g_ref, w_ref, o_ref, h_ref):
    @pl.when(pl.program_id(1) == 0)
    def _():
        h_ref[...] = _rms_norm(x_ref[...], g_ref[...]).astype(BF16)

    o_ref[...] = _dot(h_ref[...], w_ref[...])


def _norm_matmul(x, gain, w, cast_weights, *, tm, tn):
    t, d = x.shape
    n = w.shape[1]
    grid = (t // tm, n // tn)
    casts = _Casts(cast_weights, grid)
    out, *cast = pl.pallas_call(
        casts.wrap(_norm_matmul_kernel, n_in=3, n_out=1),
        grid=grid,
        in_specs=[pl.BlockSpec((tm, d), lambda i, j: (i, 0)),
                  pl.BlockSpec((1, d), lambda i, j: (0, 0)),
                  pl.BlockSpec((d, tn), lambda i, j: (0, j)),
                  *casts.in_specs],
        out_specs=[pl.BlockSpec((tm, tn), lambda i, j: (i, j)), *casts.out_specs],
        out_shape=[jax.ShapeDtypeStruct((t, n), F32), *casts.out_shapes],
        scratch_shapes=[pltpu.VMEM((tm, d), BF16)],
        compiler_params=_params("parallel", "arbitrary"),
        name="norm_matmul",
    )(x, gain.reshape(1, d), w, *casts.views)
    return out, casts.restore(cast)


def _hgrn_conv_kernel(tri_ref, q_ref, f_ref, i_ref, g_ref, gb_ref, gc_ref, vc_ref,
                      lbt_ref, gain_ref, cw_ref, a_ref, b_ref, *, rows, layer):
    s_len, dh = q_ref.shape
    ch = HGRN_CHUNK
    ct = tri_ref.shape[0]

    lbt = lbt_ref[...]
    e = jnp.exp(lbt - jnp.max(lbt, axis=0, keepdims=True))
    lb = (jnp.sum(e[0:layer + 1, :], axis=0, keepdims=True)
          / jnp.sum(e, axis=0, keepdims=True))
    gain = gain_ref[...]

    tri = tri_ref[...]
    row = lax.broadcasted_iota(jnp.int32, (ct, ct), 0)
    col = lax.broadcasted_iota(jnp.int32, (ct, ct), 1)
    causal = (row >= col) & (row // ch == col // ch)
    n_chunks = rows // ch

    def body(t, st):
        r0 = pl.multiple_of(t * rows, rows)
        qr = q_ref[pl.ds(r0, rows), :]
        q = qr * _sigmoid(qr)
        f = lb + (1.0 - lb) * _sigmoid(f_ref[pl.ds(r0, rows), :])
        k = 1.0 - f
        v = i_ref[pl.ds(r0, rows), :].astype(BF16)
        lf = jnp.log(f)
        hi = lf.astype(BF16)
        r1 = lf - hi.astype(F32)
        mid = r1.astype(BF16)
        lo = (r1 - mid.astype(F32)).astype(BF16)
        parts = jnp.concatenate([hi, mid, lo], axis=1)
        cums = []
        for j in range(rows // ct):
            c3 = _dot(tri, parts[j * ct:(j + 1) * ct])
            cums.append(c3[:, :dh] + c3[:, dh:2 * dh] + c3[:, 2 * dh:])
        cum = jnp.concatenate(cums, axis=0)

        cum3 = cum.reshape(n_chunks, ch, dh)
        ref = cum3[:, ch // 2:ch // 2 + 1, :]
        last = cum3[:, ch - 1:ch, :]
        q3 = q.reshape(n_chunks, ch, dh)
        k3 = k.reshape(n_chunks, ch, dh)
        qe = (q3 * jnp.exp(cum3 - ref)).astype(BF16).reshape(rows, dh)
        ke = (k3 * jnp.exp(ref - cum3)).astype(BF16).reshape(rows, dh)
        qc = (q3 * jnp.exp(cum3)).astype(BF16).reshape(rows, dh)
        kd = (k3 * jnp.exp(last - cum3)).astype(BF16).reshape(rows, dh)
        decay = jnp.exp(last)

        chunks = [slice(c * ch, (c + 1) * ch) for c in range(n_chunks)]
        kv_t = [lax.dot_general(v[cs], kd[cs], (((0,), (0,)), ((), ())),
                                preferred_element_type=F32) for cs in chunks]
        states = []
        for c in range(n_chunks):
            states.append(st.astype(BF16))
            st = decay[c] * st + kv_t[c]
        o_inter = jnp.concatenate(
            [_dot_nt(qc[cs], s_c) for cs, s_c in zip(chunks, states)], axis=0)

        o_intra = []
        for j in range(rows // ct):
            sl = slice(j * ct, (j + 1) * ct)
            scores = jnp.where(causal, _dot_nt(qe[sl], ke[sl]), 0.0)
            o_intra.append(_dot(scores.astype(BF16), v[sl]))
        o = jnp.concatenate(o_intra, axis=0) + o_inter
        o = o * lax.rsqrt(jnp.mean(o * o, axis=-1, keepdims=True) + EPS)
        gr = g_ref[pl.ds(r0, rows), :]
        a_ref[pl.ds(r0, rows), :] = (o * gain * (gr * _sigmoid(gr))).astype(BF16)
        return st

    lax.fori_loop(0, s_len // rows, body, jnp.zeros((dh, dh), F32))

    tt = gc_ref[...] * vc_ref[...]
    ridx = lax.broadcasted_iota(jnp.int32, tt.shape, 0)
    cw = cw_ref[...]
    conv = tt * cw[CONV_WIDTH - 1:CONV_WIDTH, :]
    for back in range(1, CONV_WIDTH):
        shifted = jnp.where(ridx >= back, pltpu.roll(tt, back, 0), 0.0)
        conv = conv + shifted * cw[CONV_WIDTH - 1 - back:CONV_WIDTH - back, :]
    b_ref[...] = (gb_ref[...] * conv).astype(BF16)


def _hgrn_conv(u, lb_table, hgrn_gain, conv_w, cast_weights, *, batch, rows, layer):
    t, n_in = u.shape
    s_len = t // batch
    d_hgrn = hgrn_gain.shape[0]
    dh = d_hgrn // HGRN_HEADS
    d_conv = conv_w.shape[1]
    nh = HGRN_HEADS
    assert d_conv // dh == nh and n_in == 4 * d_hgrn + 3 * d_conv
    u3 = u.reshape(batch, s_len, n_in)

    idx = jnp.arange(2 * HGRN_CHUNK)
    tri = ((idx[:, None] >= idx[None, :])
           & (idx[:, None] // HGRN_CHUNK == idx[None, :] // HGRN_CHUNK)).astype(BF16)

    def col(off):
        return pl.BlockSpec((None, s_len, dh), lambda b, h, off=off: (b, 0, off + h))

    n_lb = lb_table.shape[0]
    out_spec = pl.BlockSpec((None, s_len, dh), lambda b, h: (b, 0, h))
    grid = (batch, nh)
    casts = _Casts(cast_weights, grid)
    a, b, *cast = pl.pallas_call(
        casts.wrap(functools.partial(_hgrn_conv_kernel, rows=rows, layer=layer),
                   n_in=11, n_out=2),
        grid=grid,
        in_specs=[pl.BlockSpec(tri.shape, lambda b, h: (0, 0)),
                  col(0), col(nh), col(2 * nh), col(3 * nh),
                  col(4 * nh), col(5 * nh), col(6 * nh),
                  pl.BlockSpec((n_lb, dh), lambda b, h: (0, h)),
                  pl.BlockSpec((1, dh), lambda b, h: (0, h)),
                  pl.BlockSpec((CONV_WIDTH, dh), lambda b, h: (0, h)),
                  *casts.in_specs],
        out_specs=[out_spec, out_spec, *casts.out_specs],
        out_shape=[jax.ShapeDtypeStruct((batch, s_len, d_hgrn), BF16),
                   jax.ShapeDtypeStruct((batch, s_len, d_conv), BF16),
                   *casts.out_shapes],
        compiler_params=_params("parallel", "parallel"),
        name="hgrn_conv",
    )(tri, u3, u3, u3, u3, u3, u3, u3, lb_table, hgrn_gain.reshape(1, d_hgrn), conv_w,
      *casts.views)
    return a, b, casts.restore(cast)


def _out_proj_kernel(x_ref, a_ref, b_ref, wa_ref, wb_ref, o_ref):
    o_ref[...] = x_ref[...] + _dot(a_ref[...], wa_ref[...]) + _dot(b_ref[...], wb_ref[...])


def _out_proj(x, a, b, w, cast_weights, *, tm, tn):
    t, d = x.shape
    ka, kb = a.shape[1], b.shape[1]
    assert ka == kb
    grid = (t // tm, d // tn)
    casts = _Casts(cast_weights, grid)
    out, *cast = pl.pallas_call(
        casts.wrap(_out_proj_kernel, n_in=5, n_out=1),
        grid=grid,
        in_specs=[pl.BlockSpec((tm, tn), lambda i, j: (i, j)),
                  pl.BlockSpec((tm, ka), lambda i, j: (i, 0)),
                  pl.BlockSpec((tm, kb), lambda i, j: (i, 0)),
                  pl.BlockSpec((ka, tn), lambda i, j: (0, j)),
                  pl.BlockSpec((kb, tn), lambda i, j: (1, j)),
                  *casts.in_specs],
        out_specs=[pl.BlockSpec((tm, tn), lambda i, j: (i, j)), *casts.out_specs],
        out_shape=[jax.ShapeDtypeStruct((t, d), F32), *casts.out_shapes],
        compiler_params=_params("parallel", "arbitrary"),
        name="out_proj",
    )(x, a, b, w, w, *casts.views)
    return out, casts.restore(cast)


def _swiglu_kernel(x_ref, g_ref, wg_ref, wu_ref, wd_ref, o_ref, h_ref, acc_ref):
    f = pl.program_id(1)

    @pl.when(f == 0)
    def _():
        h_ref[...] = _rms_norm(x_ref[...], g_ref[...]).astype(BF16)
        acc_ref[...] = jnp.zeros_like(acc_ref)

    h = h_ref[...]
    gate = _dot(h, wg_ref[...])
    up = _dot(h, wu_ref[...])
    act = (gate * _sigmoid(gate) * up).astype(BF16)
    acc_ref[...] += _dot(act, wd_ref[...])

    @pl.when(f == pl.num_programs(1) - 1)
    def _():
        o_ref[...] = x_ref[...] + acc_ref[...]


def _swiglu(x, gain, wg, wu, wd, cast_weights, *, tm, tf):
    t, d = x.shape
    ff = wg.shape[1]
    grid = (t // tm, ff // tf)
    casts = _Casts(cast_weights, grid)
    out, *cast = pl.pallas_call(
        casts.wrap(_swiglu_kernel, n_in=5, n_out=1),
        grid=grid,
        in_specs=[pl.BlockSpec((tm, d), lambda i, f: (i, 0)),
                  pl.BlockSpec((1, d), lambda i, f: (0, 0)),
                  pl.BlockSpec((d, tf), lambda i, f: (0, f)),
                  pl.BlockSpec((d, tf), lambda i, f: (0, f)),
                  pl.BlockSpec((tf, d), lambda i, f: (f, 0)),
                  *casts.in_specs],
        out_specs=[pl.BlockSpec((tm, d), lambda i, f: (i, 0)), *casts.out_specs],
        out_shape=[jax.ShapeDtypeStruct((t, d), F32), *casts.out_shapes],
        scratch_shapes=[pltpu.VMEM((tm, d), BF16), pltpu.VMEM((tm, d), F32)],
        compiler_params=_params("parallel", "arbitrary"),
        name="swiglu",
    )(x, gain.reshape(1, d), wg, wu, wd, *casts.views)
    return out, casts.restore(cast)


def _ple_math(x, g_ref, wg_ref, p_ref, wp_ref, fg_ref, final_norm):
    h = _rms_norm(x, g_ref[...]).astype(BF16)
    gate = _sigmoid(_dot(h, wg_ref[...]))
    y = x + gate * _dot(p_ref[...].astype(BF16), wp_ref[...])
    if final_norm:
        y = _rms_norm(y, fg_ref[...])
    return y


def _ple_kernel(x_ref, g_ref, wg_ref, p_ref, wp_ref, fg_ref, o_ref, *, final_norm):
    o_ref[...] = _ple_math(x_ref[...], g_ref, wg_ref, p_ref, wp_ref, fg_ref, final_norm)


def _ple(x, gain, wg, p, layer, wp, final_gain, *, tm, final_norm):
    t, d = x.shape
    dp = p.shape[-1]
    return pl.pallas_call(
        functools.partial(_ple_kernel, final_norm=final_norm),
        grid=(t // tm,),
        in_specs=[pl.BlockSpec((tm, d), lambda i: (i, 0)),
                  pl.BlockSpec((1, d), lambda i: (0, 0)),
                  pl.BlockSpec((d, d), lambda i: (0, 0)),
                  pl.BlockSpec((None, tm, dp), lambda i: (layer, i, 0)),
                  pl.BlockSpec((dp, d), lambda i: (0, 0)),
                  pl.BlockSpec((1, d), lambda i: (0, 0))],
        out_specs=pl.BlockSpec((tm, d), lambda i: (i, 0)),
        out_shape=jax.ShapeDtypeStruct((t, d), F32),
        compiler_params=_params("parallel"),
        name="ple",
    )(x, gain.reshape(1, d), wg, p, wp, final_gain.reshape(1, d))


def _pool_router_kernel(x_ref, g_ref, pw_ref, ps_ref, fg_ref, wr_ref, lt_ref,
                        x1_ref, hg_ref, route_ref, counts_ref, ext_ref, cnt_ref,
                        *, n_experts):
    s = pl.program_id(1)
    ts, d = x_ref.shape
    n_groups = len(POOL_WINDOWS)
    gd = d // n_groups

    @pl.when(s == 0)
    def _():
        ext_ref[0:POOL_HALO, :] = jnp.zeros((POOL_HALO, d), F32)

    @pl.when(s > 0)
    def _():
        ext_ref[0:POOL_HALO, :] = ext_ref[ts:ts + POOL_HALO, :]

    x = x_ref[...]
    h = _rms_norm(x, g_ref[...])
    ext_ref[POOL_HALO:, :] = h

    def window_sum(cols, w):
        acc = ext_ref[:, cols]
        span = 1
        while span < w:
            acc = acc + pltpu.roll(acc, span, 0)
            span *= 2
        return acc[POOL_HALO:, :]

    pos = (s * ts + 1 + lax.broadcasted_iota(jnp.int32, (ts, 1), 0)).astype(F32)
    ys = []
    for gi, w in enumerate(POOL_WINDOWS):
        cols = slice(gi * gd, (gi + 1) * gd)
        mean = window_sum(cols, w) / jnp.minimum(pos, float(w))
        diff = (mean - h[:, cols]).astype(BF16)
        ys.append(_dot(diff, pw_ref[gi]))
    y = jnp.concatenate(ys, axis=1)
    x1 = x + y * ps_ref[...]
    x1_ref[...] = x1

    h2 = _rms_norm(x1, fg_ref[...])
    hg_ref[:, 0:d] = h2

    hi = h2.astype(BF16)
    mid = (h2 - hi.astype(F32)).astype(BF16)
    p_hi = _dot(hi, wr_ref[...])
    p_mid = _dot(mid, wr_ref[...])
    logits = (p_hi + pltpu.roll(p_hi, LANES - n_experts, 1)
              + pltpu.roll(p_hi, LANES - 2 * n_experts, 1)
              + p_mid + pltpu.roll(p_mid, LANES - n_experts, 1))
    lane = lax.broadcasted_iota(jnp.int32, logits.shape, 1)
    valid = lane < n_experts
    logits = jnp.where(valid, logits, -jnp.inf)
    ex = jnp.exp(logits - jnp.max(logits, axis=-1, keepdims=True))
    probs = ex / jnp.sum(ex, axis=-1, keepdims=True)
    gates = jnp.zeros_like(probs)
    rest = jnp.where(valid, probs, -1.0)
    tops = []
    for _ in range(TOP_K):
        m = jnp.max(rest, axis=-1, keepdims=True)
        first = jnp.min(jnp.where(rest == m, lane, LANES), axis=-1, keepdims=True)
        sel = lane == first
        tops.append((m, sel))
        rest = jnp.where(sel, -1.0, rest)
    total = tops[0][0]
    for m, _ in tops[1:]:
        total = total + m
    for m, sel in tops:
        gates = jnp.where(sel, m / total, gates)
    hg_ref[:, d:d + LANES] = gates

    first_step = (pl.program_id(0) == 0) & (s == 0)

    @pl.when(first_step)
    def _():
        cnt_ref[...] = jnp.zeros_like(cnt_ref)

    chosen = tops[0][1]
    for _, sel in tops[1:]:
        chosen = chosen | sel
    chosen = jnp.where(chosen, 1.0, 0.0)
    before = cnt_ref[...] + _dot(lt_ref[...], chosen.astype(BF16))
    cnt_ref[...] = cnt_ref[...] + jnp.sum(chosen, axis=0, keepdims=True)
    counts_ref[...] = cnt_ref[...]
    lane_f = lane.astype(F32)
    route = jnp.zeros((ts, LANES), F32)
    for k, (_, sel) in enumerate(tops):
        expert_k = jnp.sum(jnp.where(sel, lane_f, 0.0), axis=-1, keepdims=True)
        order_k = jnp.sum(jnp.where(sel, before, 0.0), axis=-1, keepdims=True)
        route = jnp.where(lane == k, expert_k, route)
        route = jnp.where(lane == TOP_K + k, order_k, route)
    route_ref[...] = route.T[0:SUBLANES, :].astype(jnp.int32)


def _pool_router(x, gain, pool_w, pool_scale, ffn_gain, w_router, *, batch, ts):
    t, d = x.shape
    s_len = t // batch
    n_s = s_len // ts
    n_groups, gd, _ = pool_w.shape
    n_experts = w_router.shape[1]
    assert 3 * n_experts <= LANES
    w_hi = w_router.astype(BF16)
    r1 = w_router - w_hi.astype(F32)
    w_mid = r1.astype(BF16)
    w_lo = (r1 - w_mid.astype(F32)).astype(BF16)
    wr = jnp.zeros((d, LANES), BF16).at[:, :3 * n_experts].set(
        jnp.concatenate([w_hi, w_mid, w_lo], axis=1))
    idx = jnp.arange(ts)
    lower = (idx[None, :] < idx[:, None]).astype(BF16)
    row = lambda b, s: (b * n_s + s, 0)
    const = lambda b, s: (0, 0)
    return pl.pallas_call(
        functools.partial(_pool_router_kernel, n_experts=n_experts),
        grid=(batch, n_s),
        in_specs=[pl.BlockSpec((ts, d), row),
                  pl.BlockSpec((1, d), const),
                  pl.BlockSpec((n_groups, gd, gd), lambda b, s: (0, 0, 0)),
                  pl.BlockSpec((1, d), const),
                  pl.BlockSpec((1, d), const),
                  pl.BlockSpec((d, LANES), const),
                  pl.BlockSpec((ts, ts), const)],
        out_specs=[pl.BlockSpec((ts, d), row),
                   pl.BlockSpec((ts, d + LANES), row),
                   pl.BlockSpec((SUBLANES, ts), lambda b, s: (0, b * n_s + s)),
                   pl.BlockSpec((1, LANES), const)],
        out_shape=[jax.ShapeDtypeStruct((t, d), F32),
                   jax.ShapeDtypeStruct((t, d + LANES), F32),
                   jax.ShapeDtypeStruct((SUBLANES, t), jnp.int32),
                   jax.ShapeDtypeStruct((1, LANES), F32)],
        scratch_shapes=[pltpu.VMEM((POOL_HALO + ts, d), F32),
                        pltpu.VMEM((1, LANES), F32)],
        compiler_params=_params("arbitrary", "arbitrary"),
        name="pool_router",
    )(x, gain.reshape(1, d), pool_w, pool_scale.reshape(1, d), ffn_gain.reshape(1, d), wr, lower)


def _dispatch_kernel(slots_ref, bounds_ref, hg_ref, xs_ref, zero_ref, sem, zero_sem,
                     *, n_experts, tile):
    i = pl.program_id(0)
    td = hg_ref.shape[0]
    n_tokens = pl.num_programs(0) * td

    for r in range(td):
        for k in range(TOP_K):
            slot = slots_ref[k * n_tokens + i * td + r]
            pltpu.make_async_copy(hg_ref.at[pl.ds(r, 1), :],
                                  xs_ref.at[pl.ds(slot, 1), :], sem).start()

    def zero_copies():
        copies = []

        def block(first, size):
            return pltpu.make_async_copy(zero_ref.at[pl.ds(0, size), :],
                                         xs_ref.at[pl.ds(first, size), :], zero_sem)

        for e in range(n_experts):
            first_unused = bounds_ref[e]
            end = bounds_ref[n_experts + e]
            aligned = (first_unused + SUBLANES - 1) // SUBLANES * SUBLANES
            for r in range(SUBLANES - 1):
                copies.append((first_unused + r < aligned, block(first_unused + r, 1)))
            left = end - aligned
            pos = aligned
            size = tile // 2
            while size >= SUBLANES:
                needed = (left & size) != 0
                copies.append((needed, block(pl.multiple_of(pos, SUBLANES), size)))
                pos = pos + jnp.where(needed, size, 0)
                size //= 2
        total = bounds_ref[2 * n_experts]
        for b in range(n_experts):
            first = pl.multiple_of(total + b * tile, tile)
            copies.append((first < xs_ref.shape[0], block(first, tile)))
        return copies

    @pl.when(i == 0)
    def _():
        zero_ref[...] = jnp.zeros_like(zero_ref)
        for action in ("start", "wait"):
            for needed, copy in zero_copies():
                pl.when(needed)(getattr(copy, action))

    for _ in range(TOP_K):
        pltpu.make_async_copy(hg_ref, xs_ref.at[pl.ds(0, td), :], sem).wait()


def _dispatch(slots_flat, bounds, hg, *, td, n_experts, tile):
    t, width = hg.shape
    n_rows = t * TOP_K + n_experts * tile
    return pl.pallas_call(
        functools.partial(_dispatch_kernel, n_experts=n_experts, tile=tile),
        grid_spec=pltpu.PrefetchScalarGridSpec(
            num_scalar_prefetch=2,
            grid=(t // td,),
            in_specs=[pl.BlockSpec((td, width), lambda i, slots, bounds: (i, 0))],
            out_specs=pl.BlockSpec(memory_space=pl.ANY),
            scratch_shapes=[pltpu.VMEM((tile, width), F32),
                            pltpu.SemaphoreType.DMA,
                            pltpu.SemaphoreType.DMA]),
        out_shape=jax.ShapeDtypeStruct((n_rows, width), F32),
        compiler_params=_params("arbitrary"),
        name="moe_dispatch",
    )(slots_flat, bounds, hg)


def _experts_kernel(te_ref, nt_ref, xs_ref, wg_ref, wu_ref, wd_ref, ys_ref):
    i = pl.program_id(0)
    d = wg_ref.shape[0]

    @pl.when(i < nt_ref[0])
    def _():
        rows = xs_ref[:, 0:d].astype(BF16)
        gates = xs_ref[:, d:d + LANES]
        lane = lax.broadcasted_iota(jnp.int32, gates.shape, 1)
        ge = jnp.sum(jnp.where(lane == te_ref[i], gates, 0.0), axis=-1, keepdims=True)
        gate = _dot(rows, wg_ref[...])
        up = _dot(rows, wu_ref[...])
        act = (ge * (gate * _sigmoid(gate) * up)).astype(BF16)
        ys_ref[...] = _dot(act, wd_ref[...])

    @pl.when(i >= nt_ref[0])
    def _():
        ys_ref[...] = jnp.zeros_like(ys_ref)


def _experts(tile_expert, n_tiles, xs, wg, wu, wd, cast_weights, *, tile):
    n_rows, width = xs.shape
    n_experts, d, ffe = wg.shape
    row_map = lambda i, te, nt: (i, 0)
    w_map = lambda i, te, nt: (te[i], 0, 0)
    grid = (n_rows // tile,)
    casts = _Casts(cast_weights, grid)
    out, *cast = pl.pallas_call(
        casts.wrap(_experts_kernel, n_in=4, n_out=1, n_prefetch=2),
        grid_spec=pltpu.PrefetchScalarGridSpec(
            num_scalar_prefetch=2,
            grid=grid,
            in_specs=[pl.BlockSpec((tile, width), row_map),
                      pl.BlockSpec((None, d, ffe), w_map),
                      pl.BlockSpec((None, d, ffe), w_map),
                      pl.BlockSpec((None, ffe, d), w_map),
                      *casts.in_specs],
            out_specs=[pl.BlockSpec((tile, d), row_map), *casts.out_specs]),
        out_shape=[jax.ShapeDtypeStruct((n_rows, d), F32), *casts.out_shapes],
        compiler_params=_params("arbitrary"),
        name="moe_experts",
    )(tile_expert, n_tiles, xs, wg, wu, wd, *casts.views)
    return out, casts.restore(cast)


def _routing_tables(route, counts, *, n_experts, tile):
    t = route.shape[1]
    counts = counts[0, :n_experts].astype(jnp.int32)
    padded = (counts + tile - 1) // tile * tile
    ends = jnp.cumsum(padded)
    starts = ends - padded
    expert, order = route[:TOP_K], route[TOP_K:2 * TOP_K]
    slots = order
    for e in range(n_experts):
        slots = slots + jnp.where(expert == e, starts[e], 0)
    slots = slots.reshape(-1)
    bounds = jnp.concatenate([starts + counts, ends, ends[-1:]]).astype(jnp.int32)
    n_tiles_max = (t * TOP_K) // tile + n_experts
    first_row = jnp.arange(n_tiles_max, dtype=jnp.int32) * tile
    tile_expert = jnp.minimum(jnp.sum(first_row[:, None] >= ends[None, :], axis=1),
                              n_experts - 1).astype(jnp.int32)
    n_tiles = (ends[-1:] // tile).astype(jnp.int32)
    return slots.astype(jnp.int32), bounds, tile_expert, n_tiles


def _ple_combine_kernel(slots_ref, x_ref, g_ref, wg_ref, p_ref, wp_ref, fg_ref, ys_ref,
                        o_ref, ybuf, sem, *, final_norm):
    i = pl.program_id(0)
    n = pl.num_programs(0)
    tm, d = x_ref.shape
    n_tokens = n * tm
    chunk_cols = d // COMBINE_CHUNKS
    chunk_rows = tm // COMBINE_CHUNKS

    def start_rows(step, buf, first, last):
        for r in range(first, last):
            for k in range(TOP_K):
                slot = slots_ref[k * n_tokens + step * tm + r]
                pltpu.make_async_copy(ys_ref.at[pl.ds(slot, 1), :],
                                      ybuf.at[buf, k, pl.ds(r, 1), :], sem.at[buf]).start()

    def wait_tile(buf):
        for k in range(TOP_K):
            pltpu.make_async_copy(ys_ref.at[pl.ds(0, tm), :], ybuf.at[buf, k],
                                  sem.at[buf]).wait()

    cur = i % 2

    @pl.when(i == 0)
    def _():
        start_rows(i, cur, 0, tm)

    wait_tile(cur)
    x = x_ref[...]
    for k in range(TOP_K):
        x = x + ybuf[cur, k]
    h = _rms_norm(x, g_ref[...]).astype(BF16)
    pb = p_ref[...].astype(BF16)
    nxt = jnp.minimum(i + 1, n - 1)
    for c in range(COMBINE_CHUNKS):
        cols = slice(c * chunk_cols, (c + 1) * chunk_cols)
        gate = _sigmoid(_dot(h, wg_ref[:, cols]))
        o_ref[:, cols] = x[:, cols] + gate * _dot(pb, wp_ref[:, cols])
        start_rows(nxt, 1 - cur, c * chunk_rows, (c + 1) * chunk_rows)
    if final_norm:
        o_ref[...] = _rms_norm(o_ref[...], fg_ref[...])

    @pl.when(i == n - 1)
    def _():
        wait_tile(1 - cur)


def _ple_combine(slots_flat, x, gain, wg, p, layer, wp, final_gain, ys, *, tm, final_norm):
    t, d = x.shape
    dp = p.shape[-1]
    row = lambda i, slots: (i, 0)
    const = lambda i, slots: (0, 0)
    return pl.pallas_call(
        functools.partial(_ple_combine_kernel, final_norm=final_norm),
        grid_spec=pltpu.PrefetchScalarGridSpec(
            num_scalar_prefetch=1,
            grid=(t // tm,),
            in_specs=[pl.BlockSpec((tm, d), row),
                      pl.BlockSpec((1, d), const),
                      pl.BlockSpec((d, d), const),
                      pl.BlockSpec((None, tm, dp), lambda i, slots: (layer, i, 0)),
                      pl.BlockSpec((dp, d), const),
                      pl.BlockSpec((1, d), const),
                      pl.BlockSpec(memory_space=pl.ANY)],
            out_specs=pl.BlockSpec((tm, d), row),
            scratch_shapes=[pltpu.VMEM((2, TOP_K, tm, d), F32),
                            pltpu.SemaphoreType.DMA((2,))]),
        out_shape=jax.ShapeDtypeStruct((t, d), F32),
        compiler_params=_params("arbitrary"),
        name="ple_combine",
    )(slots_flat, x, gain.reshape(1, d), wg, p, wp, final_gain.reshape(1, d), ys)


def kernel(x, p, final_norm_gain, lb_table, mix_norm_even, w_in_even, hgrn_norm_gain, conv_w, w_out_even, ffn_norm_even, w_gate_dense, w_up_dense, w_down_dense, mix_norm_odd, pool_w, pool_scale, ffn_norm_odd, w_router, w_gate_exp, w_up_exp, w_down_exp, ple_norm, ple_gate_w, ple_proj):
    batch, s_len, d = x.shape
    t = batch * s_len
    depth = p.shape[0]
    assert depth == 2, "the weight-cast schedule below is written for one layer pair"
    n_experts = w_router.shape[-1]
    tile = EXPERT_ROW_TILE
    xs = x.reshape(t, d)

    pp = p.reshape(depth, t, -1)

    u, _ = _norm_matmul(xs, mix_norm_even[0], w_in_even[0].astype(BF16), [], tm=1024, tn=1792)
    a, b, (w_out, w_gate, w_up, w_down) = _hgrn_conv(
        u, lb_table, hgrn_norm_gain[0], conv_w[0],
        [(w_out_even, 0), (w_gate_dense, 0), (w_up_dense, 0), (w_down_dense, 0)],
        batch=batch, rows=1024, layer=0)
    xs, (ple_gate0, ple_proj0, pool_wb) = _out_proj(
        xs, a.reshape(t, -1), b.reshape(t, -1), w_out,
        [(ple_gate_w, 0), (ple_proj, 0), (pool_w, 0)], tm=512, tn=d)
    xs, (wg_exp, wu_exp, wd_exp) = _swiglu(
        xs, ffn_norm_even[0], w_gate, w_up, w_down,
        [(w_gate_exp, 0), (w_up_exp, 0), (w_down_exp, 0)], tm=512, tf=512)
    xs = _ple(xs, ple_norm[0], ple_gate0, pp, 0, ple_proj0, final_norm_gain,
              tm=512, final_norm=False)

    xs, hg, route, counts = _pool_router(
        xs, mix_norm_odd[0], pool_wb, pool_scale[0], ffn_norm_odd[0], w_router[0],
        batch=batch, ts=512)
    slots, bounds, tile_expert, n_tiles = _routing_tables(
        route, counts, n_experts=n_experts, tile=tile)
    rows = _dispatch(slots, bounds, hg, td=512, n_experts=n_experts, tile=tile)
    ys, (ple_gate1, ple_proj1) = _experts(
        tile_expert, n_tiles, rows, wg_exp, wu_exp, wd_exp, [(ple_gate_w, 1), (ple_proj, 1)],
        tile=tile)
    xs = _ple_combine(slots, xs, ple_norm[1], ple_gate1, pp, 1, ple_proj1,
                      final_norm_gain, ys, tm=256, final_norm=True)
    return xs.reshape(batch, s_len, d)
```

```python
import functools
import math

import jax
import jax.numpy as jnp
from jax import lax
from jax.experimental import pallas as pl
from jax.experimental.pallas import tpu as pltpu

F32 = jnp.float32
BF16 = jnp.bfloat16

EPS = 1e-6
HGRN_HEADS = 8
HGRN_CHUNK = 64
CONV_WIDTH = 3
POOL_WINDOWS = (2, 4, 8, 16)
POOL_HALO = 16
TOP_K = 2
EXPERT_ROW_TILE = 512
COMBINE_CHUNKS = 8
LANES = 128
SUBLANES = 8
BF16_SUBLANES = 16
VMEM_LIMIT_BYTES = 56 * 1024 * 1024


def _params(*semantics):
    return pltpu.CompilerParams(dimension_semantics=semantics,
                                vmem_limit_bytes=VMEM_LIMIT_BYTES)


def _rms_norm(x, gain):
    ms = jnp.mean(x * x, axis=-1, keepdims=True)
    return x * lax.rsqrt(ms + EPS) * gain


def _sigmoid(x):
    return 1.0 / (1.0 + jnp.exp(-x))


def _dot(a, b):
    return jnp.dot(a, b, preferred_element_type=F32)


def _dot_nt(a, b):
    return lax.dot_general(a, b, (((1,), (1,)), ((), ())), preferred_element_type=F32)


class _Casts:
    def __init__(self, weights, grid):
        n_steps = math.prod(grid)
        self.shapes = [stacked.shape[1:] for stacked, _ in weights]
        self.views, self.in_specs, self.out_specs, self.out_shapes = [], [], [], []
        for stacked, layer in weights:
            cols = stacked.shape[-1]
            rows = stacked[0].size // cols
            block_rows = next(br for br in range(BF16_SUBLANES, rows + 1, BF16_SUBLANES)
                              if rows % br == 0 and rows // br <= n_steps)
            n_blocks = rows // block_rows
            self.views.append(stacked.reshape(stacked.shape[0] * rows, cols))
            for specs, first in ((self.in_specs, layer * n_blocks), (self.out_specs, 0)):
                specs.append(pl.BlockSpec(
                    (block_rows, cols),
                    functools.partial(self._index_map, grid, n_blocks, first)))
            self.out_shapes.append(jax.ShapeDtypeStruct((rows, cols), BF16))

    @staticmethod
    def _index_map(grid, n_blocks, first, *args):
        step = 0
        for size, idx in zip(grid, args):
            step = step * size + idx
        return first + jnp.minimum(step, n_blocks - 1), 0

    def __len__(self):
        return len(self.views)

    def wrap(self, body, n_in, n_out, n_prefetch=0):
        n = len(self)

        def kernel(*refs):
            ins_end = n_prefetch + n_in
            outs_start = ins_end + n
            outs_end = outs_start + n_out
            body(*refs[:ins_end], *refs[outs_start:outs_end], *refs[outs_end + n:])
            for src, dst in zip(refs[ins_end:outs_start], refs[outs_end:outs_end + n]):
                dst[...] = src[...].astype(BF16)

        return kernel

    def restore(self, outs):
        return [o.reshape(shape) for o, shape in zip(outs, self.shapes)]


def _hgrn_head(u, dh, st, tri, lb, gain):
    rows = u.shape[0]
    ch = HGRN_CHUNK
    ct = tri.shape[0]
    n_chunks = rows // ch
    row = lax.broadcasted_iota(jnp.int32, (ct, ct), 0)
    col = lax.broadcasted_iota(jnp.int32, (ct, ct), 1)
    causal = (row >= col) & (row // ch == col // ch)

    qr = u[:, 0:dh]
    q = qr * _sigmoid(qr)
    f = lb + (1.0 - lb) * _sigmoid(u[:, dh:2 * dh])
    k = 1.0 - f
    v = u[:, 2 * dh:3 * dh].astype(BF16)
    lf = jnp.log(f)
    hi = lf.astype(BF16)
    r1 = lf - hi.astype(F32)
    mid = r1.astype(BF16)
    lo = (r1 - mid.astype(F32)).astype(BF16)
    parts = jnp.concatenate([hi, mid, lo], axis=1)
    cums = []
    for j in range(rows // ct):
        c3 = _dot(tri, parts[j * ct:(j + 1) * ct])
        cums.append(c3[:, :dh] + c3[:, dh:2 * dh] + c3[:, 2 * dh:])
    cum = jnp.concatenate(cums, axis=0)

    cum3 = cum.reshape(n_chunks, ch, dh)
    ref = cum3[:, ch // 2:ch // 2 + 1, :]
    last = cum3[:, ch - 1:ch, :]
    q3 = q.reshape(n_chunks, ch, dh)
    k3 = k.reshape(n_chunks, ch, dh)
    qe = (q3 * jnp.exp(cum3 - ref)).astype(BF16).reshape(rows, dh)
    ke = (k3 * jnp.exp(ref - cum3)).astype(BF16).reshape(rows, dh)
    qc = (q3 * jnp.exp(cum3)).astype(BF16).reshape(rows, dh)
    kd = (k3 * jnp.exp(last - cum3)).astype(BF16).reshape(rows, dh)
    decay = jnp.exp(last)

    chunks = [slice(c * ch, (c + 1) * ch) for c in range(n_chunks)]
    kv_t = [lax.dot_general(v[cs], kd[cs], (((0,), (0,)), ((), ())),
                            preferred_element_type=F32) for cs in chunks]
    states = []
    for c in range(n_chunks):
        states.append(st.astype(BF16))
        st = decay[c] * st + kv_t[c]
    o_inter = jnp.concatenate(
        [_dot_nt(qc[cs], s_c) for cs, s_c in zip(chunks, states)], axis=0)

    o_intra = []
    for j in range(rows // ct):
        sl = slice(j * ct, (j + 1) * ct)
        scores = jnp.where(causal, _dot_nt(qe[sl], ke[sl]), 0.0)
        o_intra.append(_dot(scores.astype(BF16), v[sl]))
    o = jnp.concatenate(o_intra, axis=0) + o_inter
    o = o * lax.rsqrt(jnp.mean(o * o, axis=-1, keepdims=True) + EPS)
    gr = u[:, 3 * dh:4 * dh]
    return (o * gain * (gr * _sigmoid(gr))).astype(BF16), st


def _conv_head(u, dh, carry, cw):
    tt = u[:, 5 * dh:6 * dh] * u[:, 6 * dh:7 * dh]
    head = jnp.concatenate([carry, tt[0:SUBLANES]], axis=0)
    conv = tt * cw[CONV_WIDTH - 1:CONV_WIDTH, :]
    for back in range(1, CONV_WIDTH):
        shifted = jnp.concatenate([pltpu.roll(head, back, 0)[SUBLANES:],
                                   pltpu.roll(tt, back, 0)[SUBLANES:]], axis=0)
        conv = conv + shifted * cw[CONV_WIDTH - 1 - back:CONV_WIDTH - back, :]
    return (u[:, 4 * dh:5 * dh] * conv).astype(BF16), tt[tt.shape[0] - SUBLANES:]


def _inproj_hgrn_kernel(x_ref, g_ref, w_ref, tri_ref, lbt_ref, gain_ref, cw_ref,
                        a_ref, b_ref, hn_ref, u_ref, st_ref, cv_ref,
                        *, n_heads, tiles_per_seq, layer):
    j = pl.program_id(0)
    n = pl.num_programs(0)
    dh = a_ref.shape[1]

    @pl.when(j == 0)
    def _():
        u_ref[...] = jnp.zeros_like(u_ref)
        st_ref[...] = jnp.zeros_like(st_ref)
        cv_ref[...] = jnp.zeros_like(cv_ref)

    @pl.when((j % n_heads == 0) & (j < n - 1))
    def _():
        hn_ref[...] = _rms_norm(x_ref[...], g_ref[...]).astype(BF16)

    item = jnp.maximum(j - 1, 0)
    head = item % n_heads
    fresh = (item // n_heads) % tiles_per_seq == 0

    def step(u_prev, u_next):
        u_next[...] = _dot(hn_ref[...], w_ref[...])
        lbt = lbt_ref[...]
        e = jnp.exp(lbt - jnp.max(lbt, axis=0, keepdims=True))
        lb = (jnp.sum(e[0:layer + 1, :], axis=0, keepdims=True)
              / jnp.sum(e, axis=0, keepdims=True))
        st0 = jnp.where(fresh, 0.0, st_ref[head])
        a, st = _hgrn_head(u_prev, dh, st0, tri_ref[...], lb, gain_ref[...])
        a_ref[...] = a
        st_ref[head] = st
        carry = jnp.where(fresh, 0.0, cv_ref[head])
        b, carry = _conv_head(u_prev, dh, carry, cw_ref[...])
        b_ref[...] = b
        cv_ref[head] = carry

    pl.when(j % 2 == 0)(functools.partial(step, u_ref.at[1], u_ref.at[0]))
    pl.when(j % 2 == 1)(functools.partial(step, u_ref.at[0], u_ref.at[1]))


def _inproj_hgrn(x, gain, w_in, lb_table, hgrn_gain, conv_w, cast_weights,
                 *, batch, ts, layer):
    t, d = x.shape
    nh = HGRN_HEADS
    d_hgrn = hgrn_gain.shape[0]
    dh = d_hgrn // nh
    d_conv = conv_w.shape[1]
    assert d_conv == d_hgrn and w_in.shape[1] == 4 * d_hgrn + 3 * d_conv
    tiles_per_seq = t // batch // ts
    n_items = (t // ts) * nh
    w_heads = w_in.reshape(d, 7, nh, dh).transpose(2, 0, 1, 3).reshape(nh, d, 7 * dh)
    w_heads = w_heads.astype(BF16)

    idx = jnp.arange(2 * HGRN_CHUNK)
    tri = ((idx[:, None] >= idx[None, :])
           & (idx[:, None] // HGRN_CHUNK == idx[None, :] // HGRN_CHUNK)).astype(BF16)

    mm_item = lambda j: jnp.minimum(j, n_items - 1)
    rec_item = lambda j: jnp.maximum(j - 1, 0)
    rec_head = lambda j: (0, rec_item(j) % nh)
    out_spec = pl.BlockSpec((ts, dh), lambda j: (rec_item(j) // nh, rec_item(j) % nh))
    grid = (n_items + 1,)
    casts = _Casts(cast_weights, grid)
    a, b, *cast = pl.pallas_call(
        casts.wrap(functools.partial(_inproj_hgrn_kernel, n_heads=nh,
                                     tiles_per_seq=tiles_per_seq, layer=layer),
                   n_in=7, n_out=2),
        grid=grid,
        in_specs=[pl.BlockSpec((ts, d), lambda j: (mm_item(j) // nh, 0)),
                  pl.BlockSpec((1, d), lambda j: (0, 0)),
                  pl.BlockSpec((None, d, 7 * dh), lambda j: (mm_item(j) % nh, 0, 0)),
                  pl.BlockSpec(tri.shape, lambda j: (0, 0)),
                  pl.BlockSpec((lb_table.shape[0], dh), rec_head),
                  pl.BlockSpec((1, dh), rec_head),
                  pl.BlockSpec((CONV_WIDTH, dh), rec_head),
                  *casts.in_specs],
        out_specs=[out_spec, out_spec, *casts.out_specs],
        out_shape=[jax.ShapeDtypeStruct((t, d_hgrn), BF16),
                   jax.ShapeDtypeStruct((t, d_conv), BF16),
                   *casts.out_shapes],
        scratch_shapes=[pltpu.VMEM((ts, d), BF16),
                        pltpu.VMEM((2, ts, 7 * dh), F32),
                        pltpu.VMEM((nh, dh, dh), F32),
                        pltpu.VMEM((nh, SUBLANES, dh), F32)],
        compiler_params=_params("arbitrary"),
        name="inproj_hgrn",
    )(x, gain.reshape(1, d), w_heads, tri, lb_table, hgrn_gain.reshape(1, d_hgrn), conv_w,
      *casts.views)
    return a, b, casts.restore(cast)


def _out_proj_kernel(x_ref, a_ref, b_ref, wa_ref, wb_ref, o_ref):
    o_ref[...] = x_ref[...] + _dot(a_ref[...], wa_ref[...]) + _dot(b_ref[...], wb_ref[...])


def _out_proj(x, a, b, w, cast_weights, *, tm, tn):
    t, d = x.shape
    ka, kb = a.shape[1], b.shape[1]
    assert ka == kb
    grid = (t // tm, d // tn)
    casts = _Casts(cast_weights, grid)
    out, *cast = pl.pallas_call(
        casts.wrap(_out_proj_kernel, n_in=5, n_out=1),
        grid=grid,
        in_specs=[pl.BlockSpec((tm, tn), lambda i, j: (i, j)),
                  pl.BlockSpec((tm, ka), lambda i, j: (i, 0)),
                  pl.BlockSpec((tm, kb), lambda i, j: (i, 0)),
                  pl.BlockSpec((ka, tn), lambda i, j: (0, j)),
                  pl.BlockSpec((kb, tn), lambda i, j: (1, j)),
                  *casts.in_specs],
        out_specs=[pl.BlockSpec((tm, tn), lambda i, j: (i, j)), *casts.out_specs],
        out_shape=[jax.ShapeDtypeStruct((t, d), F32), *casts.out_shapes],
        compiler_params=_params("parallel", "arbitrary"),
        name="out_proj",
    )(x, a, b, w, w, *casts.views)
    return out, casts.restore(cast)


def _swiglu_kernel(x_ref, g_ref, wg_ref, wu_ref, wd_ref, o_ref, h_ref):
    @pl.when(pl.program_id(1) == 0)
    def _():
        x = x_ref[...]
        h_ref[...] = _rms_norm(x, g_ref[...]).astype(BF16)
        o_ref[...] = x

    h = h_ref[...]
    gate = _dot(h, wg_ref[...])
    up = _dot(h, wu_ref[...])
    act = (gate * _sigmoid(gate) * up).astype(BF16)
    o_ref[...] += _dot(act, wd_ref[...])


def _swiglu(x, gain, wg, wu, wd, cast_weights, *, tm, tf):
    t, d = x.shape
    ff = wg.shape[1]
    grid = (t // tm, ff // tf)
    casts = _Casts(cast_weights, grid)
    out, *cast = pl.pallas_call(
        casts.wrap(_swiglu_kernel, n_in=5, n_out=1),
        grid=grid,
        in_specs=[pl.BlockSpec((tm, d), lambda i, f: (i, 0)),
                  pl.BlockSpec((1, d), lambda i, f: (0, 0)),
                  pl.BlockSpec((d, tf), lambda i, f: (0, f)),
                  pl.BlockSpec((d, tf), lambda i, f: (0, f)),
                  pl.BlockSpec((tf, d), lambda i, f: (f, 0)),
                  *casts.in_specs],
        out_specs=[pl.BlockSpec((tm, d), lambda i, f: (i, 0)), *casts.out_specs],
        out_shape=[jax.ShapeDtypeStruct((t, d), F32), *casts.out_shapes],
        scratch_shapes=[pltpu.VMEM((tm, d), BF16)],
        compiler_params=_params("parallel", "arbitrary"),
        name="swiglu",
    )(x, gain.reshape(1, d), wg, wu, wd, *casts.views)
    return out, casts.restore(cast)


def _ple_math(x, g_ref, wg_ref, p_ref, wp_ref, fg_ref, final_norm):
    h = _rms_norm(x, g_ref[...]).astype(BF16)
    gate = _sigmoid(_dot(h, wg_ref[...]))
    y = x + gate * _dot(p_ref[...].astype(BF16), wp_ref[...])
    if final_norm:
        y = _rms_norm(y, fg_ref[...])
    return y


def _ple_kernel(x_ref, g_ref, wg_ref, p_ref, wp_ref, fg_ref, o_ref, *, final_norm):
    o_ref[...] = _ple_math(x_ref[...], g_ref, wg_ref, p_ref, wp_ref, fg_ref, final_norm)


def _ple(x, gain, wg, p, layer, wp, final_gain, *, tm, final_norm):
    t, d = x.shape
    dp = p.shape[-1]
    return pl.pallas_call(
        functools.partial(_ple_kernel, final_norm=final_norm),
        grid=(t // tm,),
        in_specs=[pl.BlockSpec((tm, d), lambda i: (i, 0)),
                  pl.BlockSpec((1, d), lambda i: (0, 0)),
                  pl.BlockSpec((d, d), lambda i: (0, 0)),
                  pl.BlockSpec((None, tm, dp), lambda i: (layer, i, 0)),
                  pl.BlockSpec((dp, d), lambda i: (0, 0)),
                  pl.BlockSpec((1, d), lambda i: (0, 0))],
        out_specs=pl.BlockSpec((tm, d), lambda i: (i, 0)),
        out_shape=jax.ShapeDtypeStruct((t, d), F32),
        compiler_params=_params("parallel"),
        name="ple",
    )(x, gain.reshape(1, d), wg, p, wp, final_gain.reshape(1, d))


def _pool_router_kernel(x_ref, g_ref, pw_ref, ps_ref, fg_ref, wr_ref, lt_ref,
                        x1_ref, hg_ref, route_ref, counts_ref, ext_ref, cnt_ref,
                        *, n_experts):
    s = pl.program_id(1)
    ts, d = x_ref.shape
    n_groups = len(POOL_WINDOWS)
    gd = d // n_groups

    @pl.when(s == 0)
    def _():
        ext_ref[0:POOL_HALO, :] = jnp.zeros((POOL_HALO, d), F32)

    @pl.when(s > 0)
    def _():
        ext_ref[0:POOL_HALO, :] = ext_ref[ts:ts + POOL_HALO, :]

    x = x_ref[...]
    h = _rms_norm(x, g_ref[...])
    ext_ref[POOL_HALO:, :] = h

    def window_sum(cols, w):
        acc = ext_ref[:, cols]
        span = 1
        while span < w:
            acc = acc + pltpu.roll(acc, span, 0)
            span *= 2
        return acc[POOL_HALO:, :]

    pos = (s * ts + 1 + lax.broadcasted_iota(jnp.int32, (ts, 1), 0)).astype(F32)
    ys = []
    for gi, w in enumerate(POOL_WINDOWS):
        cols = slice(gi * gd, (gi + 1) * gd)
        mean = window_sum(cols, w) / jnp.minimum(pos, float(w))
        diff = (mean - h[:, cols]).astype(BF16)
        ys.append(_dot(diff, pw_ref[gi]))
    y = jnp.concatenate(ys, axis=1)
    x1 = x + y * ps_ref[...]
    x1_ref[...] = x1

    h2 = _rms_norm(x1, fg_ref[...])
    hg_ref[:, 0:d] = h2

    hi = h2.astype(BF16)
    mid = (h2 - hi.astype(F32)).astype(BF16)
    p_hi = _dot(hi, wr_ref[...])
    p_mid = _dot(mid, wr_ref[...])
    logits = (p_hi + pltpu.roll(p_hi, LANES - n_experts, 1)
              + pltpu.roll(p_hi, LANES - 2 * n_experts, 1)
              + p_mid + pltpu.roll(p_mid, LANES - n_experts, 1))
    lane = lax.broadcasted_iota(jnp.int32, logits.shape, 1)
    valid = lane < n_experts
    logits = jnp.where(valid, logits, -jnp.inf)
    ex = jnp.exp(logits - jnp.max(logits, axis=-1, keepdims=True))
    probs = ex / jnp.sum(ex, axis=-1, keepdims=True)
    gates = jnp.zeros_like(probs)
    rest = jnp.where(valid, probs, -1.0)
    tops = []
    for _ in range(TOP_K):
        m = jnp.max(rest, axis=-1, keepdims=True)
        first = jnp.min(jnp.where(rest == m, lane, LANES), axis=-1, keepdims=True)
        sel = lane == first
        tops.append((m, sel))
        rest = jnp.where(sel, -1.0, rest)
    total = tops[0][0]
    for m, _ in tops[1:]:
        total = total + m
    for m, sel in tops:
        gates = jnp.where(sel, m / total, gates)
    hg_ref[:, d:d + LANES] = gates

    first_step = (pl.program_id(0) == 0) & (s == 0)

    @pl.when(first_step)
    def _():
        cnt_ref[...] = jnp.zeros_like(cnt_ref)

    chosen = tops[0][1]
    for _, sel in tops[1:]:
        chosen = chosen | sel
    chosen = jnp.where(chosen, 1.0, 0.0)
    before = cnt_ref[...] + _dot(lt_ref[...], chosen.astype(BF16))
    cnt_ref[...] = cnt_ref[...] + jnp.sum(chosen, axis=0, keepdims=True)
    counts_ref[...] = cnt_ref[...]
    lane_f = lane.astype(F32)
    route = jnp.zeros((ts, LANES), F32)
    for k, (_, sel) in enumerate(tops):
        expert_k = jnp.sum(jnp.where(sel, lane_f, 0.0), axis=-1, keepdims=True)
        order_k = jnp.sum(jnp.where(sel, before, 0.0), axis=-1, keepdims=True)
        route = jnp.where(lane == k, expert_k, route)
        route = jnp.where(lane == TOP_K + k, order_k, route)
    route_ref[...] = route.T[0:SUBLANES, :].astype(jnp.int32)


def _pool_router(x, gain, pool_w, pool_scale, ffn_gain, w_router, *, batch, ts):
    t, d = x.shape
    s_len = t // batch
    n_s = s_len // ts
    n_groups, gd, _ = pool_w.shape
    n_experts = w_router.shape[1]
    assert 3 * n_experts <= LANES
    w_hi = w_router.astype(BF16)
    r1 = w_router - w_hi.astype(F32)
    w_mid = r1.astype(BF16)
    w_lo = (r1 - w_mid.astype(F32)).astype(BF16)
    wr = jnp.zeros((d, LANES), BF16).at[:, :3 * n_experts].set(
        jnp.concatenate([w_hi, w_mid, w_lo], axis=1))
    idx = jnp.arange(ts)
    lower = (idx[None, :] < idx[:, None]).astype(BF16)
    row = lambda b, s: (b * n_s + s, 0)
    const = lambda b, s: (0, 0)
    return pl.pallas_call(
        functools.partial(_pool_router_kernel, n_experts=n_experts),
        grid=(batch, n_s),
        in_specs=[pl.BlockSpec((ts, d), row),
                  pl.BlockSpec((1, d), const),
                  pl.BlockSpec((n_groups, gd, gd), lambda b, s: (0, 0, 0)),
                  pl.BlockSpec((1, d), const),
                  pl.BlockSpec((1, d), const),
                  pl.BlockSpec((d, LANES), const),
                  pl.BlockSpec((ts, ts), const)],
        out_specs=[pl.BlockSpec((ts, d), row),
                   pl.BlockSpec((ts, d + LANES), row),
                   pl.BlockSpec((SUBLANES, ts), lambda b, s: (0, b * n_s + s)),
                   pl.BlockSpec((1, LANES), const)],
        out_shape=[jax.ShapeDtypeStruct((t, d), F32),
                   jax.ShapeDtypeStruct((t, d + LANES), F32),
                   jax.ShapeDtypeStruct((SUBLANES, t), jnp.int32),
                   jax.ShapeDtypeStruct((1, LANES), F32)],
        scratch_shapes=[pltpu.VMEM((POOL_HALO + ts, d), F32),
                        pltpu.VMEM((1, LANES), F32)],
        compiler_params=_params("arbitrary", "arbitrary"),
        name="pool_router",
    )(x, gain.reshape(1, d), pool_w, pool_scale.reshape(1, d), ffn_gain.reshape(1, d), wr, lower)


def _dispatch_kernel(slots_ref, bounds_ref, hg_ref, xs_ref, zero_ref, sem, zero_sem,
                     *, n_experts, tile):
    i = pl.program_id(0)
    td = hg_ref.shape[0]
    n_tokens = pl.num_programs(0) * td

    for r in range(td):
        for k in range(TOP_K):
            slot = slots_ref[k * n_tokens + i * td + r]
            pltpu.make_async_copy(hg_ref.at[pl.ds(r, 1), :],
                                  xs_ref.at[pl.ds(slot, 1), :], sem).start()

    def zero_copies():
        copies = []

        def block(first, size):
            return pltpu.make_async_copy(zero_ref.at[pl.ds(0, size), :],
                                         xs_ref.at[pl.ds(first, size), :], zero_sem)

        for e in range(n_experts):
            first_unused = bounds_ref[e]
            end = bounds_ref[n_experts + e]
            aligned = (first_unused + SUBLANES - 1) // SUBLANES * SUBLANES
            for r in range(SUBLANES - 1):
                copies.append((first_unused + r < aligned, block(first_unused + r, 1)))
            left = end - aligned
            pos = aligned
            size = tile // 2
            while size >= SUBLANES:
                needed = (left & size) != 0
                copies.append((needed, block(pl.multiple_of(pos, SUBLANES), size)))
                pos = pos + jnp.where(needed, size, 0)
                size //= 2
        total = bounds_ref[2 * n_experts]
        for b in range(n_experts):
            first = pl.multiple_of(total + b * tile, tile)
            copies.append((first < xs_ref.shape[0], block(first, tile)))
        return copies

    @pl.when(i == 0)
    def _():
        zero_ref[...] = jnp.zeros_like(zero_ref)
        for action in ("start", "wait"):
            for needed, copy in zero_copies():
                pl.when(needed)(getattr(copy, action))

    for _ in range(TOP_K):
        pltpu.make_async_copy(hg_ref, xs_ref.at[pl.ds(0, td), :], sem).wait()


def _dispatch(slots_flat, bounds, hg, *, td, n_experts, tile):
    t, width = hg.shape
    n_rows = t * TOP_K + n_experts * tile
    return pl.pallas_call(
        functools.partial(_dispatch_kernel, n_experts=n_experts, tile=tile),
        grid_spec=pltpu.PrefetchScalarGridSpec(
            num_scalar_prefetch=2,
            grid=(t // td,),
            in_specs=[pl.BlockSpec((td, width), lambda i, slots, bounds: (i, 0))],
            out_specs=pl.BlockSpec(memory_space=pl.ANY),
            scratch_shapes=[pltpu.VMEM((tile, width), F32),
                            pltpu.SemaphoreType.DMA,
                            pltpu.SemaphoreType.DMA]),
        out_shape=jax.ShapeDtypeStruct((n_rows, width), F32),
        compiler_params=_params("arbitrary"),
        name="moe_dispatch",
    )(slots_flat, bounds, hg)


def _experts_kernel(te_ref, nt_ref, xs_ref, wg_ref, wu_ref, wd_ref, ys_ref):
    i = pl.program_id(0)
    d = wg_ref.shape[0]

    @pl.when(i < nt_ref[0])
    def _():
        rows = xs_ref[:, 0:d].astype(BF16)
        gates = xs_ref[:, d:d + LANES]
        lane = lax.broadcasted_iota(jnp.int32, gates.shape, 1)
        ge = jnp.sum(jnp.where(lane == te_ref[i], gates, 0.0), axis=-1, keepdims=True)
        gate = _dot(rows, wg_ref[...])
        up = _dot(rows, wu_ref[...])
        act = (ge * (gate * _sigmoid(gate) * up)).astype(BF16)
        ys_ref[...] = _dot(act, wd_ref[...])

    @pl.when(i >= nt_ref[0])
    def _():
        ys_ref[...] = jnp.zeros_like(ys_ref)


def _experts(tile_expert, n_tiles, xs, wg, wu, wd, cast_weights, *, tile):
    n_rows, width = xs.shape
    n_experts, d, ffe = wg.shape
    row_map = lambda i, te, nt: (i, 0)
    w_map = lambda i, te, nt: (te[i], 0, 0)
    grid = (n_rows // tile,)
    casts = _Casts(cast_weights, grid)
    out, *cast = pl.pallas_call(
        casts.wrap(_experts_kernel, n_in=4, n_out=1, n_prefetch=2),
        grid_spec=pltpu.PrefetchScalarGridSpec(
            num_scalar_prefetch=2,
            grid=grid,
            in_specs=[pl.BlockSpec((tile, width), row_map),
                      pl.BlockSpec((None, d, ffe), w_map),
                      pl.BlockSpec((None, d, ffe), w_map),
                      pl.BlockSpec((None, ffe, d), w_map),
                      *casts.in_specs],
            out_specs=[pl.BlockSpec((tile, d), row_map), *casts.out_specs]),
        out_shape=[jax.ShapeDtypeStruct((n_rows, d), F32), *casts.out_shapes],
        compiler_params=_params("arbitrary"),
        name="moe_experts",
    )(tile_expert, n_tiles, xs, wg, wu, wd, *casts.views)
    return out, casts.restore(cast)


def _routing_tables(route, counts, *, n_experts, tile):
    t = route.shape[1]
    counts = counts[0, :n_experts].astype(jnp.int32)
    padded = (counts + tile - 1) // tile * tile
    ends = jnp.cumsum(padded)
    starts = ends - padded
    expert, order = route[:TOP_K], route[TOP_K:2 * TOP_K]
    slots = order
    for e in range(n_experts):
        slots = slots + jnp.where(expert == e, starts[e], 0)
    slots = slots.reshape(-1)
    bounds = jnp.concatenate([starts + counts, ends, ends[-1:]]).astype(jnp.int32)
    n_tiles_max = (t * TOP_K) // tile + n_experts
    first_row = jnp.arange(n_tiles_max, dtype=jnp.int32) * tile
    tile_expert = jnp.minimum(jnp.sum(first_row[:, None] >= ends[None, :], axis=1),
                              n_experts - 1).astype(jnp.int32)
    n_tiles = (ends[-1:] // tile).astype(jnp.int32)
    return slots.astype(jnp.int32), bounds, tile_expert, n_tiles


def _ple_combine_kernel(slots_ref, x_ref, g_ref, wg_ref, p_ref, wp_ref, fg_ref, ys_ref,
                        o_ref, ybuf, sem, *, final_norm):
    i = pl.program_id(0)
    n = pl.num_programs(0)
    tm, d = x_ref.shape
    n_tokens = n * tm
    chunk_cols = d // COMBINE_CHUNKS
    chunk_rows = tm // COMBINE_CHUNKS

    def start_rows(step, buf, first, last):
        for r in range(first, last):
            for k in range(TOP_K):
                slot = slots_ref[k * n_tokens + step * tm + r]
                pltpu.make_async_copy(ys_ref.at[pl.ds(slot, 1), :],
                                      ybuf.at[buf, k, pl.ds(r, 1), :], sem.at[buf]).start()

    def wait_tile(buf):
        for k in range(TOP_K):
            pltpu.make_async_copy(ys_ref.at[pl.ds(0, tm), :], ybuf.at[buf, k],
                                  sem.at[buf]).wait()

    cur = i % 2

    @pl.when(i == 0)
    def _():
        start_rows(i, cur, 0, tm)

    wait_tile(cur)
    x = x_ref[...]
    for k in range(TOP_K):
        x = x + ybuf[cur, k]
    h = _rms_norm(x, g_ref[...]).astype(BF16)
    pb = p_ref[...].astype(BF16)
    nxt = jnp.minimum(i + 1, n - 1)
    for c in range(COMBINE_CHUNKS):
        cols = slice(c * chunk_cols, (c + 1) * chunk_cols)
        gate = _sigmoid(_dot(h, wg_ref[:, cols]))
        o_ref[:, cols] = x[:, cols] + gate * _dot(pb, wp_ref[:, cols])
        start_rows(nxt, 1 - cur, c * chunk_rows, (c + 1) * chunk_rows)
    if final_norm:
        o_ref[...] = _rms_norm(o_ref[...], fg_ref[...])

    @pl.when(i == n - 1)
    def _():
        wait_tile(1 - cur)


def _ple_combine(slots_flat, x, gain, wg, p, layer, wp, final_gain, ys, *, tm, final_norm):
    t, d = x.shape
    dp = p.shape[-1]
    row = lambda i, slots: (i, 0)
    const = lambda i, slots: (0, 0)
    return pl.pallas_call(
        functools.partial(_ple_combine_kernel, final_norm=final_norm),
        grid_spec=pltpu.PrefetchScalarGridSpec(
            num_scalar_prefetch=1,
            grid=(t // tm,),
            in_specs=[pl.BlockSpec((tm, d), row),
                      pl.BlockSpec((1, d), const),
                      pl.BlockSpec((d, d), const),
                      pl.BlockSpec((None, tm, dp), lambda i, slots: (layer, i, 0)),
                      pl.BlockSpec((dp, d), const),
                      pl.BlockSpec((1, d), const),
                      pl.BlockSpec(memory_space=pl.ANY)],
            out_specs=pl.BlockSpec((tm, d), row),
            scratch_shapes=[pltpu.VMEM((2, TOP_K, tm, d), F32),
                            pltpu.SemaphoreType.DMA((2,))]),
        out_shape=jax.ShapeDtypeStruct((t, d), F32),
        compiler_params=_params("arbitrary"),
        name="ple_combine",
    )(slots_flat, x, gain.reshape(1, d), wg, p, wp, final_gain.reshape(1, d), ys)


def kernel(x, p, final_norm_gain, lb_table, mix_norm_even, w_in_even, hgrn_norm_gain, conv_w, w_out_even, ffn_norm_even, w_gate_dense, w_up_dense, w_down_dense, mix_norm_odd, pool_w, pool_scale, ffn_norm_odd, w_router, w_gate_exp, w_up_exp, w_down_exp, ple_norm, ple_gate_w, ple_proj):
    batch, s_len, d = x.shape
    t = batch * s_len
    depth = p.shape[0]
    assert depth == 2, "the weight-cast schedule below is written for one layer pair"
    n_experts = w_router.shape[-1]
    tile = EXPERT_ROW_TILE
    xs = x.reshape(t, d)

    pp = p.reshape(depth, t, -1)

    a, b, (w_out, w_gate, w_up, w_down, wg_exp, wu_exp, wd_exp) = _inproj_hgrn(
        xs, mix_norm_even[0], w_in_even[0], lb_table, hgrn_norm_gain[0], conv_w[0],
        [(w_out_even, 0), (w_gate_dense, 0), (w_up_dense, 0), (w_down_dense, 0),
         (w_gate_exp, 0), (w_up_exp, 0), (w_down_exp, 0)],
        batch=batch, ts=1024, layer=0)
    xs, (ple_gate0, ple_proj0, pool_wb) = _out_proj(
        xs, a, b, w_out,
        [(ple_gate_w, 0), (ple_proj, 0), (pool_w, 0)], tm=512, tn=d)
    xs, _ = _swiglu(xs, ffn_norm_even[0], w_gate, w_up, w_down, [], tm=1024, tf=256)
    xs = _ple(xs, ple_norm[0], ple_gate0, pp, 0, ple_proj0, final_norm_gain,
              tm=512, final_norm=False)

    xs, hg, route, counts = _pool_router(
        xs, mix_norm_odd[0], pool_wb, pool_scale[0], ffn_norm_odd[0], w_router[0],
        batch=batch, ts=512)
    slots, bounds, tile_expert, n_tiles = _routing_tables(
        route, counts, n_experts=n_experts, tile=tile)
    rows = _dispatch(slots, bounds, hg, td=512, n_experts=n_experts, tile=tile)
    ys, (ple_gate1, ple_proj1) = _experts(
        tile_expert, n_tiles, rows, wg_exp, wu_exp, wd_exp, [(ple_gate_w, 1), (ple_proj, 1)],
        tile=tile)
    xs = _ple_combine(slots, xs, ple_norm[1], ple_gate1, pp, 1, ple_proj1,
                      final_norm_gain, ys, tm=256, final_norm=True)
    return xs.reshape(batch, s_len, d)
```

```python
import functools
import math

import jax
import jax.numpy as jnp
from jax import lax
from jax.experimental import pallas as pl
from jax.experimental.pallas import tpu as pltpu

F32 = jnp.float32
BF16 = jnp.bfloat16

EPS = 1e-6
HGRN_HEADS = 8
HGRN_CHUNK = 64
CONV_WIDTH = 3
POOL_WINDOWS = (2, 4, 8, 16)
POOL_HALO = 16
TOP_K = 2
EXPERT_ROW_TILE = 512
COMBINE_CHUNKS = 8
LANES = 128
SUBLANES = 8
BF16_SUBLANES = 16
VMEM_LIMIT_BYTES = 56 * 1024 * 1024


def _params(*semantics):
    return pltpu.CompilerParams(dimension_semantics=semantics,
                                vmem_limit_bytes=VMEM_LIMIT_BYTES)


def _rms_norm(x, gain):
    ms = jnp.mean(x * x, axis=-1, keepdims=True)
    return x * lax.rsqrt(ms + EPS) * gain


def _sigmoid(x):
    return 1.0 / (1.0 + jnp.exp(-x))


def _dot(a, b):
    return jnp.dot(a, b, preferred_element_type=F32)


def _dot_nt(a, b):
    return lax.dot_general(a, b, (((1,), (1,)), ((), ())), preferred_element_type=F32)


class _Casts:
    def __init__(self, weights, grid):
        n_steps = math.prod(grid)
        self.shapes = [stacked.shape[1:] for stacked, _ in weights]
        self.views, self.in_specs, self.out_specs, self.out_shapes = [], [], [], []
        for stacked, layer in weights:
            cols = stacked.shape[-1]
            rows = stacked[0].size // cols
            block_rows = next(br for br in range(BF16_SUBLANES, rows + 1, BF16_SUBLANES)
                              if rows % br == 0 and rows // br <= n_steps)
            n_blocks = rows // block_rows
            self.views.append(stacked.reshape(stacked.shape[0] * rows, cols))
            for specs, first in ((self.in_specs, layer * n_blocks), (self.out_specs, 0)):
                specs.append(pl.BlockSpec(
                    (block_rows, cols),
                    functools.partial(self._index_map, grid, n_blocks, first)))
            self.out_shapes.append(jax.ShapeDtypeStruct((rows, cols), BF16))

    @staticmethod
    def _index_map(grid, n_blocks, first, *args):
        step = 0
        for size, idx in zip(grid, args):
            step = step * size + idx
        return first + jnp.minimum(step, n_blocks - 1), 0

    def __len__(self):
        return len(self.views)

    def wrap(self, body, n_in, n_out, n_prefetch=0):
        n = len(self)

        def kernel(*refs):
            ins_end = n_prefetch + n_in
            outs_start = ins_end + n
            outs_end = outs_start + n_out
            body(*refs[:ins_end], *refs[outs_start:outs_end], *refs[outs_end + n:])
            for src, dst in zip(refs[ins_end:outs_start], refs[outs_end:outs_end + n]):
                dst[...] = src[...].astype(BF16)

        return kernel

    def restore(self, outs):
        return [o.reshape(shape) for o, shape in zip(outs, self.shapes)]


def _norm_matmul_kernel(x_ref, g_ref, w_ref, o_ref, h_ref):
    @pl.when(pl.program_id(1) == 0)
    def _():
        h_ref[...] = _rms_norm(x_ref[...], g_ref[...]).astype(BF16)

    o_ref[...] = _dot(h_ref[...], w_ref[...])


def _norm_matmul(x, gain, w, cast_weights, *, tm, tn):
    t, d = x.shape
    n = w.shape[1]
    grid = (t // tm, n // tn)
    casts = _Casts(cast_weights, grid)
    out, *cast = pl.pallas_call(
        casts.wrap(_norm_matmul_kernel, n_in=3, n_out=1),
        grid=grid,
        in_specs=[pl.BlockSpec((tm, d), lambda i, j: (i, 0)),
                  pl.BlockSpec((1, d), lambda i, j: (0, 0)),
                  pl.BlockSpec((d, tn), lambda i, j: (0, j)),
                  *casts.in_specs],
        out_specs=[pl.BlockSpec((tm, tn), lambda i, j: (i, j)), *casts.out_specs],
        out_shape=[jax.ShapeDtypeStruct((t, n), F32), *casts.out_shapes],
        scratch_shapes=[pltpu.VMEM((tm, d), BF16)],
        compiler_params=_params("parallel", "arbitrary"),
        name="norm_matmul",
    )(x, gain.reshape(1, d), w, *casts.views)
    return out, casts.restore(cast)


def _hgrn_conv_kernel(tri_ref, q_ref, f_ref, i_ref, g_ref, gb_ref, gc_ref, vc_ref,
                      lbt_ref, gain_ref, cw_ref, a_ref, b_ref, *, rows, layer):
    s_len, dh = q_ref.shape
    ch = HGRN_CHUNK
    ct = tri_ref.shape[0]

    lbt = lbt_ref[...]
    e = jnp.exp(lbt - jnp.max(lbt, axis=0, keepdims=True))
    lb = (jnp.sum(e[0:layer + 1, :], axis=0, keepdims=True)
          / jnp.sum(e, axis=0, keepdims=True))
    gain = gain_ref[...]

    tri = tri_ref[...]
    row = lax.broadcasted_iota(jnp.int32, (ct, ct), 0)
    col = lax.broadcasted_iota(jnp.int32, (ct, ct), 1)
    causal = (row >= col) & (row // ch == col // ch)
    n_chunks = rows // ch

    def body(t, st):
        r0 = pl.multiple_of(t * rows, rows)
        qr = q_ref[pl.ds(r0, rows), :]
        q = qr * _sigmoid(qr)
        f = lb + (1.0 - lb) * _sigmoid(f_ref[pl.ds(r0, rows), :])
        k = 1.0 - f
        v = i_ref[pl.ds(r0, rows), :].astype(BF16)
        lf = jnp.log(f)
        hi = lf.astype(BF16)
        r1 = lf - hi.astype(F32)
        mid = r1.astype(BF16)
        lo = (r1 - mid.astype(F32)).astype(BF16)
        parts = jnp.concatenate([hi, mid, lo], axis=1)
        cums = []
        for j in range(rows // ct):
            c3 = _dot(tri, parts[j * ct:(j + 1) * ct])
            cums.append(c3[:, :dh] + c3[:, dh:2 * dh] + c3[:, 2 * dh:])
        cum = jnp.concatenate(cums, axis=0)

        cum3 = cum.reshape(n_chunks, ch, dh)
        ref = cum3[:, ch // 2:ch // 2 + 1, :]
        last = cum3[:, ch - 1:ch, :]
        q3 = q.reshape(n_chunks, ch, dh)
        k3 = k.reshape(n_chunks, ch, dh)
        qe = (q3 * jnp.exp(cum3 - ref)).astype(BF16).reshape(rows, dh)
        ke = (k3 * jnp.exp(ref - cum3)).astype(BF16).reshape(rows, dh)
        qc = (q3 * jnp.exp(cum3)).astype(BF16).reshape(rows, dh)
        kd = (k3 * jnp.exp(last - cum3)).astype(BF16).reshape(rows, dh)
        decay = jnp.exp(last)

        chunks = [slice(c * ch, (c + 1) * ch) for c in range(n_chunks)]
        kv_t = [lax.dot_general(v[cs], kd[cs], (((0,), (0,)), ((), ())),
                                preferred_element_type=F32) for cs in chunks]
        states = []
        for c in range(n_chunks):
            states.append(st.astype(BF16))
            st = decay[c] * st + kv_t[c]
        o_inter = jnp.concatenate(
            [_dot_nt(qc[cs], s_c) for cs, s_c in zip(chunks, states)], axis=0)

        o_intra = []
        for j in range(rows // ct):
            sl = slice(j * ct, (j + 1) * ct)
            scores = jnp.where(causal, _dot_nt(qe[sl], ke[sl]), 0.0)
            o_intra.append(_dot(scores.astype(BF16), v[sl]))
        o = jnp.concatenate(o_intra, axis=0) + o_inter
        o = o * lax.rsqrt(jnp.mean(o * o, axis=-1, keepdims=True) + EPS)
        gr = g_ref[pl.ds(r0, rows), :]
        a_ref[pl.ds(r0, rows), :] = (o * gain * (gr * _sigmoid(gr))).astype(BF16)
        return st

    lax.fori_loop(0, s_len // rows, body, jnp.zeros((dh, dh), F32))

    tt = gc_ref[...] * vc_ref[...]
    ridx = lax.broadcasted_iota(jnp.int32, tt.shape, 0)
    cw = cw_ref[...]
    conv = tt * cw[CONV_WIDTH - 1:CONV_WIDTH, :]
    for back in range(1, CONV_WIDTH):
        shifted = jnp.where(ridx >= back, pltpu.roll(tt, back, 0), 0.0)
        conv = conv + shifted * cw[CONV_WIDTH - 1 - back:CONV_WIDTH - back, :]
    b_ref[...] = (gb_ref[...] * conv).astype(BF16)


def _hgrn_conv(u, lb_table, hgrn_gain, conv_w, cast_weights, *, batch, rows, layer):
    t, n_in = u.shape
    s_len = t // batch
    d_hgrn = hgrn_gain.shape[0]
    dh = d_hgrn // HGRN_HEADS
    d_conv = conv_w.shape[1]
    nh = HGRN_HEADS
    assert d_conv // dh == nh and n_in == 4 * d_hgrn + 3 * d_conv
    u3 = u.reshape(batch, s_len, n_in)

    idx = jnp.arange(2 * HGRN_CHUNK)
    tri = ((idx[:, None] >= idx[None, :])
           & (idx[:, None] // HGRN_CHUNK == idx[None, :] // HGRN_CHUNK)).astype(BF16)

    def col(off):
        return pl.BlockSpec((None, s_len, dh), lambda b, h, off=off: (b, 0, off + h))

    n_lb = lb_table.shape[0]
    out_spec = pl.BlockSpec((None, s_len, dh), lambda b, h: (b, 0, h))
    grid = (batch, nh)
    casts = _Casts(cast_weights, grid)
    a, b, *cast = pl.pallas_call(
        casts.wrap(functools.partial(_hgrn_conv_kernel, rows=rows, layer=layer),
                   n_in=11, n_out=2),
        grid=grid,
        in_specs=[pl.BlockSpec(tri.shape, lambda b, h: (0, 0)),
                  col(0), col(nh), col(2 * nh), col(3 * nh),
                  col(4 * nh), col(5 * nh), col(6 * nh),
                  pl.BlockSpec((n_lb, dh), lambda b, h: (0, h)),
                  pl.BlockSpec((1, dh), lambda b, h: (0, h)),
                  pl.BlockSpec((CONV_WIDTH, dh), lambda b, h: (0, h)),
                  *casts.in_specs],
        out_specs=[out_spec, out_spec, *casts.out_specs],
        out_shape=[jax.ShapeDtypeStruct((batch, s_len, d_hgrn), BF16),
                   jax.ShapeDtypeStruct((batch, s_len, d_conv), BF16),
                   *casts.out_shapes],
        compiler_params=_params("parallel", "parallel"),
        name="hgrn_conv",
    )(tri, u3, u3, u3, u3, u3, u3, u3, lb_table, hgrn_gain.reshape(1, d_hgrn), conv_w,
      *casts.views)
    return a, b, casts.restore(cast)


def _out_proj_kernel(x_ref, a_ref, b_ref, wa_ref, wb_ref, o_ref):
    o_ref[...] = x_ref[...] + _dot(a_ref[...], wa_ref[...]) + _dot(b_ref[...], wb_ref[...])


def _out_proj(x, a, b, w, cast_weights, *, tm, tn):
    t, d = x.shape
    ka, kb = a.shape[1], b.shape[1]
    assert ka == kb
    grid = (t // tm, d // tn)
    casts = _Casts(cast_weights, grid)
    out, *cast = pl.pallas_call(
        casts.wrap(_out_proj_kernel, n_in=5, n_out=1),
        grid=grid,
        in_specs=[pl.BlockSpec((tm, tn), lambda i, j: (i, j)),
                  pl.BlockSpec((tm, ka), lambda i, j: (i, 0)),
                  pl.BlockSpec((tm, kb), lambda i, j: (i, 0)),
                  pl.BlockSpec((ka, tn), lambda i, j: (0, j)),
                  pl.BlockSpec((kb, tn), lambda i, j: (1, j)),
                  *casts.in_specs],
        out_specs=[pl.BlockSpec((tm, tn), lambda i, j: (i, j)), *casts.out_specs],
        out_shape=[jax.ShapeDtypeStruct((t, d), F32), *casts.out_shapes],
        compiler_params=_params("parallel", "arbitrary"),
        name="out_proj",
    )(x, a, b, w, w, *casts.views)
    return out, casts.restore(cast)


def _swiglu_kernel(x_ref, g_ref, wg_ref, wu_ref, wd_ref, o_ref, h_ref):
    @pl.when(pl.program_id(1) == 0)
    def _():
        x = x_ref[...]
        h_ref[...] = _rms_norm(x, g_ref[...]).astype(BF16)
        o_ref[...] = x

    h = h_ref[...]
    gate = _dot(h, wg_ref[...])
    up = _dot(h, wu_ref[...])
    act = (gate * _sigmoid(gate) * up).astype(BF16)
    o_ref[...] += _dot(act, wd_ref[...])


def _swiglu(x, gain, wg, wu, wd, cast_weights, *, tm, tf):
    t, d = x.shape
    ff = wg.shape[1]
    grid = (t // tm, ff // tf)
    casts = _Casts(cast_weights, grid)
    out, *cast = pl.pallas_call(
        casts.wrap(_swiglu_kernel, n_in=5, n_out=1),
        grid=grid,
        in_specs=[pl.BlockSpec((tm, d), lambda i, f: (i, 0)),
                  pl.BlockSpec((1, d), lambda i, f: (0, 0)),
                  pl.BlockSpec((d, tf), lambda i, f: (0, f)),
                  pl.BlockSpec((d, tf), lambda i, f: (0, f)),
                  pl.BlockSpec((tf, d), lambda i, f: (f, 0)),
                  *casts.in_specs],
        out_specs=[pl.BlockSpec((tm, d), lambda i, f: (i, 0)), *casts.out_specs],
        out_shape=[jax.ShapeDtypeStruct((t, d), F32), *casts.out_shapes],
        scratch_shapes=[pltpu.VMEM((tm, d), BF16)],
        compiler_params=_params("parallel", "arbitrary"),
        name="swiglu",
    )(x, gain.reshape(1, d), wg, wu, wd, *casts.views)
    return out, casts.restore(cast)


def _ple_math(x, g_ref, wg_ref, p_ref, wp_ref, fg_ref, final_norm):
    h = _rms_norm(x, g_ref[...]).astype(BF16)
    gate = _sigmoid(_dot(h, wg_ref[...]))
    y = x + gate * _dot(p_ref[...].astype(BF16), wp_ref[...])
    if final_norm:
        y = _rms_norm(y, fg_ref[...])
    return y


def _ple_kernel(x_ref, g_ref, wg_ref, p_ref, wp_ref, fg_ref, o_ref, *, final_norm):
    o_ref[...] = _ple_math(x_ref[...], g_ref, wg_ref, p_ref, wp_ref, fg_ref, final_norm)


def _ple(x, gain, wg, p, layer, wp, final_gain, *, tm, final_norm):
    t, d = x.shape
    dp = p.shape[-1]
    return pl.pallas_call(
        functools.partial(_ple_kernel, final_norm=final_norm),
        grid=(t // tm,),
        in_specs=[pl.BlockSpec((tm, d), lambda i: (i, 0)),
                  pl.BlockSpec((1, d), lambda i: (0, 0)),
                  pl.BlockSpec((d, d), lambda i: (0, 0)),
                  pl.BlockSpec((None, tm, dp), lambda i: (layer, i, 0)),
                  pl.BlockSpec((dp, d), lambda i: (0, 0)),
                  pl.BlockSpec((1, d), lambda i: (0, 0))],
        out_specs=pl.BlockSpec((tm, d), lambda i: (i, 0)),
        out_shape=jax.ShapeDtypeStruct((t, d), F32),
        compiler_params=_params("parallel"),
        name="ple",
    )(x, gain.reshape(1, d), wg, p, wp, final_gain.reshape(1, d))


def _pool_router_kernel(x_ref, g_ref, pw_ref, ps_ref, fg_ref, wr_ref, lt_ref,
                        x1_ref, hg_ref, route_ref, counts_ref, ext_ref, cnt_ref,
                        *, n_experts):
    s = pl.program_id(1)
    ts, d = x_ref.shape
    n_groups = len(POOL_WINDOWS)
    gd = d // n_groups

    @pl.when(s == 0)
    def _():
        ext_ref[0:POOL_HALO, :] = jnp.zeros((POOL_HALO, d), F32)

    @pl.when(s > 0)
    def _():
        ext_ref[0:POOL_HALO, :] = ext_ref[ts:ts + POOL_HALO, :]

    x = x_ref[...]
    h = _rms_norm(x, g_ref[...])
    ext_ref[POOL_HALO:, :] = h

    def window_sum(cols, w):
        acc = ext_ref[:, cols]
        span = 1
        while span < w:
            acc = acc + pltpu.roll(acc, span, 0)
            span *= 2
        return acc[POOL_HALO:, :]

    pos = (s * ts + 1 + lax.broadcasted_iota(jnp.int32, (ts, 1), 0)).astype(F32)
    ys = []
    for gi, w in enumerate(POOL_WINDOWS):
        cols = slice(gi * gd, (gi + 1) * gd)
        mean = window_sum(cols, w) / jnp.minimum(pos, float(w))
        diff = (mean - h[:, cols]).astype(BF16)
        ys.append(_dot(diff, pw_ref[gi]))
    y = jnp.concatenate(ys, axis=1)
    x1 = x + y * ps_ref[...]
    x1_ref[...] = x1

    h2 = _rms_norm(x1, fg_ref[...])
    hg_ref[:, 0:d] = h2

    hi = h2.astype(BF16)
    mid = (h2 - hi.astype(F32)).astype(BF16)
    p_hi = _dot(hi, wr_ref[...])
    p_mid = _dot(mid, wr_ref[...])
    logits = (p_hi + pltpu.roll(p_hi, LANES - n_experts, 1)
              + pltpu.roll(p_hi, LANES - 2 * n_experts, 1)
              + p_mid + pltpu.roll(p_mid, LANES - n_experts, 1))
    lane = lax.broadcasted_iota(jnp.int32, logits.shape, 1)
    valid = lane < n_experts
    logits = jnp.where(valid, logits, -jnp.inf)
    ex = jnp.exp(logits - jnp.max(logits, axis=-1, keepdims=True))
    probs = ex / jnp.sum(ex, axis=-1, keepdims=True)
    gates = jnp.zeros_like(probs)
    rest = jnp.where(valid, probs, -1.0)
    tops = []
    for _ in range(TOP_K):
        m = jnp.max(rest, axis=-1, keepdims=True)
        first = jnp.min(jnp.where(rest == m, lane, LANES), axis=-1, keepdims=True)
        sel = lane == first
        tops.append((m, sel))
        rest = jnp.where(sel, -1.0, rest)
    total = tops[0][0]
    for m, _ in tops[1:]:
        total = total + m
    for m, sel in tops:
        gates = jnp.where(sel, m / total, gates)
    hg_ref[:, d:d + LANES] = gates

    first_step = (pl.program_id(0) == 0) & (s == 0)

    @pl.when(first_step)
    def _():
        cnt_ref[...] = jnp.zeros_like(cnt_ref)

    chosen = tops[0][1]
    for _, sel in tops[1:]:
        chosen = chosen | sel
    chosen = jnp.where(chosen, 1.0, 0.0)
    before = cnt_ref[...] + _dot(lt_ref[...], chosen.astype(BF16))
    cnt_ref[...] = cnt_ref[...] + jnp.sum(chosen, axis=0, keepdims=True)
    counts_ref[...] = cnt_ref[...]
    lane_f = lane.astype(F32)
    route = jnp.zeros((ts, LANES), F32)
    for k, (_, sel) in enumerate(tops):
        expert_k = jnp.sum(jnp.where(sel, lane_f, 0.0), axis=-1, keepdims=True)
        order_k = jnp.sum(jnp.where(sel, before, 0.0), axis=-1, keepdims=True)
        route = jnp.where(lane == k, expert_k, route)
        route = jnp.where(lane == TOP_K + k, order_k, route)
    route_ref[...] = route.T[0:SUBLANES, :].astype(jnp.int32)


def _pool_router(x, gain, pool_w, pool_scale, ffn_gain, w_router, *, batch, ts):
    t, d = x.shape
    s_len = t // batch
    n_s = s_len // ts
    n_groups, gd, _ = pool_w.shape
    n_experts = w_router.shape[1]
    assert 3 * n_experts <= LANES
    w_hi = w_router.astype(BF16)
    r1 = w_router - w_hi.astype(F32)
    w_mid = r1.astype(BF16)
    w_lo = (r1 - w_mid.astype(F32)).astype(BF16)
    wr = jnp.zeros((d, LANES), BF16).at[:, :3 * n_experts].set(
        jnp.concatenate([w_hi, w_mid, w_lo], axis=1))
    idx = jnp.arange(ts)
    lower = (idx[None, :] < idx[:, None]).astype(BF16)
    row = lambda b, s: (b * n_s + s, 0)
    const = lambda b, s: (0, 0)
    return pl.pallas_call(
        functools.partial(_pool_router_kernel, n_experts=n_experts),
        grid=(batch, n_s),
        in_specs=[pl.BlockSpec((ts, d), row),
                  pl.BlockSpec((1, d), const),
                  pl.BlockSpec((n_groups, gd, gd), lambda b, s: (0, 0, 0)),
                  pl.BlockSpec((1, d), const),
                  pl.BlockSpec((1, d), const),
                  pl.BlockSpec((d, LANES), const),
                  pl.BlockSpec((ts, ts), const)],
        out_specs=[pl.BlockSpec((ts, d), row),
                   pl.BlockSpec((ts, d + LANES), row),
                   pl.BlockSpec((SUBLANES, ts), lambda b, s: (0, b * n_s + s)),
                   pl.BlockSpec((1, LANES), const)],
        out_shape=[jax.ShapeDtypeStruct((t, d), F32),
                   jax.ShapeDtypeStruct((t, d + LANES), F32),
                   jax.ShapeDtypeStruct((SUBLANES, t), jnp.int32),
                   jax.ShapeDtypeStruct((1, LANES), F32)],
        scratch_shapes=[pltpu.VMEM((POOL_HALO + ts, d), F32),
                        pltpu.VMEM((1, LANES), F32)],
        compiler_params=_params("arbitrary", "arbitrary"),
        name="pool_router",
    )(x, gain.reshape(1, d), pool_w, pool_scale.reshape(1, d), ffn_gain.reshape(1, d), wr, lower)


def _dispatch_kernel(slots_ref, bounds_ref, hg_ref, xs_ref, zero_ref, sem, zero_sem,
                     *, n_experts, tile):
    i = pl.program_id(0)
    td = hg_ref.shape[0]
    n_tokens = pl.num_programs(0) * td

    for r in range(td):
        for k in range(TOP_K):
            slot = slots_ref[k * n_tokens + i * td + r]
            pltpu.make_async_copy(hg_ref.at[pl.ds(r, 1), :],
                                  xs_ref.at[pl.ds(slot, 1), :], sem).start()

    def zero_copies():
        copies = []

        def block(first, size):
            return pltpu.make_async_copy(zero_ref.at[pl.ds(0, size), :],
                                         xs_ref.at[pl.ds(first, size), :], zero_sem)

        for e in range(n_experts):
            first_unused = bounds_ref[e]
            end = bounds_ref[n_experts + e]
            aligned = (first_unused + SUBLANES - 1) // SUBLANES * SUBLANES
            for r in range(SUBLANES - 1):
                copies.append((first_unused + r < aligned, block(first_unused + r, 1)))
            left = end - aligned
            pos = aligned
            size = tile // 2
            while size >= SUBLANES:
                needed = (left & size) != 0
                copies.append((needed, block(pl.multiple_of(pos, SUBLANES), size)))
                pos = pos + jnp.where(needed, size, 0)
                size //= 2
        total = bounds_ref[2 * n_experts]
        for b in range(n_experts):
            first = pl.multiple_of(total + b * tile, tile)
            copies.append((first < xs_ref.shape[0], block(first, tile)))
        return copies

    @pl.when(i == 0)
    def _():
        zero_ref[...] = jnp.zeros_like(zero_ref)
        for action in ("start", "wait"):
            for needed, copy in zero_copies():
                pl.when(needed)(getattr(copy, action))

    for _ in range(TOP_K):
        pltpu.make_async_copy(hg_ref, xs_ref.at[pl.ds(0, td), :], sem).wait()


def _dispatch(slots_flat, bounds, hg, *, td, n_experts, tile):
    t, width = hg.shape
    n_rows = t * TOP_K + n_experts * tile
    return pl.pallas_call(
        functools.partial(_dispatch_kernel, n_experts=n_experts, tile=tile),
        grid_spec=pltpu.PrefetchScalarGridSpec(
            num_scalar_prefetch=2,
            grid=(t // td,),
            in_specs=[pl.BlockSpec((td, width), lambda i, slots, bounds: (i, 0))],
            out_specs=pl.BlockSpec(memory_space=pl.ANY),
            scratch_shapes=[pltpu.VMEM((tile, width), F32),
                            pltpu.SemaphoreType.DMA,
                            pltpu.SemaphoreType.DMA]),
        out_shape=jax.ShapeDtypeStruct((n_rows, width), F32),
        compiler_params=_params("arbitrary"),
        name="moe_dispatch",
    )(slots_flat, bounds, hg)


def _experts_kernel(te_ref, nt_ref, xs_ref, wg_ref, wu_ref, wd_ref, ys_ref):
    i = pl.program_id(0)
    d = wg_ref.shape[0]

    @pl.when(i < nt_ref[0])
    def _():
        rows = xs_ref[:, 0:d].astype(BF16)
        gates = xs_ref[:, d:d + LANES]
        lane = lax.broadcasted_iota(jnp.int32, gates.shape, 1)
        ge = jnp.sum(jnp.where(lane == te_ref[i], gates, 0.0), axis=-1, keepdims=True)
        gate = _dot(rows, wg_ref[...])
        up = _dot(rows, wu_ref[...])
        act = (ge * (gate * _sigmoid(gate) * up)).astype(BF16)
        ys_ref[...] = _dot(act, wd_ref[...])

    @pl.when(i >= nt_ref[0])
    def _():
        ys_ref[...] = jnp.zeros_like(ys_ref)


def _experts(tile_expert, n_tiles, xs, wg, wu, wd, cast_weights, *, tile):
    n_rows, width = xs.shape
    n_experts, d, ffe = wg.shape
    row_map = lambda i, te, nt: (i, 0)
    w_map = lambda i, te, nt: (te[i], 0, 0)
    grid = (n_rows // tile,)
    casts = _Casts(cast_weights, grid)
    out, *cast = pl.pallas_call(
        casts.wrap(_experts_kernel, n_in=4, n_out=1, n_prefetch=2),
        grid_spec=pltpu.PrefetchScalarGridSpec(
            num_scalar_prefetch=2,
            grid=grid,
            in_specs=[pl.BlockSpec((tile, width), row_map),
                      pl.BlockSpec((None, d, ffe), w_map),
                      pl.BlockSpec((None, d, ffe), w_map),
                      pl.BlockSpec((None, ffe, d), w_map),
                      *casts.in_specs],
            out_specs=[pl.BlockSpec((tile, d), row_map), *casts.out_specs]),
        out_shape=[jax.ShapeDtypeStruct((n_rows, d), F32), *casts.out_shapes],
        compiler_params=_params("arbitrary"),
        name="moe_experts",
    )(tile_expert, n_tiles, xs, wg, wu, wd, *casts.views)
    return out, casts.restore(cast)


def _routing_tables(route, counts, *, n_experts, tile):
    t = route.shape[1]
    counts = counts[0, :n_experts].astype(jnp.int32)
    padded = (counts + tile - 1) // tile * tile
    ends = jnp.cumsum(padded)
    starts = ends - padded
    expert, order = route[:TOP_K], route[TOP_K:2 * TOP_K]
    slots = order
    for e in range(n_experts):
        slots = slots + jnp.where(expert == e, starts[e], 0)
    slots = slots.reshape(-1)
    bounds = jnp.concatenate([starts + counts, ends, ends[-1:]]).astype(jnp.int32)
    n_tiles_max = (t * TOP_K) // tile + n_experts
    first_row = jnp.arange(n_tiles_max, dtype=jnp.int32) * tile
    tile_expert = jnp.minimum(jnp.sum(first_row[:, None] >= ends[None, :], axis=1),
                              n_experts - 1).astype(jnp.int32)
    n_tiles = (ends[-1:] // tile).astype(jnp.int32)
    return slots.astype(jnp.int32), bounds, tile_expert, n_tiles


def _ple_combine_kernel(slots_ref, x_ref, g_ref, wg_ref, p_ref, wp_ref, fg_ref, ys_ref,
                        o_ref, ybuf, sem, *, final_norm):
    i = pl.program_id(0)
    n = pl.num_programs(0)
    tm, d = x_ref.shape
    n_tokens = n * tm
    chunk_cols = d // COMBINE_CHUNKS
    chunk_rows = tm // COMBINE_CHUNKS

    def start_rows(step, buf, first, last):
        for r in range(first, last):
            for k in range(TOP_K):
                slot = slots_ref[k * n_tokens + step * tm + r]
                pltpu.make_async_copy(ys_ref.at[pl.ds(slot, 1), :],
                                      ybuf.at[buf, k, pl.ds(r, 1), :], sem.at[buf]).start()

    def wait_tile(buf):
        for k in range(TOP_K):
            pltpu.make_async_copy(ys_ref.at[pl.ds(0, tm), :], ybuf.at[buf, k],
                                  sem.at[buf]).wait()

    cur = i % 2

    @pl.when(i == 0)
    def _():
        start_rows(i, cur, 0, tm)

    wait_tile(cur)
    x = x_ref[...]
    for k in range(TOP_K):
        x = x + ybuf[cur, k]
    h = _rms_norm(x, g_ref[...]).astype(BF16)
    pb = p_ref[...].astype(BF16)
    nxt = jnp.minimum(i + 1, n - 1)
    for c in range(COMBINE_CHUNKS):
        cols = slice(c * chunk_cols, (c + 1) * chunk_cols)
        gate = _sigmoid(_dot(h, wg_ref[:, cols]))
        o_ref[:, cols] = x[:, cols] + gate * _dot(pb, wp_ref[:, cols])
        start_rows(nxt, 1 - cur, c * chunk_rows, (c + 1) * chunk_rows)
    if final_norm:
        o_ref[...] = _rms_norm(o_ref[...], fg_ref[...])

    @pl.when(i == n - 1)
    def _():
        wait_tile(1 - cur)


def _ple_combine(slots_flat, x, gain, wg, p, layer, wp, final_gain, ys, *, tm, final_norm):
    t, d = x.shape
    dp = p.shape[-1]
    row = lambda i, slots: (i, 0)
    const = lambda i, slots: (0, 0)
    return pl.pallas_call(
        functools.partial(_ple_combine_kernel, final_norm=final_norm),
        grid_spec=pltpu.PrefetchScalarGridSpec(
            num_scalar_prefetch=1,
            grid=(t // tm,),
            in_specs=[pl.BlockSpec((tm, d), row),
                      pl.BlockSpec((1, d), const),
                      pl.BlockSpec((d, d), const),
                      pl.BlockSpec((None, tm, dp), lambda i, slots: (layer, i, 0)),
                      pl.BlockSpec((dp, d), const),
                      pl.BlockSpec((1, d), const),
                      pl.BlockSpec(memory_space=pl.ANY)],
            out_specs=pl.BlockSpec((tm, d), row),
            scratch_shapes=[pltpu.VMEM((2, TOP_K, tm, d), F32),
                            pltpu.SemaphoreType.DMA((2,))]),
        out_shape=jax.ShapeDtypeStruct((t, d), F32),
        compiler_params=_params("arbitrary"),
        name="ple_combine",
    )(slots_flat, x, gain.reshape(1, d), wg, p, wp, final_gain.reshape(1, d), ys)


def kernel(x, p, final_norm_gain, lb_table, mix_norm_even, w_in_even, hgrn_norm_gain, conv_w, w_out_even, ffn_norm_even, w_gate_dense, w_up_dense, w_down_dense, mix_norm_odd, pool_w, pool_scale, ffn_norm_odd, w_router, w_gate_exp, w_up_exp, w_down_exp, ple_norm, ple_gate_w, ple_proj):
    batch, s_len, d = x.shape
    t = batch * s_len
    depth = p.shape[0]
    assert depth == 2, "the weight-cast schedule below is written for one layer pair"
    n_experts = w_router.shape[-1]
    tile = EXPERT_ROW_TILE
    xs = x.reshape(t, d)

    pp = p.reshape(depth, t, -1)

    u, _ = _norm_matmul(xs, mix_norm_even[0], w_in_even[0].astype(BF16), [], tm=1024, tn=1792)
    a, b, (w_out, w_gate, w_up, w_down) = _hgrn_conv(
        u, lb_table, hgrn_norm_gain[0], conv_w[0],
        [(w_out_even, 0), (w_gate_dense, 0), (w_up_dense, 0), (w_down_dense, 0)],
        batch=batch, rows=1024, layer=0)
    xs, (ple_gate0, ple_proj0, pool_wb) = _out_proj(
        xs, a.reshape(t, -1), b.reshape(t, -1), w_out,
        [(ple_gate_w, 0), (ple_proj, 0), (pool_w, 0)], tm=512, tn=d)
    xs, (wg_exp, wu_exp, wd_exp) = _swiglu(
        xs, ffn_norm_even[0], w_gate, w_up, w_down,
        [(w_gate_exp, 0), (w_up_exp, 0), (w_down_exp, 0)], tm=1024, tf=256)
    xs = _ple(xs, ple_norm[0], ple_gate0, pp, 0, ple_proj0, final_norm_gain,
              tm=512, final_norm=False)

    xs, hg, route, counts = _pool_router(
        xs, mix_norm_odd[0], pool_wb, pool_scale[0], ffn_norm_odd[0], w_router[0],
        batch=batch, ts=512)
    slots, bounds, tile_expert, n_tiles = _routing_tables(
        route, counts, n_experts=n_experts, tile=tile)
    rows = _dispatch(slots, bounds, hg, td=512, n_experts=n_experts, tile=tile)
    ys, (ple_gate1, ple_proj1) = _experts(
        tile_expert, n_tiles, rows, wg_exp, wu_exp, wd_exp, [(ple_gate_w, 1), (ple_proj, 1)],
        tile=tile)
    xs = _ple_combine(slots, xs, ple_norm[1], ple_gate1, pp, 1, ple_proj1,
                      final_norm_gain, ys, tm=256, final_norm=True)
    return xs.reshape(batch, s_len, d)
```

```python
import functools
import math

import jax
import jax.numpy as jnp
from jax import lax
from jax.experimental import pallas as pl
from jax.experimental.pallas import tpu as pltpu

F32 = jnp.float32
BF16 = jnp.bfloat16

EPS = 1e-6
HGRN_HEADS = 8
HGRN_CHUNK = 64
CONV_WIDTH = 3
POOL_WINDOWS = (2, 4, 8, 16)
POOL_HALO = 16
TOP_K = 2
EXPERT_ROW_TILE = 512
COMBINE_CHUNKS = 4
LANES = 128
SUBLANES = 8
BF16_SUBLANES = 16
VMEM_LIMIT_BYTES = 56 * 1024 * 1024


def _params(*semantics):
    return pltpu.CompilerParams(dimension_semantics=semantics,
                                vmem_limit_bytes=VMEM_LIMIT_BYTES)


def _rms_norm(x, gain):
    ms = jnp.mean(x * x, axis=-1, keepdims=True)
    return x * lax.rsqrt(ms + EPS) * gain


def _sigmoid(x):
    return 1.0 / (1.0 + jnp.exp(-x))


def _dot(a, b):
    return jnp.dot(a, b, preferred_element_type=F32)


def _dot_nt(a, b):
    return lax.dot_general(a, b, (((1,), (1,)), ((), ())), preferred_element_type=F32)


class _Casts:
    def __init__(self, weights, grid):
        n_steps = math.prod(grid)
        self.shapes = [stacked.shape[1:] for stacked, _ in weights]
        self.views, self.in_specs, self.out_specs, self.out_shapes = [], [], [], []
        for stacked, layer in weights:
            cols = stacked.shape[-1]
            rows = stacked[0].size // cols
            block_rows = next(br for br in range(BF16_SUBLANES, rows + 1, BF16_SUBLANES)
                              if rows % br == 0 and rows // br <= n_steps)
            n_blocks = rows // block_rows
            self.views.append(stacked.reshape(stacked.shape[0] * rows, cols))
            for specs, first in ((self.in_specs, layer * n_blocks), (self.out_specs, 0)):
                specs.append(pl.BlockSpec(
                    (block_rows, cols),
                    functools.partial(self._index_map, grid, n_blocks, first)))
            self.out_shapes.append(jax.ShapeDtypeStruct((rows, cols), BF16))

    @staticmethod
    def _index_map(grid, n_blocks, first, *args):
        step = 0
        for size, idx in zip(grid, args):
            step = step * size + idx
        return first + jnp.minimum(step, n_blocks - 1), 0

    def __len__(self):
        return len(self.views)

    def wrap(self, body, n_in, n_out, n_prefetch=0):
        n = len(self)

        def kernel(*refs):
            ins_end = n_prefetch + n_in
            outs_start = ins_end + n
            outs_end = outs_start + n_out
            body(*refs[:ins_end], *refs[outs_start:outs_end], *refs[outs_end + n:])
            for src, dst in zip(refs[ins_end:outs_start], refs[outs_end:outs_end + n]):
                dst[...] = src[...].astype(BF16)

        return kernel

    def restore(self, outs):
        return [o.reshape(shape) for o, shape in zip(outs, self.shapes)]


def _norm_matmul_kernel(x_ref, g_ref, w_ref, o_ref, h_ref):
    @pl.when(pl.program_id(1) == 0)
    def _():
        h_ref[...] = _rms_norm(x_ref[...], g_ref[...]).astype(BF16)

    o_ref[...] = _dot(h_ref[...], w_ref[...])


def _norm_matmul(x, gain, w, cast_weights, *, tm, tn):
    t, d = x.shape
    n = w.shape[1]
    grid = (t // tm, n // tn)
    casts = _Casts(cast_weights, grid)
    out, *cast = pl.pallas_call(
        casts.wrap(_norm_matmul_kernel, n_in=3, n_out=1),
        grid=grid,
        in_specs=[pl.BlockSpec((tm, d), lambda i, j: (i, 0)),
                  pl.BlockSpec((1, d), lambda i, j: (0, 0)),
                  pl.BlockSpec((d, tn), lambda i, j: (0, j)),
                  *casts.in_specs],
        out_specs=[pl.BlockSpec((tm, tn), lambda i, j: (i, j)), *casts.out_specs],
        out_shape=[jax.ShapeDtypeStruct((t, n), F32), *casts.out_shapes],
        scratch_shapes=[pltpu.VMEM((tm, d), BF16)],
        compiler_params=_params("parallel", "arbitrary"),
        name="norm_matmul",
    )(x, gain.reshape(1, d), w, *casts.views)
    return out, casts.restore(cast)


def _hgrn_conv_kernel(tri_ref, q_ref, f_ref, i_ref, g_ref, gb_ref, gc_ref, vc_ref,
                      lbt_ref, gain_ref, cw_ref, a_ref, b_ref, *, rows, layer):
    s_len, dh = q_ref.shape
    ch = HGRN_CHUNK
    ct = tri_ref.shape[0]

    lbt = lbt_ref[...]
    e = jnp.exp(lbt - jnp.max(lbt, axis=0, keepdims=True))
    lb = (jnp.sum(e[0:layer + 1, :], axis=0, keepdims=True)
          / jnp.sum(e, axis=0, keepdims=True))
    gain = gain_ref[...]

    tri = tri_ref[...]
    row = lax.broadcasted_iota(jnp.int32, (ct, ct), 0)
    col = lax.broadcasted_iota(jnp.int32, (ct, ct), 1)
    causal = (row >= col) & (row // ch == col // ch)
    n_chunks = rows // ch

    def body(t, st):
        r0 = pl.multiple_of(t * rows, rows)
        qr = q_ref[pl.ds(r0, rows), :]
        q = qr * _sigmoid(qr)
        f = lb + (1.0 - lb) * _sigmoid(f_ref[pl.ds(r0, rows), :])
        k = 1.0 - f
        v = i_ref[pl.ds(r0, rows), :].astype(BF16)
        lf = jnp.log(f)
        hi = lf.astype(BF16)
        r1 = lf - hi.astype(F32)
        mid = r1.astype(BF16)
        lo = (r1 - mid.astype(F32)).astype(BF16)
        parts = jnp.concatenate([hi, mid, lo], axis=1)
        cums = []
        for j in range(rows // ct):
            c3 = _dot(tri, parts[j * ct:(j + 1) * ct])
            cums.append(c3[:, :dh] + c3[:, dh:2 * dh] + c3[:, 2 * dh:])
        cum = jnp.concatenate(cums, axis=0)

        cum3 = cum.reshape(n_chunks, ch, dh)
        ref = cum3[:, ch // 2:ch // 2 + 1, :]
        last = cum3[:, ch - 1:ch, :]
        q3 = q.reshape(n_chunks, ch, dh)
        k3 = k.reshape(n_chunks, ch, dh)
        qe = (q3 * jnp.exp(cum3 - ref)).astype(BF16).reshape(rows, dh)
        ke = (k3 * jnp.exp(ref - cum3)).astype(BF16).reshape(rows, dh)
        qc = (q3 * jnp.exp(cum3)).astype(BF16).reshape(rows, dh)
        kd = (k3 * jnp.exp(last - cum3)).astype(BF16).reshape(rows, dh)
        decay = jnp.exp(last)

        chunks = [slice(c * ch, (c + 1) * ch) for c in range(n_chunks)]
        kv_t = [lax.dot_general(v[cs], kd[cs], (((0,), (0,)), ((), ())),
                                preferred_element_type=F32) for cs in chunks]
        states = []
        for c in range(n_chunks):
            states.append(st.astype(BF16))
            st = decay[c] * st + kv_t[c]
        o_inter = jnp.concatenate(
            [_dot_nt(qc[cs], s_c) for cs, s_c in zip(chunks, states)], axis=0)

        o_intra = []
        for j in range(rows // ct):
            sl = slice(j * ct, (j + 1) * ct)
            scores = jnp.where(causal, _dot_nt(qe[sl], ke[sl]), 0.0)
            o_intra.append(_dot(scores.astype(BF16), v[sl]))
        o = jnp.concatenate(o_intra, axis=0) + o_inter
        o = o * lax.rsqrt(jnp.mean(o * o, axis=-1, keepdims=True) + EPS)
        gr = g_ref[pl.ds(r0, rows), :]
        a_ref[pl.ds(r0, rows), :] = (o * gain * (gr * _sigmoid(gr))).astype(BF16)
        return st

    lax.fori_loop(0, s_len // rows, body, jnp.zeros((dh, dh), F32))

    tt = gc_ref[...] * vc_ref[...]
    ridx = lax.broadcasted_iota(jnp.int32, tt.shape, 0)
    cw = cw_ref[...]
    conv = tt * cw[CONV_WIDTH - 1:CONV_WIDTH, :]
    for back in range(1, CONV_WIDTH):
        shifted = jnp.where(ridx >= back, pltpu.roll(tt, back, 0), 0.0)
        conv = conv + shifted * cw[CONV_WIDTH - 1 - back:CONV_WIDTH - back, :]
    b_ref[...] = (gb_ref[...] * conv).astype(BF16)


def _hgrn_conv(u, lb_table, hgrn_gain, conv_w, cast_weights, *, batch, rows, layer):
    t, n_in = u.shape
    s_len = t // batch
    d_hgrn = hgrn_gain.shape[0]
    dh = d_hgrn // HGRN_HEADS
    d_conv = conv_w.shape[1]
    nh = HGRN_HEADS
    assert d_conv // dh == nh and n_in == 4 * d_hgrn + 3 * d_conv
    u3 = u.reshape(batch, s_len, n_in)

    idx = jnp.arange(2 * HGRN_CHUNK)
    tri = ((idx[:, None] >= idx[None, :])
           & (idx[:, None] // HGRN_CHUNK == idx[None, :] // HGRN_CHUNK)).astype(BF16)

    def col(off):
        return pl.BlockSpec((None, s_len, dh), lambda b, h, off=off: (b, 0, off + h))

    n_lb = lb_table.shape[0]
    out_spec = pl.BlockSpec((None, s_len, dh), lambda b, h: (b, 0, h))
    grid = (batch, nh)
    casts = _Casts(cast_weights, grid)
    a, b, *cast = pl.pallas_call(
        casts.wrap(functools.partial(_hgrn_conv_kernel, rows=rows, layer=layer),
                   n_in=11, n_out=2),
        grid=grid,
        in_specs=[pl.BlockSpec(tri.shape, lambda b, h: (0, 0)),
                  col(0), col(nh), col(2 * nh), col(3 * nh),
                  col(4 * nh), col(5 * nh), col(6 * nh),
                  pl.BlockSpec((n_lb, dh), lambda b, h: (0, h)),
                  pl.BlockSpec((1, dh), lambda b, h: (0, h)),
                  pl.BlockSpec((CONV_WIDTH, dh), lambda b, h: (0, h)),
                  *casts.in_specs],
        out_specs=[out_spec, out_spec, *casts.out_specs],
        out_shape=[jax.ShapeDtypeStruct((batch, s_len, d_hgrn), BF16),
                   jax.ShapeDtypeStruct((batch, s_len, d_conv), BF16),
                   *casts.out_shapes],
        compiler_params=_params("parallel", "parallel"),
        name="hgrn_conv",
    )(tri, u3, u3, u3, u3, u3, u3, u3, lb_table, hgrn_gain.reshape(1, d_hgrn), conv_w,
      *casts.views)
    return a, b, casts.restore(cast)


def _out_proj_kernel(x_ref, a_ref, b_ref, wa_ref, wb_ref, o_ref):
    o_ref[...] = x_ref[...] + _dot(a_ref[...], wa_ref[...]) + _dot(b_ref[...], wb_ref[...])


def _out_proj(x, a, b, w, cast_weights, *, tm, tn):
    t, d = x.shape
    ka, kb = a.shape[1], b.shape[1]
    assert ka == kb
    grid = (t // tm, d // tn)
    casts = _Casts(cast_weights, grid)
    out, *cast = pl.pallas_call(
        casts.wrap(_out_proj_kernel, n_in=5, n_out=1),
        grid=grid,
        in_specs=[pl.BlockSpec((tm, tn), lambda i, j: (i, j)),
                  pl.BlockSpec((tm, ka), lambda i, j: (i, 0)),
                  pl.BlockSpec((tm, kb), lambda i, j: (i, 0)),
                  pl.BlockSpec((ka, tn), lambda i, j: (0, j)),
                  pl.BlockSpec((kb, tn), lambda i, j: (1, j)),
                  *casts.in_specs],
        out_specs=[pl.BlockSpec((tm, tn), lambda i, j: (i, j)), *casts.out_specs],
        out_shape=[jax.ShapeDtypeStruct((t, d), F32), *casts.out_shapes],
        compiler_params=_params("parallel", "arbitrary"),
        name="out_proj",
    )(x, a, b, w, w, *casts.views)
    return out, casts.restore(cast)


def _swiglu_kernel(x_ref, g_ref, wg_ref, wu_ref, wd_ref, o_ref, h_ref):
    @pl.when(pl.program_id(1) == 0)
    def _():
        x = x_ref[...]
        h_ref[...] = _rms_norm(x, g_ref[...]).astype(BF16)
        o_ref[...] = x

    h = h_ref[...]
    gate = _dot(h, wg_ref[...])
    up = _dot(h, wu_ref[...])
    act = (gate * _sigmoid(gate) * up).astype(BF16)
    o_ref[...] += _dot(act, wd_ref[...])


def _swiglu(x, gain, wg, wu, wd, cast_weights, *, tm, tf):
    t, d = x.shape
    ff = wg.shape[1]
    grid = (t // tm, ff // tf)
    casts = _Casts(cast_weights, grid)
    out, *cast = pl.pallas_call(
        casts.wrap(_swiglu_kernel, n_in=5, n_out=1),
        grid=grid,
        in_specs=[pl.BlockSpec((tm, d), lambda i, f: (i, 0)),
                  pl.BlockSpec((1, d), lambda i, f: (0, 0)),
                  pl.BlockSpec((d, tf), lambda i, f: (0, f)),
                  pl.BlockSpec((d, tf), lambda i, f: (0, f)),
                  pl.BlockSpec((tf, d), lambda i, f: (f, 0)),
                  *casts.in_specs],
        out_specs=[pl.BlockSpec((tm, d), lambda i, f: (i, 0)), *casts.out_specs],
        out_shape=[jax.ShapeDtypeStruct((t, d), F32), *casts.out_shapes],
        scratch_shapes=[pltpu.VMEM((tm, d), BF16)],
        compiler_params=_params("parallel", "arbitrary"),
        name="swiglu",
    )(x, gain.reshape(1, d), wg, wu, wd, *casts.views)
    return out, casts.restore(cast)


def _ple_math(x, g_ref, wg_ref, p_ref, wp_ref, fg_ref, final_norm):
    h = _rms_norm(x, g_ref[...]).astype(BF16)
    gate = _sigmoid(_dot(h, wg_ref[...]))
    y = x + gate * _dot(p_ref[...].astype(BF16), wp_ref[...])
    if final_norm:
        y = _rms_norm(y, fg_ref[...])
    return y


def _ple_kernel(x_ref, g_ref, wg_ref, p_ref, wp_ref, fg_ref, o_ref, *, final_norm):
    o_ref[...] = _ple_math(x_ref[...], g_ref, wg_ref, p_ref, wp_ref, fg_ref, final_norm)


def _ple(x, gain, wg, p, layer, wp, final_gain, *, tm, final_norm):
    t, d = x.shape
    dp = p.shape[-1]
    return pl.pallas_call(
        functools.partial(_ple_kernel, final_norm=final_norm),
        grid=(t // tm,),
        in_specs=[pl.BlockSpec((tm, d), lambda i: (i, 0)),
                  pl.BlockSpec((1, d), lambda i: (0, 0)),
                  pl.BlockSpec((d, d), lambda i: (0, 0)),
                  pl.BlockSpec((None, tm, dp), lambda i: (layer, i, 0)),
                  pl.BlockSpec((dp, d), lambda i: (0, 0)),
                  pl.BlockSpec((1, d), lambda i: (0, 0))],
        out_specs=pl.BlockSpec((tm, d), lambda i: (i, 0)),
        out_shape=jax.ShapeDtypeStruct((t, d), F32),
        compiler_params=_params("parallel"),
        name="ple",
    )(x, gain.reshape(1, d), wg, p, wp, final_gain.reshape(1, d))


def _pool_router_kernel(x_ref, g_ref, pw_ref, ps_ref, fg_ref, wr_ref, lt_ref,
                        x1_ref, hg_ref, route_ref, counts_ref, ext_ref, cnt_ref,
                        *, n_experts):
    s = pl.program_id(1)
    ts, d = x_ref.shape
    n_groups = len(POOL_WINDOWS)
    gd = d // n_groups

    @pl.when(s == 0)
    def _():
        ext_ref[0:POOL_HALO, :] = jnp.zeros((POOL_HALO, d), F32)

    @pl.when(s > 0)
    def _():
        ext_ref[0:POOL_HALO, :] = ext_ref[ts:ts + POOL_HALO, :]

    x = x_ref[...]
    h = _rms_norm(x, g_ref[...])
    ext_ref[POOL_HALO:, :] = h

    def window_sum(cols, w):
        acc = ext_ref[:, cols]
        span = 1
        while span < w:
            acc = acc + pltpu.roll(acc, span, 0)
            span *= 2
        return acc[POOL_HALO:, :]

    pos = (s * ts + 1 + lax.broadcasted_iota(jnp.int32, (ts, 1), 0)).astype(F32)
    ys = []
    for gi, w in enumerate(POOL_WINDOWS):
        cols = slice(gi * gd, (gi + 1) * gd)
        mean = window_sum(cols, w) / jnp.minimum(pos, float(w))
        diff = (mean - h[:, cols]).astype(BF16)
        ys.append(_dot(diff, pw_ref[gi]))
    y = jnp.concatenate(ys, axis=1)
    x1 = x + y * ps_ref[...]
    x1_ref[...] = x1

    h2 = _rms_norm(x1, fg_ref[...])
    hg_ref[:, 0:d] = h2

    hi = h2.astype(BF16)
    mid = (h2 - hi.astype(F32)).astype(BF16)
    p_hi = _dot(hi, wr_ref[...])
    p_mid = _dot(mid, wr_ref[...])
    logits = (p_hi + pltpu.roll(p_hi, LANES - n_experts, 1)
              + pltpu.roll(p_hi, LANES - 2 * n_experts, 1)
              + p_mid + pltpu.roll(p_mid, LANES - n_experts, 1))
    lane = lax.broadcasted_iota(jnp.int32, logits.shape, 1)
    valid = lane < n_experts
    logits = jnp.where(valid, logits, -jnp.inf)
    ex = jnp.exp(logits - jnp.max(logits, axis=-1, keepdims=True))
    probs = ex / jnp.sum(ex, axis=-1, keepdims=True)
    gates = jnp.zeros_like(probs)
    rest = jnp.where(valid, probs, -1.0)
    tops = []
    for _ in range(TOP_K):
        m = jnp.max(rest, axis=-1, keepdims=True)
        first = jnp.min(jnp.where(rest == m, lane, LANES), axis=-1, keepdims=True)
        sel = lane == first
        tops.append((m, sel))
        rest = jnp.where(sel, -1.0, rest)
    total = tops[0][0]
    for m, _ in tops[1:]:
        total = total + m
    for m, sel in tops:
        gates = jnp.where(sel, m / total, gates)
    hg_ref[:, d:d + LANES] = gates

    first_step = (pl.program_id(0) == 0) & (s == 0)

    @pl.when(first_step)
    def _():
        cnt_ref[...] = jnp.zeros_like(cnt_ref)

    chosen = tops[0][1]
    for _, sel in tops[1:]:
        chosen = chosen | sel
    chosen = jnp.where(chosen, 1.0, 0.0)
    before = cnt_ref[...] + _dot(lt_ref[...], chosen.astype(BF16))
    cnt_ref[...] = cnt_ref[...] + jnp.sum(chosen, axis=0, keepdims=True)
    counts_ref[...] = cnt_ref[...]
    lane_f = lane.astype(F32)
    route = jnp.zeros((ts, LANES), F32)
    for k, (_, sel) in enumerate(tops):
        expert_k = jnp.sum(jnp.where(sel, lane_f, 0.0), axis=-1, keepdims=True)
        order_k = jnp.sum(jnp.where(sel, before, 0.0), axis=-1, keepdims=True)
        route = jnp.where(lane == k, expert_k, route)
        route = jnp.where(lane == TOP_K + k, order_k, route)
    route_ref[...] = route.T[0:SUBLANES, :].astype(jnp.int32)


def _pool_router(x, gain, pool_w, pool_scale, ffn_gain, w_router, *, batch, ts):
    t, d = x.shape
    s_len = t // batch
    n_s = s_len // ts
    n_groups, gd, _ = pool_w.shape
    n_experts = w_router.shape[1]
    assert 3 * n_experts <= LANES
    w_hi = w_router.astype(BF16)
    r1 = w_router - w_hi.astype(F32)
    w_mid = r1.astype(BF16)
    w_lo = (r1 - w_mid.astype(F32)).astype(BF16)
    wr = jnp.zeros((d, LANES), BF16).at[:, :3 * n_experts].set(
        jnp.concatenate([w_hi, w_mid, w_lo], axis=1))
    idx = jnp.arange(ts)
    lower = (idx[None, :] < idx[:, None]).astype(BF16)
    row = lambda b, s: (b * n_s + s, 0)
    const = lambda b, s: (0, 0)
    return pl.pallas_call(
        functools.partial(_pool_router_kernel, n_experts=n_experts),
        grid=(batch, n_s),
        in_specs=[pl.BlockSpec((ts, d), row),
                  pl.BlockSpec((1, d), const),
                  pl.BlockSpec((n_groups, gd, gd), lambda b, s: (0, 0, 0)),
                  pl.BlockSpec((1, d), const),
                  pl.BlockSpec((1, d), const),
                  pl.BlockSpec((d, LANES), const),
                  pl.BlockSpec((ts, ts), const)],
        out_specs=[pl.BlockSpec((ts, d), row),
                   pl.BlockSpec((ts, d + LANES), row),
                   pl.BlockSpec((SUBLANES, ts), lambda b, s: (0, b * n_s + s)),
                   pl.BlockSpec((1, LANES), const)],
        out_shape=[jax.ShapeDtypeStruct((t, d), F32),
                   jax.ShapeDtypeStruct((t, d + LANES), F32),
                   jax.ShapeDtypeStruct((SUBLANES, t), jnp.int32),
                   jax.ShapeDtypeStruct((1, LANES), F32)],
        scratch_shapes=[pltpu.VMEM((POOL_HALO + ts, d), F32),
                        pltpu.VMEM((1, LANES), F32)],
        compiler_params=_params("arbitrary", "arbitrary"),
        name="pool_router",
    )(x, gain.reshape(1, d), pool_w, pool_scale.reshape(1, d), ffn_gain.reshape(1, d), wr, lower)


def _dispatch_kernel(slots_ref, bounds_ref, hg_ref, xs_ref, zero_ref, sem, zero_sem,
                     *, n_experts, tile):
    i = pl.program_id(0)
    td = hg_ref.shape[0]
    n_tokens = pl.num_programs(0) * td

    for r in range(td):
        for k in range(TOP_K):
            slot = slots_ref[k * n_tokens + i * td + r]
            pltpu.make_async_copy(hg_ref.at[pl.ds(r, 1), :],
                                  xs_ref.at[pl.ds(slot, 1), :], sem).start()

    def zero_copies():
        copies = []

        def block(first, size):
            return pltpu.make_async_copy(zero_ref.at[pl.ds(0, size), :],
                                         xs_ref.at[pl.ds(first, size), :], zero_sem)

        for e in range(n_experts):
            first_unused = bounds_ref[e]
            end = bounds_ref[n_experts + e]
            aligned = (first_unused + SUBLANES - 1) // SUBLANES * SUBLANES
            for r in range(SUBLANES - 1):
                copies.append((first_unused + r < aligned, block(first_unused + r, 1)))
            left = end - aligned
            pos = aligned
            size = tile // 2
            while size >= SUBLANES:
                needed = (left & size) != 0
                copies.append((needed, block(pl.multiple_of(pos, SUBLANES), size)))
                pos = pos + jnp.where(needed, size, 0)
                size //= 2
        total = bounds_ref[2 * n_experts]
        for b in range(n_experts):
            first = pl.multiple_of(total + b * tile, tile)
            copies.append((first < xs_ref.shape[0], block(first, tile)))
        return copies

    @pl.when(i == 0)
    def _():
        zero_ref[...] = jnp.zeros_like(zero_ref)
        for action in ("start", "wait"):
            for needed, copy in zero_copies():
                pl.when(needed)(getattr(copy, action))

    for _ in range(TOP_K):
        pltpu.make_async_copy(hg_ref, xs_ref.at[pl.ds(0, td), :], sem).wait()


def _dispatch(slots_flat, bounds, hg, *, td, n_experts, tile):
    t, width = hg.shape
    n_rows = t * TOP_K + n_experts * tile
    return pl.pallas_call(
        functools.partial(_dispatch_kernel, n_experts=n_experts, tile=tile),
        grid_spec=pltpu.PrefetchScalarGridSpec(
            num_scalar_prefetch=2,
            grid=(t // td,),
            in_specs=[pl.BlockSpec((td, width), lambda i, slots, bounds: (i, 0))],
            out_specs=pl.BlockSpec(memory_space=pl.ANY),
            scratch_shapes=[pltpu.VMEM((tile, width), F32),
                            pltpu.SemaphoreType.DMA,
                            pltpu.SemaphoreType.DMA]),
        out_shape=jax.ShapeDtypeStruct((n_rows, width), F32),
        compiler_params=_params("arbitrary"),
        name="moe_dispatch",
    )(slots_flat, bounds, hg)


def _experts_kernel(te_ref, nt_ref, xs_ref, wg_ref, wu_ref, wd_ref, ys_ref):
    i = pl.program_id(0)
    d = wg_ref.shape[0]

    @pl.when(i < nt_ref[0])
    def _():
        rows = xs_ref[:, 0:d].astype(BF16)
        gates = xs_ref[:, d:d + LANES]
        lane = lax.broadcasted_iota(jnp.int32, gates.shape, 1)
        ge = jnp.sum(jnp.where(lane == te_ref[i], gates, 0.0), axis=-1, keepdims=True)
        gate = _dot(rows, wg_ref[...])
        up = _dot(rows, wu_ref[...])
        act = (ge * (gate * _sigmoid(gate) * up)).astype(BF16)
        ys_ref[...] = _dot(act, wd_ref[...])

    @pl.when(i >= nt_ref[0])
    def _():
        ys_ref[...] = jnp.zeros_like(ys_ref)


def _experts(tile_expert, n_tiles, xs, wg, wu, wd, cast_weights, *, tile):
    n_rows, width = xs.shape
    n_experts, d, ffe = wg.shape
    row_map = lambda i, te, nt: (i, 0)
    w_map = lambda i, te, nt: (te[i], 0, 0)
    grid = (n_rows // tile,)
    casts = _Casts(cast_weights, grid)
    out, *cast = pl.pallas_call(
        casts.wrap(_experts_kernel, n_in=4, n_out=1, n_prefetch=2),
        grid_spec=pltpu.PrefetchScalarGridSpec(
            num_scalar_prefetch=2,
            grid=grid,
            in_specs=[pl.BlockSpec((tile, width), row_map),
                      pl.BlockSpec((None, d, ffe), w_map),
                      pl.BlockSpec((None, d, ffe), w_map),
                      pl.BlockSpec((None, ffe, d), w_map),
                      *casts.in_specs],
            out_specs=[pl.BlockSpec((tile, d), row_map), *casts.out_specs]),
        out_shape=[jax.ShapeDtypeStruct((n_rows, d), F32), *casts.out_shapes],
        compiler_params=_params("arbitrary"),
        name="moe_experts",
    )(tile_expert, n_tiles, xs, wg, wu, wd, *casts.views)
    return out, casts.restore(cast)


def _routing_tables(route, counts, *, n_experts, tile):
    t = route.shape[1]
    counts = counts[0, :n_experts].astype(jnp.int32)
    padded = (counts + tile - 1) // tile * tile
    ends = jnp.cumsum(padded)
    starts = ends - padded
    expert, order = route[:TOP_K], route[TOP_K:2 * TOP_K]
    slots = order
    for e in range(n_experts):
        slots = slots + jnp.where(expert == e, starts[e], 0)
    slots = slots.reshape(-1)
    bounds = jnp.concatenate([starts + counts, ends, ends[-1:]]).astype(jnp.int32)
    n_tiles_max = (t * TOP_K) // tile + n_experts
    first_row = jnp.arange(n_tiles_max, dtype=jnp.int32) * tile
    tile_expert = jnp.minimum(jnp.sum(first_row[:, None] >= ends[None, :], axis=1),
                              n_experts - 1).astype(jnp.int32)
    n_tiles = (ends[-1:] // tile).astype(jnp.int32)
    return slots.astype(jnp.int32), bounds, tile_expert, n_tiles


def _ple_combine_kernel(slots_ref, x_ref, g_ref, wg_ref, p_ref, wp_ref, fg_ref, ys_ref,
                        o_ref, ybuf, sem, *, final_norm):
    i = pl.program_id(0)
    n = pl.num_programs(0)
    tm, d = x_ref.shape
    n_tokens = n * tm
    chunk_cols = d // COMBINE_CHUNKS
    chunk_rows = tm // COMBINE_CHUNKS

    def start_rows(step, buf, first, last):
        for r in range(first, last):
            for k in range(TOP_K):
                slot = slots_ref[k * n_tokens + step * tm + r]
                pltpu.make_async_copy(ys_ref.at[pl.ds(slot, 1), :],
                                      ybuf.at[buf, k, pl.ds(r, 1), :], sem.at[buf]).start()

    def wait_tile(buf):
        for k in range(TOP_K):
            pltpu.make_async_copy(ys_ref.at[pl.ds(0, tm), :], ybuf.at[buf, k],
                                  sem.at[buf]).wait()

    cur = i % 2

    @pl.when(i == 0)
    def _():
        start_rows(i, cur, 0, tm)

    wait_tile(cur)
    x = x_ref[...]
    for k in range(TOP_K):
        x = x + ybuf[cur, k]
    h = _rms_norm(x, g_ref[...]).astype(BF16)
    pb = p_ref[...].astype(BF16)
    nxt = jnp.minimum(i + 1, n - 1)
    for c in range(COMBINE_CHUNKS):
        cols = slice(c * chunk_cols, (c + 1) * chunk_cols)
        gate = _sigmoid(_dot(h, wg_ref[:, cols]))
        o_ref[:, cols] = x[:, cols] + gate * _dot(pb, wp_ref[:, cols])
        start_rows(nxt, 1 - cur, c * chunk_rows, (c + 1) * chunk_rows)
    if final_norm:
        o_ref[...] = _rms_norm(o_ref[...], fg_ref[...])

    @pl.when(i == n - 1)
    def _():
        wait_tile(1 - cur)


def _ple_combine(slots_flat, x, gain, wg, p, layer, wp, final_gain, ys, *, tm, final_norm):
    t, d = x.shape
    dp = p.shape[-1]
    row = lambda i, slots: (i, 0)
    const = lambda i, slots: (0, 0)
    return pl.pallas_call(
        functools.partial(_ple_combine_kernel, final_norm=final_norm),
        grid_spec=pltpu.PrefetchScalarGridSpec(
            num_scalar_prefetch=1,
            grid=(t // tm,),
            in_specs=[pl.BlockSpec((tm, d), row),
                      pl.BlockSpec((1, d), const),
                      pl.BlockSpec((d, d), const),
                      pl.BlockSpec((None, tm, dp), lambda i, slots: (layer, i, 0)),
                      pl.BlockSpec((dp, d), const),
                      pl.BlockSpec((1, d), const),
                      pl.BlockSpec(memory_space=pl.ANY)],
            out_specs=pl.BlockSpec((tm, d), row),
            scratch_shapes=[pltpu.VMEM((2, TOP_K, tm, d), F32),
                            pltpu.SemaphoreType.DMA((2,))]),
        out_shape=jax.ShapeDtypeStruct((t, d), F32),
        compiler_params=_params("arbitrary"),
        name="ple_combine",
    )(slots_flat, x, gain.reshape(1, d), wg, p, wp, final_gain.reshape(1, d), ys)


def kernel(x, p, final_norm_gain, lb_table, mix_norm_even, w_in_even, hgrn_norm_gain, conv_w, w_out_even, ffn_norm_even, w_gate_dense, w_up_dense, w_down_dense, mix_norm_odd, pool_w, pool_scale, ffn_norm_odd, w_router, w_gate_exp, w_up_exp, w_down_exp, ple_norm, ple_gate_w, ple_proj):
    batch, s_len, d = x.shape
    t = batch * s_len
    depth = p.shape[0]
    assert depth == 2, "the weight-cast schedule below is written for one layer pair"
    n_experts = w_router.shape[-1]
    tile = EXPERT_ROW_TILE
    xs = x.reshape(t, d)

    pp = p.reshape(depth, t, -1)

    u, _ = _norm_matmul(xs, mix_norm_even[0], w_in_even[0].astype(BF16), [], tm=1024, tn=1792)
    a, b, (w_out, w_gate, w_up, w_down) = _hgrn_conv(
        u, lb_table, hgrn_norm_gain[0], conv_w[0],
        [(w_out_even, 0), (w_gate_dense, 0), (w_up_dense, 0), (w_down_dense, 0)],
        batch=batch, rows=1024, layer=0)
    xs, (ple_gate0, ple_proj0, pool_wb) = _out_proj(
        xs, a.reshape(t, -1), b.reshape(t, -1), w_out,
        [(ple_gate_w, 0), (ple_proj, 0), (pool_w, 0)], tm=512, tn=d)
    xs, (wg_exp, wu_exp, wd_exp) = _swiglu(
        xs, ffn_norm_even[0], w_gate, w_up, w_down,
        [(w_gate_exp, 0), (w_up_exp, 0), (w_down_exp, 0)], tm=1024, tf=256)
    xs = _ple(xs, ple_norm[0], ple_gate0, pp, 0, ple_proj0, final_norm_gain,
              tm=512, final_norm=False)

    xs, hg, route, counts = _pool_router(
        xs, mix_norm_odd[0], pool_wb, pool_scale[0], ffn_norm_odd[0], w_router[0],
        batch=batch, ts=512)
    slots, bounds, tile_expert, n_tiles = _routing_tables(
        route, counts, n_experts=n_experts, tile=tile)
    rows = _dispatch(slots, bounds, hg, td=512, n_experts=n_experts, tile=tile)
    ys, (ple_gate1, ple_proj1) = _experts(
        tile_expert, n_tiles, rows, wg_exp, wu_exp, wd_exp, [(ple_gate_w, 1), (ple_proj, 1)],
        tile=tile)
    xs = _ple_combine(slots, xs, ple_norm[1], ple_gate1, pp, 1, ple_proj1,
                      final_norm_gain, ys, tm=256, final_norm=True)
    return xs.reshape(batch, s_len, d)
```

```python
import functools
import math

import jax
import jax.numpy as jnp
from jax import lax
from jax.experimental import pallas as pl
from jax.experimental.pallas import tpu as pltpu

F32 = jnp.float32
BF16 = jnp.bfloat16

EPS = 1e-6
HGRN_HEADS = 8
HGRN_CHUNK = 64
CONV_WIDTH = 3
POOL_WINDOWS = (2, 4, 8, 16)
POOL_HALO = 16
TOP_K = 2
EXPERT_ROW_TILE = 512
COMBINE_CHUNKS = 8
LANES = 128
SUBLANES = 8
BF16_SUBLANES = 16
VMEM_LIMIT_BYTES = 56 * 1024 * 1024


def _params(*semantics):
    return pltpu.CompilerParams(dimension_semantics=semantics,
                                vmem_limit_bytes=VMEM_LIMIT_BYTES)


def _rms_norm(x, gain):
    ms = jnp.mean(x * x, axis=-1, keepdims=True)
    return x * lax.rsqrt(ms + EPS) * gain


def _sigmoid(x):
    return 1.0 / (1.0 + jnp.exp(-x))


def _dot(a, b):
    return jnp.dot(a, b, preferred_element_type=F32)


def _dot_nt(a, b):
    return lax.dot_general(a, b, (((1,), (1,)), ((), ())), preferred_element_type=F32)


class _Casts:
    def __init__(self, weights, grid):
        n_steps = math.prod(grid)
        self.shapes = [stacked.shape[1:] for stacked, _ in weights]
        self.views, self.in_specs, self.out_specs, self.out_shapes = [], [], [], []
        for stacked, layer in weights:
            cols = stacked.shape[-1]
            rows = stacked[0].size // cols
            block_rows = next(br for br in range(BF16_SUBLANES, rows + 1, BF16_SUBLANES)
                              if rows % br == 0 and rows // br <= n_steps)
            n_blocks = rows // block_rows
            self.views.append(stacked.reshape(stacked.shape[0] * rows, cols))
            for specs, first in ((self.in_specs, layer * n_blocks), (self.out_specs, 0)):
                specs.append(pl.BlockSpec(
                    (block_rows, cols),
                    functools.partial(self._index_map, grid, n_blocks, first)))
            self.out_shapes.append(jax.ShapeDtypeStruct((rows, cols), BF16))

    @staticmethod
    def _index_map(grid, n_blocks, first, *args):
        step = 0
        for size, idx in zip(grid, args):
            step = step * size + idx
        return first + jnp.minimum(step, n_blocks - 1), 0

    def __len__(self):
        return len(self.views)

    def wrap(self, body, n_in, n_out, n_prefetch=0):
        n = len(self)

        def kernel(*refs):
            ins_end = n_prefetch + n_in
            outs_start = ins_end + n
            outs_end = outs_start + n_out
            body(*refs[:ins_end], *refs[outs_start:outs_end], *refs[outs_end + n:])
            for src, dst in zip(refs[ins_end:outs_start], refs[outs_end:outs_end + n]):
                dst[...] = src[...].astype(BF16)

        return kernel

    def restore(self, outs):
        return [o.reshape(shape) for o, shape in zip(outs, self.shapes)]


def _norm_matmul_kernel(x_ref, g_ref, w_ref, o_ref, h_ref):
    @pl.when(pl.program_id(1) == 0)
    def _():
        h_ref[...] = _rms_norm(x_ref[...], g_ref[...]).astype(BF16)

    o_ref[...] = _dot(h_ref[...], w_ref[...])


def _norm_matmul(x, gain, w, cast_weights, *, tm, tn):
    t, d = x.shape
    n = w.shape[1]
    grid = (t // tm, n // tn)
    casts = _Casts(cast_weights, grid)
    out, *cast = pl.pallas_call(
        casts.wrap(_norm_matmul_kernel, n_in=3, n_out=1),
        grid=grid,
        in_specs=[pl.BlockSpec((tm, d), lambda i, j: (i, 0)),
                  pl.BlockSpec((1, d), lambda i, j: (0, 0)),
                  pl.BlockSpec((d, tn), lambda i, j: (0, j)),
                  *casts.in_specs],
        out_specs=[pl.BlockSpec((tm, tn), lambda i, j: (i, j)), *casts.out_specs],
        out_shape=[jax.ShapeDtypeStruct((t, n), F32), *casts.out_shapes],
        scratch_shapes=[pltpu.VMEM((tm, d), BF16)],
        compiler_params=_params("parallel", "arbitrary"),
        name="norm_matmul",
    )(x, gain.reshape(1, d), w, *casts.views)
    return out, casts.restore(cast)


def _hgrn_conv_kernel(tri_ref, q_ref, f_ref, i_ref, g_ref, gb_ref, gc_ref, vc_ref,
                      lbt_ref, gain_ref, cw_ref, a_ref, b_ref, *, rows, layer):
    s_len, dh = q_ref.shape
    ch = HGRN_CHUNK
    ct = tri_ref.shape[0]

    lbt = lbt_ref[...]
    e = jnp.exp(lbt - jnp.max(lbt, axis=0, keepdims=True))
    lb = (jnp.sum(e[0:layer + 1, :], axis=0, keepdims=True)
          / jnp.sum(e, axis=0, keepdims=True))
    gain = gain_ref[...]

    tri = tri_ref[...]
    row = lax.broadcasted_iota(jnp.int32, (ct, ct), 0)
    col = lax.broadcasted_iota(jnp.int32, (ct, ct), 1)
    causal = (row >= col) & (row // ch == col // ch)
    n_chunks = rows // ch

    def body(t, st):
        r0 = pl.multiple_of(t * rows, rows)
        qr = q_ref[pl.ds(r0, rows), :]
        q = qr * _sigmoid(qr)
        f = lb + (1.0 - lb) * _sigmoid(f_ref[pl.ds(r0, rows), :])
        k = 1.0 - f
        v = i_ref[pl.ds(r0, rows), :].astype(BF16)
        lf = jnp.log(f)
        hi = lf.astype(BF16)
        r1 = lf - hi.astype(F32)
        mid = r1.astype(BF16)
        lo = (r1 - mid.astype(F32)).astype(BF16)
        parts = jnp.concatenate([hi, mid, lo], axis=1)
        cums = []
        for j in range(rows // ct):
            c3 = _dot(tri, parts[j * ct:(j + 1) * ct])
            cums.append(c3[:, :dh] + c3[:, dh:2 * dh] + c3[:, 2 * dh:])
        cum = jnp.concatenate(cums, axis=0)

        cum3 = cum.reshape(n_chunks, ch, dh)
        ref = cum3[:, ch // 2:ch // 2 + 1, :]
        last = cum3[:, ch - 1:ch, :]
        q3 = q.reshape(n_chunks, ch, dh)
        k3 = k.reshape(n_chunks, ch, dh)
        qe = (q3 * jnp.exp(cum3 - ref)).astype(BF16).reshape(rows, dh)
        ke = (k3 * jnp.exp(ref - cum3)).astype(BF16).reshape(rows, dh)
        qc = (q3 * jnp.exp(cum3)).astype(BF16).reshape(rows, dh)
        kd = (k3 * jnp.exp(last - cum3)).astype(BF16).reshape(rows, dh)
        decay = jnp.exp(last)

        chunks = [slice(c * ch, (c + 1) * ch) for c in range(n_chunks)]
        kv_t = [lax.dot_general(v[cs], kd[cs], (((0,), (0,)), ((), ())),
                                preferred_element_type=F32) for cs in chunks]
        states = []
        for c in range(n_chunks):
            states.append(st.astype(BF16))
            st = decay[c] * st + kv_t[c]
        o_inter = jnp.concatenate(
            [_dot_nt(qc[cs], s_c) for cs, s_c in zip(chunks, states)], axis=0)

        o_intra = []
        for j in range(rows // ct):
            sl = slice(j * ct, (j + 1) * ct)
            scores = jnp.where(causal, _dot_nt(qe[sl], ke[sl]), 0.0)
            o_intra.append(_dot(scores.astype(BF16), v[sl]))
        o = jnp.concatenate(o_intra, axis=0) + o_inter
        o = o * lax.rsqrt(jnp.mean(o * o, axis=-1, keepdims=True) + EPS)
        gr = g_ref[pl.ds(r0, rows), :]
        a_ref[pl.ds(r0, rows), :] = (o * gain * (gr * _sigmoid(gr))).astype(BF16)
        return st

    lax.fori_loop(0, s_len // rows, body, jnp.zeros((dh, dh), F32))

    tt = gc_ref[...] * vc_ref[...]
    ridx = lax.broadcasted_iota(jnp.int32, tt.shape, 0)
    cw = cw_ref[...]
    conv = tt * cw[CONV_WIDTH - 1:CONV_WIDTH, :]
    for back in range(1, CONV_WIDTH):
        shifted = jnp.where(ridx >= back, pltpu.roll(tt, back, 0), 0.0)
        conv = conv + shifted * cw[CONV_WIDTH - 1 - back:CONV_WIDTH - back, :]
    b_ref[...] = (gb_ref[...] * conv).astype(BF16)


def _hgrn_conv(u, lb_table, hgrn_gain, conv_w, cast_weights, *, batch, rows, layer):
    t, n_in = u.shape
    s_len = t // batch
    d_hgrn = hgrn_gain.shape[0]
    dh = d_hgrn // HGRN_HEADS
    d_conv = conv_w.shape[1]
    nh = HGRN_HEADS
    assert d_conv // dh == nh and n_in == 4 * d_hgrn + 3 * d_conv
    u3 = u.reshape(batch, s_len, n_in)

    idx = jnp.arange(2 * HGRN_CHUNK)
    tri = ((idx[:, None] >= idx[None, :])
           & (idx[:, None] // HGRN_CHUNK == idx[None, :] // HGRN_CHUNK)).astype(BF16)

    def col(off):
        return pl.BlockSpec((None, s_len, dh), lambda b, h, off=off: (b, 0, off + h))

    n_lb = lb_table.shape[0]
    out_spec = pl.BlockSpec((None, s_len, dh), lambda b, h: (b, 0, h))
    grid = (batch, nh)
    casts = _Casts(cast_weights, grid)
    a, b, *cast = pl.pallas_call(
        casts.wrap(functools.partial(_hgrn_conv_kernel, rows=rows, layer=layer),
                   n_in=11, n_out=2),
        grid=grid,
        in_specs=[pl.BlockSpec(tri.shape, lambda b, h: (0, 0)),
                  col(0), col(nh), col(2 * nh), col(3 * nh),
                  col(4 * nh), col(5 * nh), col(6 * nh),
                  pl.BlockSpec((n_lb, dh), lambda b, h: (0, h)),
                  pl.BlockSpec((1, dh), lambda b, h: (0, h)),
                  pl.BlockSpec((CONV_WIDTH, dh), lambda b, h: (0, h)),
                  *casts.in_specs],
        out_specs=[out_spec, out_spec, *casts.out_specs],
        out_shape=[jax.ShapeDtypeStruct((batch, s_len, d_hgrn), BF16),
                   jax.ShapeDtypeStruct((batch, s_len, d_conv), BF16),
                   *casts.out_shapes],
        compiler_params=_params("parallel", "parallel"),
        name="hgrn_conv",
    )(tri, u3, u3, u3, u3, u3, u3, u3, lb_table, hgrn_gain.reshape(1, d_hgrn), conv_w,
      *casts.views)
    return a, b, casts.restore(cast)


def _out_proj_kernel(x_ref, a_ref, b_ref, wa_ref, wb_ref, o_ref):
    o_ref[...] = x_ref[...] + _dot(a_ref[...], wa_ref[...]) + _dot(b_ref[...], wb_ref[...])


def _out_proj(x, a, b, w, cast_weights, *, tm, tn):
    t, d = x.shape
    ka, kb = a.shape[1], b.shape[1]
    assert ka == kb
    grid = (t // tm, d // tn)
    casts = _Casts(cast_weights, grid)
    out, *cast = pl.pallas_call(
        casts.wrap(_out_proj_kernel, n_in=5, n_out=1),
        grid=grid,
        in_specs=[pl.BlockSpec((tm, tn), lambda i, j: (i, j)),
                  pl.BlockSpec((tm, ka), lambda i, j: (i, 0)),
                  pl.BlockSpec((tm, kb), lambda i, j: (i, 0)),
                  pl.BlockSpec((ka, tn), lambda i, j: (0, j)),
                  pl.BlockSpec((kb, tn), lambda i, j: (1, j)),
                  *casts.in_specs],
        out_specs=[pl.BlockSpec((tm, tn), lambda i, j: (i, j)), *casts.out_specs],
        out_shape=[jax.ShapeDtypeStruct((t, d), F32), *casts.out_shapes],
        compiler_params=_params("parallel", "arbitrary"),
        name="out_proj",
    )(x, a, b, w, w, *casts.views)
    return out, casts.restore(cast)


def _swiglu_kernel(x_ref, g_ref, wg_ref, wu_ref, wd_ref, o_ref, h_ref):
    @pl.when(pl.program_id(1) == 0)
    def _():
        x = x_ref[...]
        h_ref[...] = _rms_norm(x, g_ref[...]).astype(BF16)
        o_ref[...] = x

    h = h_ref[...]
    gate = _dot(h, wg_ref[...])
    up = _dot(h, wu_ref[...])
    act = (gate * _sigmoid(gate) * up).astype(BF16)
    o_ref[...] += _dot(act, wd_ref[...])


def _swiglu(x, gain, wg, wu, wd, cast_weights, *, tm, tf):
    t, d = x.shape
    ff = wg.shape[1]
    grid = (t // tm, ff // tf)
    casts = _Casts(cast_weights, grid)
    out, *cast = pl.pallas_call(
        casts.wrap(_swiglu_kernel, n_in=5, n_out=1),
        grid=grid,
        in_specs=[pl.BlockSpec((tm, d), lambda i, f: (i, 0)),
                  pl.BlockSpec((1, d), lambda i, f: (0, 0)),
                  pl.BlockSpec((d, tf), lambda i, f: (0, f)),
                  pl.BlockSpec((d, tf), lambda i, f: (0, f)),
                  pl.BlockSpec((tf, d), lambda i, f: (f, 0)),
                  *casts.in_specs],
        out_specs=[pl.BlockSpec((tm, d), lambda i, f: (i, 0)), *casts.out_specs],
        out_shape=[jax.ShapeDtypeStruct((t, d), F32), *casts.out_shapes],
        scratch_shapes=[pltpu.VMEM((tm, d), BF16)],
        compiler_params=_params("parallel", "arbitrary"),
        name="swiglu",
    )(x, gain.reshape(1, d), wg, wu, wd, *casts.views)
    return out, casts.restore(cast)


def _ple_math(x, g_ref, wg_ref, p_ref, wp_ref, fg_ref, final_norm):
    h = _rms_norm(x, g_ref[...]).astype(BF16)
    gate = _sigmoid(_dot(h, wg_ref[...]))
    y = x + gate * _dot(p_ref[...].astype(BF16), wp_ref[...])
    if final_norm:
        y = _rms_norm(y, fg_ref[...])
    return y


def _ple_kernel(x_ref, g_ref, wg_ref, p_ref, wp_ref, fg_ref, o_ref, *, final_norm):
    o_ref[...] = _ple_math(x_ref[...], g_ref, wg_ref, p_ref, wp_ref, fg_ref, final_norm)


def _ple(x, gain, wg, p, layer, wp, final_gain, *, tm, final_norm):
    t, d = x.shape
    dp = p.shape[-1]
    return pl.pallas_call(
        functools.partial(_ple_kernel, final_norm=final_norm),
        grid=(t // tm,),
        in_specs=[pl.BlockSpec((tm, d), lambda i: (i, 0)),
                  pl.BlockSpec((1, d), lambda i: (0, 0)),
                  pl.BlockSpec((d, d), lambda i: (0, 0)),
                  pl.BlockSpec((None, tm, dp), lambda i: (layer, i, 0)),
                  pl.BlockSpec((dp, d), lambda i: (0, 0)),
                  pl.BlockSpec((1, d), lambda i: (0, 0))],
        out_specs=pl.BlockSpec((tm, d), lambda i: (i, 0)),
        out_shape=jax.ShapeDtypeStruct((t, d), F32),
        compiler_params=_params("parallel"),
        name="ple",
    )(x, gain.reshape(1, d), wg, p, wp, final_gain.reshape(1, d))


def _pool_router_kernel(x_ref, g_ref, pw_ref, ps_ref, fg_ref, wr_ref, lt_ref,
                        x1_ref, hg_ref, route_ref, counts_ref, ext_ref, cnt_ref,
                        *, n_experts):
    s = pl.program_id(1)
    ts, d = x_ref.shape
    n_groups = len(POOL_WINDOWS)
    gd = d // n_groups

    @pl.when(s == 0)
    def _():
        ext_ref[0:POOL_HALO, :] = jnp.zeros((POOL_HALO, d), F32)

    @pl.when(s > 0)
    def _():
        ext_ref[0:POOL_HALO, :] = ext_ref[ts:ts + POOL_HALO, :]

    x = x_ref[...]
    h = _rms_norm(x, g_ref[...])
    ext_ref[POOL_HALO:, :] = h

    def window_sum(cols, w):
        acc = ext_ref[:, cols]
        span = 1
        while span < w:
            acc = acc + pltpu.roll(acc, span, 0)
            span *= 2
        return acc[POOL_HALO:, :]

    pos = (s * ts + 1 + lax.broadcasted_iota(jnp.int32, (ts, 1), 0)).astype(F32)
    ys = []
    for gi, w in enumerate(POOL_WINDOWS):
        cols = slice(gi * gd, (gi + 1) * gd)
        mean = window_sum(cols, w) / jnp.minimum(pos, float(w))
        diff = (mean - h[:, cols]).astype(BF16)
        ys.append(_dot(diff, pw_ref[gi]))
    y = jnp.concatenate(ys, axis=1)
    x1 = x + y * ps_ref[...]
    x1_ref[...] = x1

    h2 = _rms_norm(x1, fg_ref[...])
    hg_ref[:, 0:d] = h2

    hi = h2.astype(BF16)
    mid = (h2 - hi.astype(F32)).astype(BF16)
    p_hi = _dot(hi, wr_ref[...])
    p_mid = _dot(mid, wr_ref[...])
    logits = (p_hi + pltpu.roll(p_hi, LANES - n_experts, 1)
              + pltpu.roll(p_hi, LANES - 2 * n_experts, 1)
              + p_mid + pltpu.roll(p_mid, LANES - n_experts, 1))
    lane = lax.broadcasted_iota(jnp.int32, logits.shape, 1)
    valid = lane < n_experts
    logits = jnp.where(valid, logits, -jnp.inf)
    ex = jnp.exp(logits - jnp.max(logits, axis=-1, keepdims=True))
    probs = ex / jnp.sum(ex, axis=-1, keepdims=True)
    gates = jnp.zeros_like(probs)
    rest = jnp.where(valid, probs, -1.0)
    tops = []
    for _ in range(TOP_K):
        m = jnp.max(rest, axis=-1, keepdims=True)
        first = jnp.min(jnp.where(rest == m, lane, LANES), axis=-1, keepdims=True)
        sel = lane == first
        tops.append((m, sel))
        rest = jnp.where(sel, -1.0, rest)
    total = tops[0][0]
    for m, _ in tops[1:]:
        total = total + m
    for m, sel in tops:
        gates = jnp.where(sel, m / total, gates)
    hg_ref[:, d:d + LANES] = gates

    first_step = (pl.program_id(0) == 0) & (s == 0)

    @pl.when(first_step)
    def _():
        cnt_ref[...] = jnp.zeros_like(cnt_ref)

    chosen = tops[0][1]
    for _, sel in tops[1:]:
        chosen = chosen | sel
    chosen = jnp.where(chosen, 1.0, 0.0)
    before = cnt_ref[...] + _dot(lt_ref[...], chosen.astype(BF16))
    cnt_ref[...] = cnt_ref[...] + jnp.sum(chosen, axis=0, keepdims=True)
    counts_ref[...] = cnt_ref[...]
    lane_f = lane.astype(F32)
    route = jnp.zeros((ts, LANES), F32)
    for k, (_, sel) in enumerate(tops):
        expert_k = jnp.sum(jnp.where(sel, lane_f, 0.0), axis=-1, keepdims=True)
        order_k = jnp.sum(jnp.where(sel, before, 0.0), axis=-1, keepdims=True)
        route = jnp.where(lane == k, expert_k, route)
        route = jnp.where(lane == TOP_K + k, order_k, route)
    route_ref[...] = route.T[0:SUBLANES, :].astype(jnp.int32)


def _pool_router(x, gain, pool_w, pool_scale, ffn_gain, w_router, *, batch, ts):
    t, d = x.shape
    s_len = t // batch
    n_s = s_len // ts
    n_groups, gd, _ = pool_w.shape
    n_experts = w_router.shape[1]
    assert 3 * n_experts <= LANES
    w_hi = w_router.astype(BF16)
    r1 = w_router - w_hi.astype(F32)
    w_mid = r1.astype(BF16)
    w_lo = (r1 - w_mid.astype(F32)).astype(BF16)
    wr = jnp.zeros((d, LANES), BF16).at[:, :3 * n_experts].set(
        jnp.concatenate([w_hi, w_mid, w_lo], axis=1))
    idx = jnp.arange(ts)
    lower = (idx[None, :] < idx[:, None]).astype(BF16)
    row = lambda b, s: (b * n_s + s, 0)
    const = lambda b, s: (0, 0)
    return pl.pallas_call(
        functools.partial(_pool_router_kernel, n_experts=n_experts),
        grid=(batch, n_s),
        in_specs=[pl.BlockSpec((ts, d), row),
                  pl.BlockSpec((1, d), const),
                  pl.BlockSpec((n_groups, gd, gd), lambda b, s: (0, 0, 0)),
                  pl.BlockSpec((1, d), const),
                  pl.BlockSpec((1, d), const),
                  pl.BlockSpec((d, LANES), const),
                  pl.BlockSpec((ts, ts), const)],
        out_specs=[pl.BlockSpec((ts, d), row),
                   pl.BlockSpec((ts, d + LANES), row),
                   pl.BlockSpec((SUBLANES, ts), lambda b, s: (0, b * n_s + s)),
                   pl.BlockSpec((1, LANES), const)],
        out_shape=[jax.ShapeDtypeStruct((t, d), F32),
                   jax.ShapeDtypeStruct((t, d + LANES), F32),
                   jax.ShapeDtypeStruct((SUBLANES, t), jnp.int32),
                   jax.ShapeDtypeStruct((1, LANES), F32)],
        scratch_shapes=[pltpu.VMEM((POOL_HALO + ts, d), F32),
                        pltpu.VMEM((1, LANES), F32)],
        compiler_params=_params("arbitrary", "arbitrary"),
        name="pool_router",
    )(x, gain.reshape(1, d), pool_w, pool_scale.reshape(1, d), ffn_gain.reshape(1, d), wr, lower)


def _dispatch_kernel(slots_ref, bounds_ref, hg_ref, xs_ref, zero_ref, sem, zero_sem,
                     *, n_experts, tile):
    i = pl.program_id(0)
    td = hg_ref.shape[0]
    n_tokens = pl.num_programs(0) * td

    for r in range(td):
        for k in range(TOP_K):
            slot = slots_ref[k * n_tokens + i * td + r]
            pltpu.make_async_copy(hg_ref.at[pl.ds(r, 1), :],
                                  xs_ref.at[pl.ds(slot, 1), :], sem).start(priority=k % 2)

    def zero_copies():
        copies = []

        def block(first, size):
            return pltpu.make_async_copy(zero_ref.at[pl.ds(0, size), :],
                                         xs_ref.at[pl.ds(first, size), :], zero_sem)

        for e in range(n_experts):
            first_unused = bounds_ref[e]
            end = bounds_ref[n_experts + e]
            aligned = (first_unused + SUBLANES - 1) // SUBLANES * SUBLANES
            for r in range(SUBLANES - 1):
                copies.append((first_unused + r < aligned, block(first_unused + r, 1)))
            left = end - aligned
            pos = aligned
            size = tile // 2
            while size >= SUBLANES:
                needed = (left & size) != 0
                copies.append((needed, block(pl.multiple_of(pos, SUBLANES), size)))
                pos = pos + jnp.where(needed, size, 0)
                size //= 2
        total = bounds_ref[2 * n_experts]
        for b in range(n_experts):
            first = pl.multiple_of(total + b * tile, tile)
            copies.append((first < xs_ref.shape[0], block(first, tile)))
        return copies

    @pl.when(i == 0)
    def _():
        zero_ref[...] = jnp.zeros_like(zero_ref)
        for action in ("start", "wait"):
            for needed, copy in zero_copies():
                pl.when(needed)(getattr(copy, action))

    for _ in range(TOP_K):
        pltpu.make_async_copy(hg_ref, xs_ref.at[pl.ds(0, td), :], sem).wait()


def _dispatch(slots_flat, bounds, hg, *, td, n_experts, tile):
    t, width = hg.shape
    n_rows = t * TOP_K + n_experts * tile
    return pl.pallas_call(
        functools.partial(_dispatch_kernel, n_experts=n_experts, tile=tile),
        grid_spec=pltpu.PrefetchScalarGridSpec(
            num_scalar_prefetch=2,
            grid=(t // td,),
            in_specs=[pl.BlockSpec((td, width), lambda i, slots, bounds: (i, 0))],
            out_specs=pl.BlockSpec(memory_space=pl.ANY),
            scratch_shapes=[pltpu.VMEM((tile, width), F32),
                            pltpu.SemaphoreType.DMA,
                            pltpu.SemaphoreType.DMA]),
        out_shape=jax.ShapeDtypeStruct((n_rows, width), F32),
        compiler_params=_params("arbitrary"),
        name="moe_dispatch",
    )(slots_flat, bounds, hg)


def _experts_kernel(te_ref, nt_ref, xs_ref, wg_ref, wu_ref, wd_ref, ys_ref):
    i = pl.program_id(0)
    d = wg_ref.shape[0]

    @pl.when(i < nt_ref[0])
    def _():
        rows = xs_ref[:, 0:d].astype(BF16)
        gates = xs_ref[:, d:d + LANES]
        lane = lax.broadcasted_iota(jnp.int32, gates.shape, 1)
        ge = jnp.sum(jnp.where(lane == te_ref[i], gates, 0.0), axis=-1, keepdims=True)
        gate = _dot(rows, wg_ref[...])
        up = _dot(rows, wu_ref[...])
        act = (ge * (gate * _sigmoid(gate) * up)).astype(BF16)
        ys_ref[...] = _dot(act, wd_ref[...])

    @pl.when(i >= nt_ref[0])
    def _():
        ys_ref[...] = jnp.zeros_like(ys_ref)


def _experts(tile_expert, n_tiles, xs, wg, wu, wd, cast_weights, *, tile):
    n_rows, width = xs.shape
    n_experts, d, ffe = wg.shape
    row_map = lambda i, te, nt: (i, 0)
    w_map = lambda i, te, nt: (te[i], 0, 0)
    grid = (n_rows // tile,)
    casts = _Casts(cast_weights, grid)
    out, *cast = pl.pallas_call(
        casts.wrap(_experts_kernel, n_in=4, n_out=1, n_prefetch=2),
        grid_spec=pltpu.PrefetchScalarGridSpec(
            num_scalar_prefetch=2,
            grid=grid,
            in_specs=[pl.BlockSpec((tile, width), row_map),
                      pl.BlockSpec((None, d, ffe), w_map),
                      pl.BlockSpec((None, d, ffe), w_map),
                      pl.BlockSpec((None, ffe, d), w_map),
                      *casts.in_specs],
            out_specs=[pl.BlockSpec((tile, d), row_map), *casts.out_specs]),
        out_shape=[jax.ShapeDtypeStruct((n_rows, d), F32), *casts.out_shapes],
        compiler_params=_params("arbitrary"),
        name="moe_experts",
    )(tile_expert, n_tiles, xs, wg, wu, wd, *casts.views)
    return out, casts.restore(cast)


def _routing_tables(route, counts, *, n_experts, tile):
    t = route.shape[1]
    counts = counts[0, :n_experts].astype(jnp.int32)
    padded = (counts + tile - 1) // tile * tile
    ends = jnp.cumsum(padded)
    starts = ends - padded
    expert, order = route[:TOP_K], route[TOP_K:2 * TOP_K]
    slots = order
    for e in range(n_experts):
        slots = slots + jnp.where(expert == e, starts[e], 0)
    slots = slots.reshape(-1)
    bounds = jnp.concatenate([starts + counts, ends, ends[-1:]]).astype(jnp.int32)
    n_tiles_max = (t * TOP_K) // tile + n_experts
    first_row = jnp.arange(n_tiles_max, dtype=jnp.int32) * tile
    tile_expert = jnp.minimum(jnp.sum(first_row[:, None] >= ends[None, :], axis=1),
                              n_experts - 1).astype(jnp.int32)
    n_tiles = (ends[-1:] // tile).astype(jnp.int32)
    return slots.astype(jnp.int32), bounds, tile_expert, n_tiles


def _ple_combine_kernel(slots_ref, x_ref, g_ref, wg_ref, p_ref, wp_ref, fg_ref, ys_ref,
                        o_ref, ybuf, sem, *, final_norm):
    i = pl.program_id(0)
    n = pl.num_programs(0)
    tm, d = x_ref.shape
    n_tokens = n * tm
    chunk_cols = d // COMBINE_CHUNKS
    chunk_rows = tm // COMBINE_CHUNKS

    def start_rows(step, buf, first, last):
        for r in range(first, last):
            for k in range(TOP_K):
                slot = slots_ref[k * n_tokens + step * tm + r]
                pltpu.make_async_copy(ys_ref.at[pl.ds(slot, 1), :],
                                      ybuf.at[buf, k, pl.ds(r, 1), :],
                                      sem.at[buf]).start(priority=k % 2)

    def wait_tile(buf):
        for k in range(TOP_K):
            pltpu.make_async_copy(ys_ref.at[pl.ds(0, tm), :], ybuf.at[buf, k],
                                  sem.at[buf]).wait()

    cur = i % 2

    @pl.when(i == 0)
    def _():
        start_rows(i, cur, 0, tm)

    wait_tile(cur)
    x = x_ref[...]
    for k in range(TOP_K):
        x = x + ybuf[cur, k]
    h = _rms_norm(x, g_ref[...]).astype(BF16)
    pb = p_ref[...].astype(BF16)
    nxt = jnp.minimum(i + 1, n - 1)
    for c in range(COMBINE_CHUNKS):
        cols = slice(c * chunk_cols, (c + 1) * chunk_cols)
        gate = _sigmoid(_dot(h, wg_ref[:, cols]))
        o_ref[:, cols] = x[:, cols] + gate * _dot(pb, wp_ref[:, cols])
        start_rows(nxt, 1 - cur, c * chunk_rows, (c + 1) * chunk_rows)
    if final_norm:
        o_ref[...] = _rms_norm(o_ref[...], fg_ref[...])

    @pl.when(i == n - 1)
    def _():
        wait_tile(1 - cur)


def _ple_combine(slots_flat, x, gain, wg, p, layer, wp, final_gain, ys, *, tm, final_norm):
    t, d = x.shape
    dp = p.shape[-1]
    row = lambda i, slots: (i, 0)
    const = lambda i, slots: (0, 0)
    return pl.pallas_call(
        functools.partial(_ple_combine_kernel, final_norm=final_norm),
        grid_spec=pltpu.PrefetchScalarGridSpec(
            num_scalar_prefetch=1,
            grid=(t // tm,),
            in_specs=[pl.BlockSpec((tm, d), row),
                      pl.BlockSpec((1, d), const),
                      pl.BlockSpec((d, d), const),
                      pl.BlockSpec((None, tm, dp), lambda i, slots: (layer, i, 0)),
                      pl.BlockSpec((dp, d), const),
                      pl.BlockSpec((1, d), const),
                      pl.BlockSpec(memory_space=pl.ANY)],
            out_specs=pl.BlockSpec((tm, d), row),
            scratch_shapes=[pltpu.VMEM((2, TOP_K, tm, d), F32),
                            pltpu.SemaphoreType.DMA((2,))]),
        out_shape=jax.ShapeDtypeStruct((t, d), F32),
        compiler_params=_params("arbitrary"),
        name="ple_combine",
    )(slots_flat, x, gain.reshape(1, d), wg, p, wp, final_gain.reshape(1, d), ys)


def kernel(x, p, final_norm_gain, lb_table, mix_norm_even, w_in_even, hgrn_norm_gain, conv_w, w_out_even, ffn_norm_even, w_gate_dense, w_up_dense, w_down_dense, mix_norm_odd, pool_w, pool_scale, ffn_norm_odd, w_router, w_gate_exp, w_up_exp, w_down_exp, ple_norm, ple_gate_w, ple_proj):
    batch, s_len, d = x.shape
    t = batch * s_len
    depth = p.shape[0]
    assert depth == 2, "the weight-cast schedule below is written for one layer pair"
    n_experts = w_router.shape[-1]
    tile = EXPERT_ROW_TILE
    xs = x.reshape(t, d)

    pp = p.reshape(depth, t, -1)

    u, _ = _norm_matmul(xs, mix_norm_even[0], w_in_even[0].astype(BF16), [], tm=1024, tn=1792)
    a, b, (w_out, w_gate, w_up, w_down) = _hgrn_conv(
        u, lb_table, hgrn_norm_gain[0], conv_w[0],
        [(w_out_even, 0), (w_gate_dense, 0), (w_up_dense, 0), (w_down_dense, 0)],
        batch=batch, rows=1024, layer=0)
    xs, (ple_gate0, ple_proj0, pool_wb) = _out_proj(
        xs, a.reshape(t, -1), b.reshape(t, -1), w_out,
        [(ple_gate_w, 0), (ple_proj, 0), (pool_w, 0)], tm=512, tn=d)
    xs, (wg_exp, wu_exp, wd_exp) = _swiglu(
        xs, ffn_norm_even[0], w_gate, w_up, w_down,
        [(w_gate_exp, 0), (w_up_exp, 0), (w_down_exp, 0)], tm=1024, tf=256)
    xs = _ple(xs, ple_norm[0], ple_gate0, pp, 0, ple_proj0, final_norm_gain,
              tm=512, final_norm=False)

    xs, hg, route, counts = _pool_router(
        xs, mix_norm_odd[0], pool_wb, pool_scale[0], ffn_norm_odd[0], w_router[0],
        batch=batch, ts=512)
    slots, bounds, tile_expert, n_tiles = _routing_tables(
        route, counts, n_experts=n_experts, tile=tile)
    rows = _dispatch(slots, bounds, hg, td=512, n_experts=n_experts, tile=tile)
    ys, (ple_gate1, ple_proj1) = _experts(
        tile_expert, n_tiles, rows, wg_exp, wu_exp, wd_exp, [(ple_gate_w, 1), (ple_proj, 1)],
        tile=tile)
    xs = _ple_combine(slots, xs, ple_norm[1], ple_gate1, pp, 1, ple_proj1,
                      final_norm_gain, ys, tm=256, final_norm=True)
    return xs.reshape(batch, s_len, d)
```

```python
import functools
import math

import jax
import jax.numpy as jnp
from jax import lax
from jax.experimental import pallas as pl
from jax.experimental.pallas import tpu as pltpu

F32 = jnp.float32
BF16 = jnp.bfloat16

EPS = 1e-6
HGRN_HEADS = 8
HGRN_CHUNK = 64
CONV_WIDTH = 3
POOL_WINDOWS = (2, 4, 8, 16)
POOL_HALO = 16
TOP_K = 2
EXPERT_ROW_TILE = 512
COMBINE_CHUNKS = 4
LANES = 128
SUBLANES = 8
BF16_SUBLANES = 16
VMEM_LIMIT_BYTES = 56 * 1024 * 1024


def _params(*semantics):
    return pltpu.CompilerParams(dimension_semantics=semantics,
                                vmem_limit_bytes=VMEM_LIMIT_BYTES)


def _rms_norm(x, gain):
    ms = jnp.mean(x * x, axis=-1, keepdims=True)
    return x * lax.rsqrt(ms + EPS) * gain


def _sigmoid(x):
    return 1.0 / (1.0 + jnp.exp(-x))


def _dot(a, b):
    return jnp.dot(a, b, preferred_element_type=F32)


def _dot_nt(a, b):
    return lax.dot_general(a, b, (((1,), (1,)), ((), ())), preferred_element_type=F32)


class _Casts:
    def __init__(self, weights, grid):
        n_steps = math.prod(grid)
        self.shapes = [stacked.shape[1:] for stacked, _ in weights]
        self.views, self.in_specs, self.out_specs, self.out_shapes = [], [], [], []
        for stacked, layer in weights:
            cols = stacked.shape[-1]
            rows = stacked[0].size // cols
            block_rows = next(br for br in range(BF16_SUBLANES, rows + 1, BF16_SUBLANES)
                              if rows % br == 0 and rows // br <= n_steps)
            n_blocks = rows // block_rows
            self.views.append(stacked.reshape(stacked.shape[0] * rows, cols))
            for specs, first in ((self.in_specs, layer * n_blocks), (self.out_specs, 0)):
                specs.append(pl.BlockSpec(
                    (block_rows, cols),
                    functools.partial(self._index_map, grid, n_blocks, first)))
            self.out_shapes.append(jax.ShapeDtypeStruct((rows, cols), BF16))

    @staticmethod
    def _index_map(grid, n_blocks, first, *args):
        step = 0
        for size, idx in zip(grid, args):
            step = step * size + idx
        return first + jnp.minimum(step, n_blocks - 1), 0

    def __len__(self):
        return len(self.views)

    def wrap(self, body, n_in, n_out, n_prefetch=0):
        n = len(self)

        def kernel(*refs):
            ins_end = n_prefetch + n_in
            outs_start = ins_end + n
            outs_end = outs_start + n_out
            body(*refs[:ins_end], *refs[outs_start:outs_end], *refs[outs_end + n:])
            for src, dst in zip(refs[ins_end:outs_start], refs[outs_end:outs_end + n]):
                dst[...] = src[...].astype(BF16)

        return kernel

    def restore(self, outs):
        return [o.reshape(shape) for o, shape in zip(outs, self.shapes)]


def _norm_matmul_kernel(x_ref, g_ref, w_ref, o_ref, h_ref):
    @pl.when(pl.program_id(1) == 0)
    def _():
        h_ref[...] = _rms_norm(x_ref[...], g_ref[...]).astype(BF16)

    o_ref[...] = _dot(h_ref[...], w_ref[...])


def _norm_matmul(x, gain, w, cast_weights, *, tm, tn):
    t, d = x.shape
    n = w.shape[1]
    grid = (t // tm, n // tn)
    casts = _Casts(cast_weights, grid)
    out, *cast = pl.pallas_call(
        casts.wrap(_norm_matmul_kernel, n_in=3, n_out=1),
        grid=grid,
        in_specs=[pl.BlockSpec((tm, d), lambda i, j: (i, 0)),
                  pl.BlockSpec((1, d), lambda i, j: (0, 0)),
                  pl.BlockSpec((d, tn), lambda i, j: (0, j)),
                  *casts.in_specs],
        out_specs=[pl.BlockSpec((tm, tn), lambda i, j: (i, j)), *casts.out_specs],
        out_shape=[jax.ShapeDtypeStruct((t, n), F32), *casts.out_shapes],
        scratch_shapes=[pltpu.VMEM((tm, d), BF16)],
        compiler_params=_params("parallel", "arbitrary"),
        name="norm_matmul",
    )(x, gain.reshape(1, d), w, *casts.views)
    return out, casts.restore(cast)


def _hgrn_conv_kernel(tri_ref, q_ref, f_ref, i_ref, g_ref, gb_ref, gc_ref, vc_ref,
                      lbt_ref, gain_ref, cw_ref, a_ref, b_ref, *, rows, layer):
    s_len, dh = q_ref.shape
    ch = HGRN_CHUNK
    ct = tri_ref.shape[0]

    lbt = lbt_ref[...]
    e = jnp.exp(lbt - jnp.max(lbt, axis=0, keepdims=True))
    lb = (jnp.sum(e[0:layer + 1, :], axis=0, keepdims=True)
          / jnp.sum(e, axis=0, keepdims=True))
    gain = gain_ref[...]

    tri = tri_ref[...]
    row = lax.broadcasted_iota(jnp.int32, (ct, ct), 0)
    col = lax.broadcasted_iota(jnp.int32, (ct, ct), 1)
    causal = (row >= col) & (row // ch == col // ch)
    n_chunks = rows // ch

    def body(t, st):
        r0 = pl.multiple_of(t * rows, rows)
        qr = q_ref[pl.ds(r0, rows), :]
        q = qr * _sigmoid(qr)
        f = lb + (1.0 - lb) * _sigmoid(f_ref[pl.ds(r0, rows), :])
        k = 1.0 - f
        v = i_ref[pl.ds(r0, rows), :].astype(BF16)
        lf = jnp.log(f)
        hi = lf.astype(BF16)
        r1 = lf - hi.astype(F32)
        mid = r1.astype(BF16)
        lo = (r1 - mid.astype(F32)).astype(BF16)
        parts = jnp.concatenate([hi, mid, lo], axis=1)
        cums = []
        for j in range(rows // ct):
            c3 = _dot(tri, parts[j * ct:(j + 1) * ct])
            cums.append(c3[:, :dh] + c3[:, dh:2 * dh] + c3[:, 2 * dh:])
        cum = jnp.concatenate(cums, axis=0)

        cum3 = cum.reshape(n_chunks, ch, dh)
        ref = cum3[:, ch // 2:ch // 2 + 1, :]
        last = cum3[:, ch - 1:ch, :]
        q3 = q.reshape(n_chunks, ch, dh)
        k3 = k.reshape(n_chunks, ch, dh)
        qe = (q3 * jnp.exp(cum3 - ref)).astype(BF16).reshape(rows, dh)
        ke = (k3 * jnp.exp(ref - cum3)).astype(BF16).reshape(rows, dh)
        qc = (q3 * jnp.exp(cum3)).astype(BF16).reshape(rows, dh)
        kd = (k3 * jnp.exp(last - cum3)).astype(BF16).reshape(rows, dh)
        decay = jnp.exp(last)

        chunks = [slice(c * ch, (c + 1) * ch) for c in range(n_chunks)]
        kv_t = [lax.dot_general(v[cs], kd[cs], (((0,), (0,)), ((), ())),
                                preferred_element_type=F32) for cs in chunks]
        states = []
        for c in range(n_chunks):
            states.append(st.astype(BF16))
            st = decay[c] * st + kv_t[c]
        o_inter = jnp.concatenate(
            [_dot_nt(qc[cs], s_c) for cs, s_c in zip(chunks, states)], axis=0)

        o_intra = []
        for j in range(rows // ct):
            sl = slice(j * ct, (j + 1) * ct)
            scores = jnp.where(causal, _dot_nt(qe[sl], ke[sl]), 0.0)
            o_intra.append(_dot(scores.astype(BF16), v[sl]))
        o = jnp.concatenate(o_intra, axis=0) + o_inter
        o = o * lax.rsqrt(jnp.mean(o * o, axis=-1, keepdims=True) + EPS)
        gr = g_ref[pl.ds(r0, rows), :]
        a_ref[pl.ds(r0, rows), :] = (o * gain * (gr * _sigmoid(gr))).astype(BF16)
        return st

    lax.fori_loop(0, s_len // rows, body, jnp.zeros((dh, dh), F32))

    tt = gc_ref[...] * vc_ref[...]
    ridx = lax.broadcasted_iota(jnp.int32, tt.shape, 0)
    cw = cw_ref[...]
    conv = tt * cw[CONV_WIDTH - 1:CONV_WIDTH, :]
    for back in range(1, CONV_WIDTH):
        shifted = jnp.where(ridx >= back, pltpu.roll(tt, back, 0), 0.0)
        conv = conv + shifted * cw[CONV_WIDTH - 1 - back:CONV_WIDTH - back, :]
    b_ref[...] = (gb_ref[...] * conv).astype(BF16)


def _hgrn_conv(u, lb_table, hgrn_gain, conv_w, cast_weights, *, batch, rows, layer):
    t, n_in = u.shape
    s_len = t // batch
    d_hgrn = hgrn_gain.shape[0]
    dh = d_hgrn // HGRN_HEADS
    d_conv = conv_w.shape[1]
    nh = HGRN_HEADS
    assert d_conv // dh == nh and n_in == 4 * d_hgrn + 3 * d_conv
    u3 = u.reshape(batch, s_len, n_in)

    idx = jnp.arange(2 * HGRN_CHUNK)
    tri = ((idx[:, None] >= idx[None, :])
           & (idx[:, None] // HGRN_CHUNK == idx[None, :] // HGRN_CHUNK)).astype(BF16)

    def col(off):
        return pl.BlockSpec((None, s_len, dh), lambda b, h, off=off: (b, 0, off + h))

    n_lb = lb_table.shape[0]
    out_spec = pl.BlockSpec((None, s_len, dh), lambda b, h: (b, 0, h))
    grid = (batch, nh)
    casts = _Casts(cast_weights, grid)
    a, b, *cast = pl.pallas_call(
        casts.wrap(functools.partial(_hgrn_conv_kernel, rows=rows, layer=layer),
                   n_in=11, n_out=2),
        grid=grid,
        in_specs=[pl.BlockSpec(tri.shape, lambda b, h: (0, 0)),
                  col(0), col(nh), col(2 * nh), col(3 * nh),
                  col(4 * nh), col(5 * nh), col(6 * nh),
                  pl.BlockSpec((n_lb, dh), lambda b, h: (0, h)),
                  pl.BlockSpec((1, dh), lambda b, h: (0, h)),
                  pl.BlockSpec((CONV_WIDTH, dh), lambda b, h: (0, h)),
                  *casts.in_specs],
        out_specs=[out_spec, out_spec, *casts.out_specs],
        out_shape=[jax.ShapeDtypeStruct((batch, s_len, d_hgrn), BF16),
                   jax.ShapeDtypeStruct((batch, s_len, d_conv), BF16),
                   *casts.out_shapes],
        compiler_params=_params("parallel", "parallel"),
        name="hgrn_conv",
    )(tri, u3, u3, u3, u3, u3, u3, u3, lb_table, hgrn_gain.reshape(1, d_hgrn), conv_w,
      *casts.views)
    return a, b, casts.restore(cast)


def _out_proj_kernel(x_ref, a_ref, b_ref, wa_ref, wb_ref, o_ref):
    o_ref[...] = x_ref[...] + _dot(a_ref[...], wa_ref[...]) + _dot(b_ref[...], wb_ref[...])


def _out_proj(x, a, b, w, cast_weights, *, tm, tn):
    t, d = x.shape
    ka, kb = a.shape[1], b.shape[1]
    assert ka == kb
    grid = (t // tm, d // tn)
    casts = _Casts(cast_weights, grid)
    out, *cast = pl.pallas_call(
        casts.wrap(_out_proj_kernel, n_in=5, n_out=1),
        grid=grid,
        in_specs=[pl.BlockSpec((tm, tn), lambda i, j: (i, j)),
                  pl.BlockSpec((tm, ka), lambda i, j: (i, 0)),
                  pl.BlockSpec((tm, kb), lambda i, j: (i, 0)),
                  pl.BlockSpec((ka, tn), lambda i, j: (0, j)),
                  pl.BlockSpec((kb, tn), lambda i, j: (1, j)),
                  *casts.in_specs],
        out_specs=[pl.BlockSpec((tm, tn), lambda i, j: (i, j)), *casts.out_specs],
        out_shape=[jax.ShapeDtypeStruct((t, d), F32), *casts.out_shapes],
        compiler_params=_params("parallel", "arbitrary"),
        name="out_proj",
    )(x, a, b, w, w, *casts.views)
    return out, casts.restore(cast)


def _swiglu_kernel(x_ref, g_ref, wg_ref, wu_ref, wd_ref, o_ref, h_ref):
    @pl.when(pl.program_id(1) == 0)
    def _():
        x = x_ref[...]
        h_ref[...] = _rms_norm(x, g_ref[...]).astype(BF16)
        o_ref[...] = x

    h = h_ref[...]
    gate = _dot(h, wg_ref[...])
    up = _dot(h, wu_ref[...])
    act = (gate * _sigmoid(gate) * up).astype(BF16)
    o_ref[...] += _dot(act, wd_ref[...])


def _swiglu(x, gain, wg, wu, wd, cast_weights, *, tm, tf):
    t, d = x.shape
    ff = wg.shape[1]
    grid = (t // tm, ff // tf)
    casts = _Casts(cast_weights, grid)
    out, *cast = pl.pallas_call(
        casts.wrap(_swiglu_kernel, n_in=5, n_out=1),
        grid=grid,
        in_specs=[pl.BlockSpec((tm, d), lambda i, f: (i, 0)),
                  pl.BlockSpec((1, d), lambda i, f: (0, 0)),
                  pl.BlockSpec((d, tf), lambda i, f: (0, f)),
                  pl.BlockSpec((d, tf), lambda i, f: (0, f)),
                  pl.BlockSpec((tf, d), lambda i, f: (f, 0)),
                  *casts.in_specs],
        out_specs=[pl.BlockSpec((tm, d), lambda i, f: (i, 0)), *casts.out_specs],
        out_shape=[jax.ShapeDtypeStruct((t, d), F32), *casts.out_shapes],
        scratch_shapes=[pltpu.VMEM((tm, d), BF16)],
        compiler_params=_params("parallel", "arbitrary"),
        name="swiglu",
    )(x, gain.reshape(1, d), wg, wu, wd, *casts.views)
    return out, casts.restore(cast)


def _ple_math(x, g_ref, wg_ref, p_ref, wp_ref, fg_ref, final_norm):
    h = _rms_norm(x, g_ref[...]).astype(BF16)
    gate = _sigmoid(_dot(h, wg_ref[...]))
    y = x + gate * _dot(p_ref[...].astype(BF16), wp_ref[...])
    if final_norm:
        y = _rms_norm(y, fg_ref[...])
    return y


def _ple_kernel(x_ref, g_ref, wg_ref, p_ref, wp_ref, fg_ref, o_ref, *, final_norm):
    o_ref[...] = _ple_math(x_ref[...], g_ref, wg_ref, p_ref, wp_ref, fg_ref, final_norm)


def _ple(x, gain, wg, p, layer, wp, final_gain, *, tm, final_norm):
    t, d = x.shape
    dp = p.shape[-1]
    return pl.pallas_call(
        functools.partial(_ple_kernel, final_norm=final_norm),
        grid=(t // tm,),
        in_specs=[pl.BlockSpec((tm, d), lambda i: (i, 0)),
                  pl.BlockSpec((1, d), lambda i: (0, 0)),
                  pl.BlockSpec((d, d), lambda i: (0, 0)),
                  pl.BlockSpec((None, tm, dp), lambda i: (layer, i, 0)),
                  pl.BlockSpec((dp, d), lambda i: (0, 0)),
                  pl.BlockSpec((1, d), lambda i: (0, 0))],
        out_specs=pl.BlockSpec((tm, d), lambda i: (i, 0)),
        out_shape=jax.ShapeDtypeStruct((t, d), F32),
        compiler_params=_params("parallel"),
        name="ple",
    )(x, gain.reshape(1, d), wg, p, wp, final_gain.reshape(1, d))


def _pool_router_kernel(x_ref, g_ref, pw_ref, ps_ref, fg_ref, wr_ref, lt_ref,
                        x1_ref, hg_ref, route_ref, counts_ref, ext_ref, cnt_ref,
                        *, n_experts):
    s = pl.program_id(1)
    ts, d = x_ref.shape
    n_groups = len(POOL_WINDOWS)
    gd = d // n_groups

    @pl.when(s == 0)
    def _():
        ext_ref[0:POOL_HALO, :] = jnp.zeros((POOL_HALO, d), F32)

    @pl.when(s > 0)
    def _():
        ext_ref[0:POOL_HALO, :] = ext_ref[ts:ts + POOL_HALO, :]

    x = x_ref[...]
    h = _rms_norm(x, g_ref[...])
    ext_ref[POOL_HALO:, :] = h

    def window_sum(cols, w):
        acc = ext_ref[:, cols]
        span = 1
        while span < w:
            acc = acc + pltpu.roll(acc, span, 0)
            span *= 2
        return acc[POOL_HALO:, :]

    pos = (s * ts + 1 + lax.broadcasted_iota(jnp.int32, (ts, 1), 0)).astype(F32)
    ys = []
    for gi, w in enumerate(POOL_WINDOWS):
        cols = slice(gi * gd, (gi + 1) * gd)
        mean = window_sum(cols, w) / jnp.minimum(pos, float(w))
        diff = (mean - h[:, cols]).astype(BF16)
        ys.append(_dot(diff, pw_ref[gi]))
    y = jnp.concatenate(ys, axis=1)
    x1 = x + y * ps_ref[...]
    x1_ref[...] = x1

    h2 = _rms_norm(x1, fg_ref[...])
    hg_ref[:, 0:d] = h2

    hi = h2.astype(BF16)
    mid = (h2 - hi.astype(F32)).astype(BF16)
    p_hi = _dot(hi, wr_ref[...])
    p_mid = _dot(mid, wr_ref[...])
    logits = (p_hi + pltpu.roll(p_hi, LANES - n_experts, 1)
              + pltpu.roll(p_hi, LANES - 2 * n_experts, 1)
              + p_mid + pltpu.roll(p_mid, LANES - n_experts, 1))
    lane = lax.broadcasted_iota(jnp.int32, logits.shape, 1)
    valid = lane < n_experts
    logits = jnp.where(valid, logits, -jnp.inf)
    ex = jnp.exp(logits - jnp.max(logits, axis=-1, keepdims=True))
    probs = ex / jnp.sum(ex, axis=-1, keepdims=True)
    gates = jnp.zeros_like(probs)
    rest = jnp.where(valid, probs, -1.0)
    tops = []
    for _ in range(TOP_K):
        m = jnp.max(rest, axis=-1, keepdims=True)
        first = jnp.min(jnp.where(rest == m, lane, LANES), axis=-1, keepdims=True)
        sel = lane == first
        tops.append((m, sel))
        rest = jnp.where(sel, -1.0, rest)
    total = tops[0][0]
    for m, _ in tops[1:]:
        total = total + m
    for m, sel in tops:
        gates = jnp.where(sel, m / total, gates)
    hg_ref[:, d:d + LANES] = gates

    first_step = (pl.program_id(0) == 0) & (s == 0)

    @pl.when(first_step)
    def _():
        cnt_ref[...] = jnp.zeros_like(cnt_ref)

    chosen = tops[0][1]
    for _, sel in tops[1:]:
        chosen = chosen | sel
    chosen = jnp.where(chosen, 1.0, 0.0)
    before = cnt_ref[...] + _dot(lt_ref[...], chosen.astype(BF16))
    cnt_ref[...] = cnt_ref[...] + jnp.sum(chosen, axis=0, keepdims=True)
    counts_ref[...] = cnt_ref[...]
    lane_f = lane.astype(F32)
    route = jnp.zeros((ts, LANES), F32)
    for k, (_, sel) in enumerate(tops):
        expert_k = jnp.sum(jnp.where(sel, lane_f, 0.0), axis=-1, keepdims=True)
        order_k = jnp.sum(jnp.where(sel, before, 0.0), axis=-1, keepdims=True)
        route = jnp.where(lane == k, expert_k, route)
        route = jnp.where(lane == TOP_K + k, order_k, route)
    route_ref[...] = route.T[0:SUBLANES, :].astype(jnp.int32)


def _pool_router(x, gain, pool_w, pool_scale, ffn_gain, w_router, *, batch, ts):
    t, d = x.shape
    s_len = t // batch
    n_s = s_len // ts
    n_groups, gd, _ = pool_w.shape
    n_experts = w_router.shape[1]
    assert 3 * n_experts <= LANES
    w_hi = w_router.astype(BF16)
    r1 = w_router - w_hi.astype(F32)
    w_mid = r1.astype(BF16)
    w_lo = (r1 - w_mid.astype(F32)).astype(BF16)
    wr = jnp.zeros((d, LANES), BF16).at[:, :3 * n_experts].set(
        jnp.concatenate([w_hi, w_mid, w_lo], axis=1))
    idx = jnp.arange(ts)
    lower = (idx[None, :] < idx[:, None]).astype(BF16)
    row = lambda b, s: (b * n_s + s, 0)
    const = lambda b, s: (0, 0)
    return pl.pallas_call(
        functools.partial(_pool_router_kernel, n_experts=n_experts),
        grid=(batch, n_s),
        in_specs=[pl.BlockSpec((ts, d), row),
                  pl.BlockSpec((1, d), const),
                  pl.BlockSpec((n_groups, gd, gd), lambda b, s: (0, 0, 0)),
                  pl.BlockSpec((1, d), const),
                  pl.BlockSpec((1, d), const),
                  pl.BlockSpec((d, LANES), const),
                  pl.BlockSpec((ts, ts), const)],
        out_specs=[pl.BlockSpec((ts, d), row),
                   pl.BlockSpec((ts, d + LANES), row),
                   pl.BlockSpec((SUBLANES, ts), lambda b, s: (0, b * n_s + s)),
                   pl.BlockSpec((1, LANES), const)],
        out_shape=[jax.ShapeDtypeStruct((t, d), F32),
                   jax.ShapeDtypeStruct((t, d + LANES), F32),
                   jax.ShapeDtypeStruct((SUBLANES, t), jnp.int32),
                   jax.ShapeDtypeStruct((1, LANES), F32)],
        scratch_shapes=[pltpu.VMEM((POOL_HALO + ts, d), F32),
                        pltpu.VMEM((1, LANES), F32)],
        compiler_params=_params("arbitrary", "arbitrary"),
        name="pool_router",
    )(x, gain.reshape(1, d), pool_w, pool_scale.reshape(1, d), ffn_gain.reshape(1, d), wr, lower)


def _dispatch_kernel(slots_ref, bounds_ref, hg_ref, xs_ref, zero_ref, sem, zero_sem,
                     *, n_experts, tile):
    i = pl.program_id(0)
    td = hg_ref.shape[0]
    n_tokens = pl.num_programs(0) * td

    for r in range(td):
        for k in range(TOP_K):
            slot = slots_ref[k * n_tokens + i * td + r]
            pltpu.make_async_copy(hg_ref.at[pl.ds(r, 1), :],
                                  xs_ref.at[pl.ds(slot, 1), :], sem).start(priority=k % 2)

    def zero_copies():
        copies = []

        def block(first, size):
            return pltpu.make_async_copy(zero_ref.at[pl.ds(0, size), :],
                                         xs_ref.at[pl.ds(first, size), :], zero_sem)

        for e in range(n_experts):
            first_unused = bounds_ref[e]
            end = bounds_ref[n_experts + e]
            aligned = (first_unused + SUBLANES - 1) // SUBLANES * SUBLANES
            for r in range(SUBLANES - 1):
                copies.append((first_unused + r < aligned, block(first_unused + r, 1)))
            left = end - aligned
            pos = aligned
            size = tile // 2
            while size >= SUBLANES:
                needed = (left & size) != 0
                copies.append((needed, block(pl.multiple_of(pos, SUBLANES), size)))
                pos = pos + jnp.where(needed, size, 0)
                size //= 2
        total = bounds_ref[2 * n_experts]
        for b in range(n_experts):
            first = pl.multiple_of(total + b * tile, tile)
            copies.append((first < xs_ref.shape[0], block(first, tile)))
        return copies

    @pl.when(i == 0)
    def _():
        zero_ref[...] = jnp.zeros_like(zero_ref)
        for action in ("start", "wait"):
            for needed, copy in zero_copies():
                pl.when(needed)(getattr(copy, action))

    for _ in range(TOP_K):
        pltpu.make_async_copy(hg_ref, xs_ref.at[pl.ds(0, td), :], sem).wait()


def _dispatch(slots_flat, bounds, hg, *, td, n_experts, tile):
    t, width = hg.shape
    n_rows = t * TOP_K + n_experts * tile
    return pl.pallas_call(
        functools.partial(_dispatch_kernel, n_experts=n_experts, tile=tile),
        grid_spec=pltpu.PrefetchScalarGridSpec(
            num_scalar_prefetch=2,
            grid=(t // td,),
            in_specs=[pl.BlockSpec((td, width), lambda i, slots, bounds: (i, 0))],
            out_specs=pl.BlockSpec(memory_space=pl.ANY),
            scratch_shapes=[pltpu.VMEM((tile, width), F32),
                            pltpu.SemaphoreType.DMA,
                            pltpu.SemaphoreType.DMA]),
        out_shape=jax.ShapeDtypeStruct((n_rows, width), F32),
        compiler_params=_params("arbitrary"),
        name="moe_dispatch",
    )(slots_flat, bounds, hg)


def _experts_kernel(te_ref, nt_ref, xs_ref, wg_ref, wu_ref, wd_ref, ys_ref):
    i = pl.program_id(0)
    d = wg_ref.shape[0]

    @pl.when(i < nt_ref[0])
    def _():
        rows = xs_ref[:, 0:d].astype(BF16)
        gates = xs_ref[:, d:d + LANES]
        lane = lax.broadcasted_iota(jnp.int32, gates.shape, 1)
        ge = jnp.sum(jnp.where(lane == te_ref[i], gates, 0.0), axis=-1, keepdims=True)
        gate = _dot(rows, wg_ref[...])
        up = _dot(rows, wu_ref[...])
        act = (ge * (gate * _sigmoid(gate) * up)).astype(BF16)
        ys_ref[...] = _dot(act, wd_ref[...])

    @pl.when(i >= nt_ref[0])
    def _():
        ys_ref[...] = jnp.zeros_like(ys_ref)


def _experts(tile_expert, n_tiles, xs, wg, wu, wd, cast_weights, *, tile):
    n_rows, width = xs.shape
    n_experts, d, ffe = wg.shape
    row_map = lambda i, te, nt: (i, 0)
    w_map = lambda i, te, nt: (te[i], 0, 0)
    grid = (n_rows // tile,)
    casts = _Casts(cast_weights, grid)
    out, *cast = pl.pallas_call(
        casts.wrap(_experts_kernel, n_in=4, n_out=1, n_prefetch=2),
        grid_spec=pltpu.PrefetchScalarGridSpec(
            num_scalar_prefetch=2,
            grid=grid,
            in_specs=[pl.BlockSpec((tile, width), row_map),
                      pl.BlockSpec((None, d, ffe), w_map),
                      pl.BlockSpec((None, d, ffe), w_map),
                      pl.BlockSpec((None, ffe, d), w_map),
                      *casts.in_specs],
            out_specs=[pl.BlockSpec((tile, d), row_map), *casts.out_specs]),
        out_shape=[jax.ShapeDtypeStruct((n_rows, d), F32), *casts.out_shapes],
        compiler_params=_params("arbitrary"),
        name="moe_experts",
    )(tile_expert, n_tiles, xs, wg, wu, wd, *casts.views)
    return out, casts.restore(cast)


def _routing_tables(route, counts, *, n_experts, tile):
    t = route.shape[1]
    counts = counts[0, :n_experts].astype(jnp.int32)
    padded = (counts + tile - 1) // tile * tile
    ends = jnp.cumsum(padded)
    starts = ends - padded
    expert, order = route[:TOP_K], route[TOP_K:2 * TOP_K]
    slots = order
    for e in range(n_experts):
        slots = slots + jnp.where(expert == e, starts[e], 0)
    slots = slots.reshape(-1)
    bounds = jnp.concatenate([starts + counts, ends, ends[-1:]]).astype(jnp.int32)
    n_tiles_max = (t * TOP_K) // tile + n_experts
    first_row = jnp.arange(n_tiles_max, dtype=jnp.int32) * tile
    tile_expert = jnp.minimum(jnp.sum(first_row[:, None] >= ends[None, :], axis=1),
                              n_experts - 1).astype(jnp.int32)
    n_tiles = (ends[-1:] // tile).astype(jnp.int32)
    return slots.astype(jnp.int32), bounds, tile_expert, n_tiles


def _ple_combine_kernel(slots_ref, x_ref, g_ref, wg_ref, p_ref, wp_ref, fg_ref, ys_ref,
                        o_ref, ybuf, sem, *, final_norm):
    i = pl.program_id(0)
    n = pl.num_programs(0)
    tm, d = x_ref.shape
    n_tokens = n * tm
    chunk_cols = d // COMBINE_CHUNKS
    chunk_rows = tm // COMBINE_CHUNKS

    def start_rows(step, buf, first, last):
        for r in range(first, last):
            for k in range(TOP_K):
                slot = slots_ref[k * n_tokens + step * tm + r]
                pltpu.make_async_copy(ys_ref.at[pl.ds(slot, 1), :],
                                      ybuf.at[buf, k, pl.ds(r, 1), :],
                                      sem.at[buf]).start(priority=k % 2)

    def wait_tile(buf):
        for k in range(TOP_K):
            pltpu.make_async_copy(ys_ref.at[pl.ds(0, tm), :], ybuf.at[buf, k],
                                  sem.at[buf]).wait()

    cur = i % 2

    @pl.when(i == 0)
    def _():
        start_rows(i, cur, 0, tm)

    wait_tile(cur)
    x = x_ref[...]
    for k in range(TOP_K):
        x = x + ybuf[cur, k]
    h = _rms_norm(x, g_ref[...]).astype(BF16)
    pb = p_ref[...].astype(BF16)
    nxt = jnp.minimum(i + 1, n - 1)
    for c in range(COMBINE_CHUNKS):
        cols = slice(c * chunk_cols, (c + 1) * chunk_cols)
        gate = _sigmoid(_dot(h, wg_ref[:, cols]))
        o_ref[:, cols] = x[:, cols] + gate * _dot(pb, wp_ref[:, cols])
        start_rows(nxt, 1 - cur, c * chunk_rows, (c + 1) * chunk_rows)
    if final_norm:
        o_ref[...] = _rms_norm(o_ref[...], fg_ref[...])

    @pl.when(i == n - 1)
    def _():
        wait_tile(1 - cur)


def _ple_combine(slots_flat, x, gain, wg, p, layer, wp, final_gain, ys, *, tm, final_norm):
    t, d = x.shape
    dp = p.shape[-1]
    row = lambda i, slots: (i, 0)
    const = lambda i, slots: (0, 0)
    return pl.pallas_call(
        functools.partial(_ple_combine_kernel, final_norm=final_norm),
        grid_spec=pltpu.PrefetchScalarGridSpec(
            num_scalar_prefetch=1,
            grid=(t // tm,),
            in_specs=[pl.BlockSpec((tm, d), row),
                      pl.BlockSpec((1, d), const),
                      pl.BlockSpec((d, d), const),
                      pl.BlockSpec((None, tm, dp), lambda i, slots: (layer, i, 0)),
                      pl.BlockSpec((dp, d), const),
                      pl.BlockSpec((1, d), const),
                      pl.BlockSpec(memory_space=pl.ANY)],
            out_specs=pl.BlockSpec((tm, d), row),
            scratch_shapes=[pltpu.VMEM((2, TOP_K, tm, d), F32),
                            pltpu.SemaphoreType.DMA((2,))]),
        out_shape=jax.ShapeDtypeStruct((t, d), F32),
        compiler_params=_params("arbitrary"),
        name="ple_combine",
    )(slots_flat, x, gain.reshape(1, d), wg, p, wp, final_gain.reshape(1, d), ys)


def kernel(x, p, final_norm_gain, lb_table, mix_norm_even, w_in_even, hgrn_norm_gain, conv_w, w_out_even, ffn_norm_even, w_gate_dense, w_up_dense, w_down_dense, mix_norm_odd, pool_w, pool_scale, ffn_norm_odd, w_router, w_gate_exp, w_up_exp, w_down_exp, ple_norm, ple_gate_w, ple_proj):
    batch, s_len, d = x.shape
    t = batch * s_len
    depth = p.shape[0]
    assert depth == 2, "the weight-cast schedule below is written for one layer pair"
    n_experts = w_router.shape[-1]
    tile = EXPERT_ROW_TILE
    xs = x.reshape(t, d)

    pp = p.reshape(depth, t, -1)

    u, _ = _norm_matmul(xs, mix_norm_even[0], w_in_even[0].astype(BF16), [], tm=1024, tn=1792)
    a, b, (w_out, w_gate, w_up, w_down) = _hgrn_conv(
        u, lb_table, hgrn_norm_gain[0], conv_w[0],
        [(w_out_even, 0), (w_gate_dense, 0), (w_up_dense, 0), (w_down_dense, 0)],
        batch=batch, rows=1024, layer=0)
    xs, (ple_gate0, ple_proj0, pool_wb) = _out_proj(
        xs, a.reshape(t, -1), b.reshape(t, -1), w_out,
        [(ple_gate_w, 0), (ple_proj, 0), (pool_w, 0)], tm=512, tn=d)
    xs, (wg_exp, wu_exp, wd_exp) = _swiglu(
        xs, ffn_norm_even[0], w_gate, w_up, w_down,
        [(w_gate_exp, 0), (w_up_exp, 0), (w_down_exp, 0)], tm=1024, tf=256)
    xs = _ple(xs, ple_norm[0], ple_gate0, pp, 0, ple_proj0, final_norm_gain,
              tm=512, final_norm=False)

    xs, hg, route, counts = _pool_router(
        xs, mix_norm_odd[0], pool_wb, pool_scale[0], ffn_norm_odd[0], w_router[0],
        batch=batch, ts=512)
    slots, bounds, tile_expert, n_tiles = _routing_tables(
        route, counts, n_experts=n_experts, tile=tile)
    rows = _dispatch(slots, bounds, hg, td=512, n_experts=n_experts, tile=tile)
    ys, (ple_gate1, ple_proj1) = _experts(
        tile_expert, n_tiles, rows, wg_exp, wu_exp, wd_exp, [(ple_gate_w, 1), (ple_proj, 1)],
        tile=tile)
    xs = _ple_combine(slots, xs, ple_norm[1], ple_gate1, pp, 1, ple_proj1,
                      final_norm_gain, ys, tm=256, final_norm=True)
    return xs.reshape(batch, s_len, d)
```

```python
import functools
import math

import jax
import jax.numpy as jnp
from jax import lax
from jax.experimental import pallas as pl
from jax.experimental.pallas import tpu as pltpu

F32 = jnp.float32
BF16 = jnp.bfloat16

EPS = 1e-6
HGRN_HEADS = 8
HGRN_CHUNK = 64
CONV_WIDTH = 3
POOL_WINDOWS = (2, 4, 8, 16)
POOL_HALO = 16
TOP_K = 2
EXPERT_ROW_TILE = 512
COMBINE_CHUNKS = 8
LANES = 128
SUBLANES = 8
BF16_SUBLANES = 16
VMEM_LIMIT_BYTES = 56 * 1024 * 1024

IN_PROJ_TILE = (1024, 1792)
HGRN_ROWS = 1024
OUT_PROJ_ROWS = 512
SWIGLU_TILE = (1024, 256)
PLE_ROWS = 512
POOL_ROWS = 512
DISPATCH_ROWS = 512
COMBINE_ROWS = 256


def _params(*semantics):
    return pltpu.CompilerParams(dimension_semantics=semantics,
                                vmem_limit_bytes=VMEM_LIMIT_BYTES)


def _rms_norm(x, gain):
    ms = jnp.mean(x * x, axis=-1, keepdims=True)
    return x * lax.rsqrt(ms + EPS) * gain


def _sigmoid(x):
    return 1.0 / (1.0 + jnp.exp(-x))


def _dot(a, b):
    return jnp.dot(a, b, preferred_element_type=F32)


def _dot_nt(a, b):
    return lax.dot_general(a, b, (((1,), (1,)), ((), ())), preferred_element_type=F32)


class _Casts:
    def __init__(self, weights, grid):
        n_steps = math.prod(grid)
        self.shapes = [stacked.shape[1:] for stacked, _ in weights]
        self.views, self.in_specs, self.out_specs, self.out_shapes = [], [], [], []
        for stacked, layer in weights:
            cols = stacked.shape[-1]
            rows = stacked[0].size // cols
            block_rows = next(br for br in range(BF16_SUBLANES, rows + 1, BF16_SUBLANES)
                              if rows % br == 0 and rows // br <= n_steps)
            n_blocks = rows // block_rows
            self.views.append(stacked.reshape(stacked.shape[0] * rows, cols))
            for specs, first in ((self.in_specs, layer * n_blocks), (self.out_specs, 0)):
                specs.append(pl.BlockSpec(
                    (block_rows, cols),
                    functools.partial(self._index_map, grid, n_blocks, first)))
            self.out_shapes.append(jax.ShapeDtypeStruct((rows, cols), BF16))

    @staticmethod
    def _index_map(grid, n_blocks, first, *args):
        step = 0
        for size, idx in zip(grid, args):
            step = step * size + idx
        return first + jnp.minimum(step, n_blocks - 1), 0

    def __len__(self):
        return len(self.views)

    def wrap(self, body, n_in, n_out, n_prefetch=0):
        n = len(self)

        def kernel(*refs):
            ins_end = n_prefetch + n_in
            outs_start = ins_end + n
            outs_end = outs_start + n_out
            body(*refs[:ins_end], *refs[outs_start:outs_end], *refs[outs_end + n:])
            for src, dst in zip(refs[ins_end:outs_start], refs[outs_end:outs_end + n]):
                dst[...] = src[...].astype(BF16)

        return kernel

    def restore(self, outs):
        return [o.reshape(shape) for o, shape in zip(outs, self.shapes)]


def _norm_matmul_kernel(x_ref, g_ref, w_ref, o_ref, h_ref):
    @pl.when(pl.program_id(1) == 0)
    def _():
        h_ref[...] = _rms_norm(x_ref[...], g_ref[...]).astype(BF16)

    o_ref[...] = _dot(h_ref[...], w_ref[...])


def _norm_matmul(x, gain, w, cast_weights, *, tm, tn):
    t, d = x.shape
    n = w.shape[1]
    grid = (t // tm, n // tn)
    casts = _Casts(cast_weights, grid)
    out, *cast = pl.pallas_call(
        casts.wrap(_norm_matmul_kernel, n_in=3, n_out=1),
        grid=grid,
        in_specs=[pl.BlockSpec((tm, d), lambda i, j: (i, 0)),
                  pl.BlockSpec((1, d), lambda i, j: (0, 0)),
                  pl.BlockSpec((d, tn), lambda i, j: (0, j)),
                  *casts.in_specs],
        out_specs=[pl.BlockSpec((tm, tn), lambda i, j: (i, j)), *casts.out_specs],
        out_shape=[jax.ShapeDtypeStruct((t, n), F32), *casts.out_shapes],
        scratch_shapes=[pltpu.VMEM((tm, d), BF16)],
        compiler_params=_params("parallel", "arbitrary"),
        name="norm_matmul",
    )(x, gain.reshape(1, d), w, *casts.views)
    return out, casts.restore(cast)


def _hgrn_conv_kernel(tri_ref, q_ref, f_ref, i_ref, g_ref, gb_ref, gc_ref, vc_ref,
                      lbt_ref, gain_ref, cw_ref, a_ref, b_ref, *, rows, layer):
    s_len, dh = q_ref.shape
    ch = HGRN_CHUNK
    ct = tri_ref.shape[0]

    lbt = lbt_ref[...]
    e = jnp.exp(lbt - jnp.max(lbt, axis=0, keepdims=True))
    lb = (jnp.sum(e[0:layer + 1, :], axis=0, keepdims=True)
          / jnp.sum(e, axis=0, keepdims=True))
    gain = gain_ref[...]

    tri = tri_ref[...]
    row = lax.broadcasted_iota(jnp.int32, (ct, ct), 0)
    col = lax.broadcasted_iota(jnp.int32, (ct, ct), 1)
    causal = (row >= col) & (row // ch == col // ch)
    n_chunks = rows // ch

    def body(t, st):
        r0 = pl.multiple_of(t * rows, rows)
        qr = q_ref[pl.ds(r0, rows), :]
        q = qr * _sigmoid(qr)
        f = lb + (1.0 - lb) * _sigmoid(f_ref[pl.ds(r0, rows), :])
        k = 1.0 - f
        v = i_ref[pl.ds(r0, rows), :].astype(BF16)
        lf = jnp.log(f)
        hi = lf.astype(BF16)
        r1 = lf - hi.astype(F32)
        mid = r1.astype(BF16)
        lo = (r1 - mid.astype(F32)).astype(BF16)
        parts = jnp.concatenate([hi, mid, lo], axis=1)
        cums = []
        for j in range(rows // ct):
            c3 = _dot(tri, parts[j * ct:(j + 1) * ct])
            cums.append(c3[:, :dh] + c3[:, dh:2 * dh] + c3[:, 2 * dh:])
        cum = jnp.concatenate(cums, axis=0)

        cum3 = cum.reshape(n_chunks, ch, dh)
        ref = cum3[:, ch // 2:ch // 2 + 1, :]
        last = cum3[:, ch - 1:ch, :]
        q3 = q.reshape(n_chunks, ch, dh)
        k3 = k.reshape(n_chunks, ch, dh)
        qe = (q3 * jnp.exp(cum3 - ref)).astype(BF16).reshape(rows, dh)
        ke = (k3 * jnp.exp(ref - cum3)).astype(BF16).reshape(rows, dh)
        qc = (q3 * jnp.exp(cum3)).astype(BF16).reshape(rows, dh)
        kd = (k3 * jnp.exp(last - cum3)).astype(BF16).reshape(rows, dh)
        decay = jnp.exp(last)

        chunks = [slice(c * ch, (c + 1) * ch) for c in range(n_chunks)]
        kv_t = [lax.dot_general(v[cs], kd[cs], (((0,), (0,)), ((), ())),
                                preferred_element_type=F32) for cs in chunks]
        states = []
        for c in range(n_chunks):
            states.append(st.astype(BF16))
            st = decay[c] * st + kv_t[c]
        o_inter = jnp.concatenate(
            [_dot_nt(qc[cs], s_c) for cs, s_c in zip(chunks, states)], axis=0)

        o_intra = []
        for j in range(rows // ct):
            sl = slice(j * ct, (j + 1) * ct)
            scores = jnp.where(causal, _dot_nt(qe[sl], ke[sl]), 0.0)
            o_intra.append(_dot(scores.astype(BF16), v[sl]))
        o = jnp.concatenate(o_intra, axis=0) + o_inter
        o = o * lax.rsqrt(jnp.mean(o * o, axis=-1, keepdims=True) + EPS)
        gr = g_ref[pl.ds(r0, rows), :]
        a_ref[pl.ds(r0, rows), :] = (o * gain * (gr * _sigmoid(gr))).astype(BF16)
        return st

    lax.fori_loop(0, s_len // rows, body, jnp.zeros((dh, dh), F32))

    tt = gc_ref[...] * vc_ref[...]
    ridx = lax.broadcasted_iota(jnp.int32, tt.shape, 0)
    cw = cw_ref[...]
    conv = tt * cw[CONV_WIDTH - 1:CONV_WIDTH, :]
    for back in range(1, CONV_WIDTH):
        shifted = jnp.where(ridx >= back, pltpu.roll(tt, back, 0), 0.0)
        conv = conv + shifted * cw[CONV_WIDTH - 1 - back:CONV_WIDTH - back, :]
    b_ref[...] = (gb_ref[...] * conv).astype(BF16)


def _hgrn_conv(u, lb_table, hgrn_gain, conv_w, cast_weights, *, batch, rows, layer):
    t, n_in = u.shape
    s_len = t // batch
    d_hgrn = hgrn_gain.shape[0]
    dh = d_hgrn // HGRN_HEADS
    d_conv = conv_w.shape[1]
    nh = HGRN_HEADS
    assert d_conv // dh == nh and n_in == 4 * d_hgrn + 3 * d_conv
    u3 = u.reshape(batch, s_len, n_in)

    idx = jnp.arange(2 * HGRN_CHUNK)
    tri = ((idx[:, None] >= idx[None, :])
           & (idx[:, None] // HGRN_CHUNK == idx[None, :] // HGRN_CHUNK)).astype(BF16)

    def col(off):
        return pl.BlockSpec((None, s_len, dh), lambda b, h, off=off: (b, 0, off + h))

    n_lb = lb_table.shape[0]
    out_spec = pl.BlockSpec((None, s_len, dh), lambda b, h: (b, 0, h))
    grid = (batch, nh)
    casts = _Casts(cast_weights, grid)
    a, b, *cast = pl.pallas_call(
        casts.wrap(functools.partial(_hgrn_conv_kernel, rows=rows, layer=layer),
                   n_in=11, n_out=2),
        grid=grid,
        in_specs=[pl.BlockSpec(tri.shape, lambda b, h: (0, 0)),
                  col(0), col(nh), col(2 * nh), col(3 * nh),
                  col(4 * nh), col(5 * nh), col(6 * nh),
                  pl.BlockSpec((n_lb, dh), lambda b, h: (0, h)),
                  pl.BlockSpec((1, dh), lambda b, h: (0, h)),
                  pl.BlockSpec((CONV_WIDTH, dh), lambda b, h: (0, h)),
                  *casts.in_specs],
        out_specs=[out_spec, out_spec, *casts.out_specs],
        out_shape=[jax.ShapeDtypeStruct((batch, s_len, d_hgrn), BF16),
                   jax.ShapeDtypeStruct((batch, s_len, d_conv), BF16),
                   *casts.out_shapes],
        compiler_params=_params("parallel", "parallel"),
        name="hgrn_conv",
    )(tri, u3, u3, u3, u3, u3, u3, u3, lb_table, hgrn_gain.reshape(1, d_hgrn), conv_w,
      *casts.views)
    return a, b, casts.restore(cast)


def _out_proj_kernel(x_ref, a_ref, b_ref, wa_ref, wb_ref, o_ref):
    o_ref[...] = x_ref[...] + _dot(a_ref[...], wa_ref[...]) + _dot(b_ref[...], wb_ref[...])


def _out_proj(x, a, b, w, cast_weights, *, tm, tn):
    t, d = x.shape
    ka, kb = a.shape[1], b.shape[1]
    assert ka == kb
    grid = (t // tm, d // tn)
    casts = _Casts(cast_weights, grid)
    out, *cast = pl.pallas_call(
        casts.wrap(_out_proj_kernel, n_in=5, n_out=1),
        grid=grid,
        in_specs=[pl.BlockSpec((tm, tn), lambda i, j: (i, j)),
                  pl.BlockSpec((tm, ka), lambda i, j: (i, 0)),
                  pl.BlockSpec((tm, kb), lambda i, j: (i, 0)),
                  pl.BlockSpec((ka, tn), lambda i, j: (0, j)),
                  pl.BlockSpec((kb, tn), lambda i, j: (1, j)),
                  *casts.in_specs],
        out_specs=[pl.BlockSpec((tm, tn), lambda i, j: (i, j)), *casts.out_specs],
        out_shape=[jax.ShapeDtypeStruct((t, d), F32), *casts.out_shapes],
        compiler_params=_params("parallel", "arbitrary"),
        name="out_proj",
    )(x, a, b, w, w, *casts.views)
    return out, casts.restore(cast)


def _swiglu_kernel(x_ref, g_ref, wg_ref, wu_ref, wd_ref, o_ref, h_ref):
    @pl.when(pl.program_id(1) == 0)
    def _():
        x = x_ref[...]
        h_ref[...] = _rms_norm(x, g_ref[...]).astype(BF16)
        o_ref[...] = x

    h = h_ref[...]
    gate = _dot(h, wg_ref[...])
    up = _dot(h, wu_ref[...])
    act = (gate * _sigmoid(gate) * up).astype(BF16)
    o_ref[...] += _dot(act, wd_ref[...])


def _swiglu(x, gain, wg, wu, wd, cast_weights, *, tm, tf):
    t, d = x.shape
    ff = wg.shape[1]
    grid = (t // tm, ff // tf)
    casts = _Casts(cast_weights, grid)
    out, *cast = pl.pallas_call(
        casts.wrap(_swiglu_kernel, n_in=5, n_out=1),
        grid=grid,
        in_specs=[pl.BlockSpec((tm, d), lambda i, f: (i, 0)),
                  pl.BlockSpec((1, d), lambda i, f: (0, 0)),
                  pl.BlockSpec((d, tf), lambda i, f: (0, f)),
                  pl.BlockSpec((d, tf), lambda i, f: (0, f)),
                  pl.BlockSpec((tf, d), lambda i, f: (f, 0)),
                  *casts.in_specs],
        out_specs=[pl.BlockSpec((tm, d), lambda i, f: (i, 0)), *casts.out_specs],
        out_shape=[jax.ShapeDtypeStruct((t, d), F32), *casts.out_shapes],
        scratch_shapes=[pltpu.VMEM((tm, d), BF16)],
        compiler_params=_params("parallel", "arbitrary"),
        name="swiglu",
    )(x, gain.reshape(1, d), wg, wu, wd, *casts.views)
    return out, casts.restore(cast)


def _ple_math(x, g_ref, wg_ref, p_ref, wp_ref, fg_ref, final_norm):
    h = _rms_norm(x, g_ref[...]).astype(BF16)
    gate = _sigmoid(_dot(h, wg_ref[...]))
    y = x + gate * _dot(p_ref[...].astype(BF16), wp_ref[...])
    if final_norm:
        y = _rms_norm(y, fg_ref[...])
    return y


def _ple_kernel(x_ref, g_ref, wg_ref, p_ref, wp_ref, fg_ref, o_ref, *, final_norm):
    o_ref[...] = _ple_math(x_ref[...], g_ref, wg_ref, p_ref, wp_ref, fg_ref, final_norm)


def _ple(x, gain, wg, p, layer, wp, final_gain, *, tm, final_norm):
    t, d = x.shape
    dp = p.shape[-1]
    return pl.pallas_call(
        functools.partial(_ple_kernel, final_norm=final_norm),
        grid=(t // tm,),
        in_specs=[pl.BlockSpec((tm, d), lambda i: (i, 0)),
                  pl.BlockSpec((1, d), lambda i: (0, 0)),
                  pl.BlockSpec((d, d), lambda i: (0, 0)),
                  pl.BlockSpec((None, tm, dp), lambda i: (layer, i, 0)),
                  pl.BlockSpec((dp, d), lambda i: (0, 0)),
                  pl.BlockSpec((1, d), lambda i: (0, 0))],
        out_specs=pl.BlockSpec((tm, d), lambda i: (i, 0)),
        out_shape=jax.ShapeDtypeStruct((t, d), F32),
        compiler_params=_params("parallel"),
        name="ple",
    )(x, gain.reshape(1, d), wg, p, wp, final_gain.reshape(1, d))


def _pool_router_kernel(x_ref, g_ref, pw_ref, ps_ref, fg_ref, wr_ref, lt_ref,
                        x1_ref, hg_ref, route_ref, counts_ref, ext_ref, cnt_ref,
                        *, n_experts):
    s = pl.program_id(1)
    ts, d = x_ref.shape
    n_groups = len(POOL_WINDOWS)
    gd = d // n_groups

    @pl.when(s == 0)
    def _():
        ext_ref[0:POOL_HALO, :] = jnp.zeros((POOL_HALO, d), F32)

    @pl.when(s > 0)
    def _():
        ext_ref[0:POOL_HALO, :] = ext_ref[ts:ts + POOL_HALO, :]

    x = x_ref[...]
    h = _rms_norm(x, g_ref[...])
    ext_ref[POOL_HALO:, :] = h

    def window_sum(cols, w):
        acc = ext_ref[:, cols]
        span = 1
        while span < w:
            acc = acc + pltpu.roll(acc, span, 0)
            span *= 2
        return acc[POOL_HALO:, :]

    pos = (s * ts + 1 + lax.broadcasted_iota(jnp.int32, (ts, 1), 0)).astype(F32)
    ys = []
    for gi, w in enumerate(POOL_WINDOWS):
        cols = slice(gi * gd, (gi + 1) * gd)
        mean = window_sum(cols, w) / jnp.minimum(pos, float(w))
        diff = (mean - h[:, cols]).astype(BF16)
        ys.append(_dot(diff, pw_ref[gi]))
    y = jnp.concatenate(ys, axis=1)
    x1 = x + y * ps_ref[...]
    x1_ref[...] = x1

    h2 = _rms_norm(x1, fg_ref[...])
    hg_ref[:, 0:d] = h2

    hi = h2.astype(BF16)
    mid = (h2 - hi.astype(F32)).astype(BF16)
    p_hi = _dot(hi, wr_ref[...])
    p_mid = _dot(mid, wr_ref[...])
    logits = (p_hi + pltpu.roll(p_hi, LANES - n_experts, 1)
              + pltpu.roll(p_hi, LANES - 2 * n_experts, 1)
              + p_mid + pltpu.roll(p_mid, LANES - n_experts, 1))
    lane = lax.broadcasted_iota(jnp.int32, logits.shape, 1)
    valid = lane < n_experts
    logits = jnp.where(valid, logits, -jnp.inf)
    ex = jnp.exp(logits - jnp.max(logits, axis=-1, keepdims=True))
    probs = ex / jnp.sum(ex, axis=-1, keepdims=True)
    gates = jnp.zeros_like(probs)
    rest = jnp.where(valid, probs, -1.0)
    tops = []
    for _ in range(TOP_K):
        m = jnp.max(rest, axis=-1, keepdims=True)
        first = jnp.min(jnp.where(rest == m, lane, LANES), axis=-1, keepdims=True)
        sel = lane == first
        tops.append((m, sel))
        rest = jnp.where(sel, -1.0, rest)
    total = tops[0][0]
    for m, _ in tops[1:]:
        total = total + m
    for m, sel in tops:
        gates = jnp.where(sel, m / total, gates)
    hg_ref[:, d:d + LANES] = gates

    first_step = (pl.program_id(0) == 0) & (s == 0)

    @pl.when(first_step)
    def _():
        cnt_ref[...] = jnp.zeros_like(cnt_ref)

    chosen = tops[0][1]
    for _, sel in tops[1:]:
        chosen = chosen | sel
    chosen = jnp.where(chosen, 1.0, 0.0)
    before = cnt_ref[...] + _dot(lt_ref[...], chosen.astype(BF16))
    cnt_ref[...] = cnt_ref[...] + jnp.sum(chosen, axis=0, keepdims=True)
    counts_ref[...] = cnt_ref[...]
    lane_f = lane.astype(F32)
    route = jnp.zeros((ts, LANES), F32)
    for k, (_, sel) in enumerate(tops):
        expert_k = jnp.sum(jnp.where(sel, lane_f, 0.0), axis=-1, keepdims=True)
        order_k = jnp.sum(jnp.where(sel, before, 0.0), axis=-1, keepdims=True)
        route = jnp.where(lane == k, expert_k, route)
        route = jnp.where(lane == TOP_K + k, order_k, route)
    route_ref[...] = route.T[0:SUBLANES, :].astype(jnp.int32)


def _pool_router(x, gain, pool_w, pool_scale, ffn_gain, w_router, *, batch, ts):
    t, d = x.shape
    s_len = t // batch
    n_s = s_len // ts
    n_groups, gd, _ = pool_w.shape
    n_experts = w_router.shape[1]
    assert 3 * n_experts <= LANES
    w_hi = w_router.astype(BF16)
    r1 = w_router - w_hi.astype(F32)
    w_mid = r1.astype(BF16)
    w_lo = (r1 - w_mid.astype(F32)).astype(BF16)
    wr = jnp.zeros((d, LANES), BF16).at[:, :3 * n_experts].set(
        jnp.concatenate([w_hi, w_mid, w_lo], axis=1))
    idx = jnp.arange(ts)
    lower = (idx[None, :] < idx[:, None]).astype(BF16)
    row = lambda b, s: (b * n_s + s, 0)
    const = lambda b, s: (0, 0)
    return pl.pallas_call(
        functools.partial(_pool_router_kernel, n_experts=n_experts),
        grid=(batch, n_s),
        in_specs=[pl.BlockSpec((ts, d), row),
                  pl.BlockSpec((1, d), const),
                  pl.BlockSpec((n_groups, gd, gd), lambda b, s: (0, 0, 0)),
                  pl.BlockSpec((1, d), const),
                  pl.BlockSpec((1, d), const),
                  pl.BlockSpec((d, LANES), const),
                  pl.BlockSpec((ts, ts), const)],
        out_specs=[pl.BlockSpec((ts, d), row),
                   pl.BlockSpec((ts, d + LANES), row),
                   pl.BlockSpec((SUBLANES, ts), lambda b, s: (0, b * n_s + s)),
                   pl.BlockSpec((1, LANES), const)],
        out_shape=[jax.ShapeDtypeStruct((t, d), F32),
                   jax.ShapeDtypeStruct((t, d + LANES), F32),
                   jax.ShapeDtypeStruct((SUBLANES, t), jnp.int32),
                   jax.ShapeDtypeStruct((1, LANES), F32)],
        scratch_shapes=[pltpu.VMEM((POOL_HALO + ts, d), F32),
                        pltpu.VMEM((1, LANES), F32)],
        compiler_params=_params("arbitrary", "arbitrary"),
        name="pool_router",
    )(x, gain.reshape(1, d), pool_w, pool_scale.reshape(1, d), ffn_gain.reshape(1, d), wr, lower)


def _dispatch_kernel(slots_ref, bounds_ref, hg_ref, xs_ref, zero_ref, sem, zero_sem,
                     *, n_experts, tile):
    i = pl.program_id(0)
    td = hg_ref.shape[0]
    n_tokens = pl.num_programs(0) * td

    for r in range(td):
        for k in range(TOP_K):
            slot = slots_ref[k * n_tokens + i * td + r]
            pltpu.make_async_copy(hg_ref.at[pl.ds(r, 1), :],
                                  xs_ref.at[pl.ds(slot, 1), :], sem).start(priority=k % 2)

    def zero_copies():
        copies = []

        def block(first, size):
            return pltpu.make_async_copy(zero_ref.at[pl.ds(0, size), :],
                                         xs_ref.at[pl.ds(first, size), :], zero_sem)

        for e in range(n_experts):
            first_unused = bounds_ref[e]
            end = bounds_ref[n_experts + e]
            aligned = (first_unused + SUBLANES - 1) // SUBLANES * SUBLANES
            for r in range(SUBLANES - 1):
                copies.append((first_unused + r < aligned, block(first_unused + r, 1)))
            left = end - aligned
            pos = aligned
            size = tile // 2
            while size >= SUBLANES:
                needed = (left & size) != 0
                copies.append((needed, block(pl.multiple_of(pos, SUBLANES), size)))
                pos = pos + jnp.where(needed, size, 0)
                size //= 2
        total = bounds_ref[2 * n_experts]
        for b in range(n_experts):
            first = pl.multiple_of(total + b * tile, tile)
            copies.append((first < xs_ref.shape[0], block(first, tile)))
        return copies

    @pl.when(i == 0)
    def _():
        zero_ref[...] = jnp.zeros_like(zero_ref)
        for action in ("start", "wait"):
            for needed, copy in zero_copies():
                pl.when(needed)(getattr(copy, action))

    for _ in range(TOP_K):
        pltpu.make_async_copy(hg_ref, xs_ref.at[pl.ds(0, td), :], sem).wait()


def _dispatch(slots_flat, bounds, hg, *, td, n_experts, tile):
    t, width = hg.shape
    n_rows = t * TOP_K + n_experts * tile
    return pl.pallas_call(
        functools.partial(_dispatch_kernel, n_experts=n_experts, tile=tile),
        grid_spec=pltpu.PrefetchScalarGridSpec(
            num_scalar_prefetch=2,
            grid=(t // td,),
            in_specs=[pl.BlockSpec((td, width), lambda i, slots, bounds: (i, 0))],
            out_specs=pl.BlockSpec(memory_space=pl.ANY),
            scratch_shapes=[pltpu.VMEM((tile, width), F32),
                            pltpu.SemaphoreType.DMA,
                            pltpu.SemaphoreType.DMA]),
        out_shape=jax.ShapeDtypeStruct((n_rows, width), F32),
        compiler_params=_params("arbitrary"),
        name="moe_dispatch",
    )(slots_flat, bounds, hg)


def _experts_kernel(te_ref, nt_ref, xs_ref, wg_ref, wu_ref, wd_ref, ys_ref):
    i = pl.program_id(0)
    d = wg_ref.shape[0]

    @pl.when(i < nt_ref[0])
    def _():
        rows = xs_ref[:, 0:d].astype(BF16)
        gates = xs_ref[:, d:d + LANES]
        lane = lax.broadcasted_iota(jnp.int32, gates.shape, 1)
        ge = jnp.sum(jnp.where(lane == te_ref[i], gates, 0.0), axis=-1, keepdims=True)
        gate = _dot(rows, wg_ref[...])
        up = _dot(rows, wu_ref[...])
        act = (ge * (gate * _sigmoid(gate) * up)).astype(BF16)
        ys_ref[...] = _dot(act, wd_ref[...])

    @pl.when(i >= nt_ref[0])
    def _():
        ys_ref[...] = jnp.zeros_like(ys_ref)


def _experts(tile_expert, n_tiles, xs, wg, wu, wd, cast_weights, *, tile):
    n_rows, width = xs.shape
    n_experts, d, ffe = wg.shape
    row_map = lambda i, te, nt: (i, 0)
    w_map = lambda i, te, nt: (te[i], 0, 0)
    grid = (n_rows // tile,)
    casts = _Casts(cast_weights, grid)
    out, *cast = pl.pallas_call(
        casts.wrap(_experts_kernel, n_in=4, n_out=1, n_prefetch=2),
        grid_spec=pltpu.PrefetchScalarGridSpec(
            num_scalar_prefetch=2,
            grid=grid,
            in_specs=[pl.BlockSpec((tile, width), row_map),
                      pl.BlockSpec((None, d, ffe), w_map),
                      pl.BlockSpec((None, d, ffe), w_map),
                      pl.BlockSpec((None, ffe, d), w_map),
                      *casts.in_specs],
            out_specs=[pl.BlockSpec((tile, d), row_map), *casts.out_specs]),
        out_shape=[jax.ShapeDtypeStruct((n_rows, d), F32), *casts.out_shapes],
        compiler_params=_params("arbitrary"),
        name="moe_experts",
    )(tile_expert, n_tiles, xs, wg, wu, wd, *casts.views)
    return out, casts.restore(cast)


def _routing_tables(route, counts, *, n_experts, tile):
    t = route.shape[1]
    counts = counts[0, :n_experts].astype(jnp.int32)
    padded = (counts + tile - 1) // tile * tile
    ends = jnp.cumsum(padded)
    starts = ends - padded
    expert, order = route[:TOP_K], route[TOP_K:2 * TOP_K]
    slots = order
    for e in range(n_experts):
        slots = slots + jnp.where(expert == e, starts[e], 0)
    slots = slots.reshape(-1)
    bounds = jnp.concatenate([starts + counts, ends, ends[-1:]]).astype(jnp.int32)
    n_tiles_max = (t * TOP_K) // tile + n_experts
    first_row = jnp.arange(n_tiles_max, dtype=jnp.int32) * tile
    tile_expert = jnp.minimum(jnp.sum(first_row[:, None] >= ends[None, :], axis=1),
                              n_experts - 1).astype(jnp.int32)
    n_tiles = (ends[-1:] // tile).astype(jnp.int32)
    return slots.astype(jnp.int32), bounds, tile_expert, n_tiles


def _ple_combine_kernel(slots_ref, x_ref, g_ref, wg_ref, p_ref, wp_ref, fg_ref, ys_ref,
                        o_ref, ybuf, sem, *, final_norm):
    i = pl.program_id(0)
    n = pl.num_programs(0)
    tm, d = x_ref.shape
    n_tokens = n * tm
    chunk_cols = d // COMBINE_CHUNKS
    chunk_rows = tm // COMBINE_CHUNKS

    def start_rows(step, buf, first, last):
        for r in range(first, last):
            for k in range(TOP_K):
                slot = slots_ref[k * n_tokens + step * tm + r]
                pltpu.make_async_copy(ys_ref.at[pl.ds(slot, 1), :],
                                      ybuf.at[buf, k, pl.ds(r, 1), :],
                                      sem.at[buf]).start(priority=k % 2)

    def wait_tile(buf):
        for k in range(TOP_K):
            pltpu.make_async_copy(ys_ref.at[pl.ds(0, tm), :], ybuf.at[buf, k],
                                  sem.at[buf]).wait()

    cur = i % 2

    @pl.when(i == 0)
    def _():
        start_rows(i, cur, 0, tm)

    wait_tile(cur)
    x = x_ref[...]
    for k in range(TOP_K):
        x = x + ybuf[cur, k]
    h = _rms_norm(x, g_ref[...]).astype(BF16)
    pb = p_ref[...].astype(BF16)
    nxt = jnp.minimum(i + 1, n - 1)
    for c in range(COMBINE_CHUNKS):
        cols = slice(c * chunk_cols, (c + 1) * chunk_cols)
        gate = _sigmoid(_dot(h, wg_ref[:, cols]))
        o_ref[:, cols] = x[:, cols] + gate * _dot(pb, wp_ref[:, cols])
        start_rows(nxt, 1 - cur, c * chunk_rows, (c + 1) * chunk_rows)
    if final_norm:
        o_ref[...] = _rms_norm(o_ref[...], fg_ref[...])

    @pl.when(i == n - 1)
    def _():
        wait_tile(1 - cur)


def _ple_combine(slots_flat, x, gain, wg, p, layer, wp, final_gain, ys, *, tm, final_norm):
    t, d = x.shape
    dp = p.shape[-1]
    row = lambda i, slots: (i, 0)
    const = lambda i, slots: (0, 0)
    return pl.pallas_call(
        functools.partial(_ple_combine_kernel, final_norm=final_norm),
        grid_spec=pltpu.PrefetchScalarGridSpec(
            num_scalar_prefetch=1,
            grid=(t // tm,),
            in_specs=[pl.BlockSpec((tm, d), row),
                      pl.BlockSpec((1, d), const),
                      pl.BlockSpec((d, d), const),
                      pl.BlockSpec((None, tm, dp), lambda i, slots: (layer, i, 0)),
                      pl.BlockSpec((dp, d), const),
                      pl.BlockSpec((1, d), const),
                      pl.BlockSpec(memory_space=pl.ANY)],
            out_specs=pl.BlockSpec((tm, d), row),
            scratch_shapes=[pltpu.VMEM((2, TOP_K, tm, d), F32),
                            pltpu.SemaphoreType.DMA((2,))]),
        out_shape=jax.ShapeDtypeStruct((t, d), F32),
        compiler_params=_params("arbitrary"),
        name="ple_combine",
    )(slots_flat, x, gain.reshape(1, d), wg, p, wp, final_gain.reshape(1, d), ys)


def kernel(x, p, final_norm_gain, lb_table, mix_norm_even, w_in_even, hgrn_norm_gain, conv_w, w_out_even, ffn_norm_even, w_gate_dense, w_up_dense, w_down_dense, mix_norm_odd, pool_w, pool_scale, ffn_norm_odd, w_router, w_gate_exp, w_up_exp, w_down_exp, ple_norm, ple_gate_w, ple_proj):
    batch, s_len, d = x.shape
    t = batch * s_len
    depth = p.shape[0]
    assert depth == 2, "the weight-cast schedule below is written for one layer pair"
    n_experts = w_router.shape[-1]
    tile = EXPERT_ROW_TILE
    xs = x.reshape(t, d)

    pp = p.reshape(depth, t, -1)

    u, _ = _norm_matmul(xs, mix_norm_even[0], w_in_even[0].astype(BF16), [],
                        tm=IN_PROJ_TILE[0], tn=IN_PROJ_TILE[1])
    a, b, (w_out, w_gate, w_up, w_down) = _hgrn_conv(
        u, lb_table, hgrn_norm_gain[0], conv_w[0],
        [(w_out_even, 0), (w_gate_dense, 0), (w_up_dense, 0), (w_down_dense, 0)],
        batch=batch, rows=HGRN_ROWS, layer=0)
    xs, (ple_gate0, ple_proj0, pool_wb) = _out_proj(
        xs, a.reshape(t, -1), b.reshape(t, -1), w_out,
        [(ple_gate_w, 0), (ple_proj, 0), (pool_w, 0)], tm=OUT_PROJ_ROWS, tn=d)
    xs, (wg_exp, wu_exp, wd_exp) = _swiglu(
        xs, ffn_norm_even[0], w_gate, w_up, w_down,
        [(w_gate_exp, 0), (w_up_exp, 0), (w_down_exp, 0)],
        tm=SWIGLU_TILE[0], tf=SWIGLU_TILE[1])
    xs = _ple(xs, ple_norm[0], ple_gate0, pp, 0, ple_proj0, final_norm_gain,
              tm=PLE_ROWS, final_norm=False)

    xs, hg, route, counts = _pool_router(
        xs, mix_norm_odd[0], pool_wb, pool_scale[0], ffn_norm_odd[0], w_router[0],
        batch=batch, ts=POOL_ROWS)
    slots, bounds, tile_expert, n_tiles = _routing_tables(
        route, counts, n_experts=n_experts, tile=tile)
    rows = _dispatch(slots, bounds, hg, td=DISPATCH_ROWS, n_experts=n_experts, tile=tile)
    ys, (ple_gate1, ple_proj1) = _experts(
        tile_expert, n_tiles, rows, wg_exp, wu_exp, wd_exp, [(ple_gate_w, 1), (ple_proj, 1)],
        tile=tile)
    xs = _ple_combine(slots, xs, ple_norm[1], ple_gate1, pp, 1, ple_proj1,
                      final_norm_gain, ys, tm=COMBINE_ROWS, final_norm=True)
    return xs.reshape(batch, s_len, d)
```

```python
import functools
import math

import jax
import jax.numpy as jnp
from jax import lax
from jax.experimental import pallas as pl
from jax.experimental.pallas import tpu as pltpu

F32 = jnp.float32
BF16 = jnp.bfloat16

EPS = 1e-6
HGRN_HEADS = 8
HGRN_CHUNK = 64
CONV_WIDTH = 3
POOL_WINDOWS = (2, 4, 8, 16)
POOL_HALO = 16
TOP_K = 2
EXPERT_ROW_TILE = 512
COMBINE_CHUNKS = 8
LANES = 128
SUBLANES = 8
BF16_SUBLANES = 16
VMEM_LIMIT_BYTES = 56 * 1024 * 1024

IN_PROJ_TILE = (1024, 1792)
HGRN_ROWS = 1024
OUT_PROJ_ROWS = 512
SWIGLU_TILE = (1024, 256)
PLE_ROWS = 512
POOL_ROWS = 512
DISPATCH_ROWS = 1024
COMBINE_ROWS = 256


def _params(*semantics):
    return pltpu.CompilerParams(dimension_semantics=semantics,
                                vmem_limit_bytes=VMEM_LIMIT_BYTES)


def _rms_norm(x, gain):
    ms = jnp.mean(x * x, axis=-1, keepdims=True)
    return x * lax.rsqrt(ms + EPS) * gain


def _sigmoid(x):
    return 1.0 / (1.0 + jnp.exp(-x))


def _dot(a, b):
    return jnp.dot(a, b, preferred_element_type=F32)


def _dot_nt(a, b):
    return lax.dot_general(a, b, (((1,), (1,)), ((), ())), preferred_element_type=F32)


class _Casts:
    def __init__(self, weights, grid):
        n_steps = math.prod(grid)
        self.shapes = [stacked.shape[1:] for stacked, _ in weights]
        self.views, self.in_specs, self.out_specs, self.out_shapes = [], [], [], []
        for stacked, layer in weights:
            cols = stacked.shape[-1]
            rows = stacked[0].size // cols
            block_rows = next(br for br in range(BF16_SUBLANES, rows + 1, BF16_SUBLANES)
                              if rows % br == 0 and rows // br <= n_steps)
            n_blocks = rows // block_rows
            self.views.append(stacked.reshape(stacked.shape[0] * rows, cols))
            for specs, first in ((self.in_specs, layer * n_blocks), (self.out_specs, 0)):
                specs.append(pl.BlockSpec(
                    (block_rows, cols),
                    functools.partial(self._index_map, grid, n_blocks, first)))
            self.out_shapes.append(jax.ShapeDtypeStruct((rows, cols), BF16))

    @staticmethod
    def _index_map(grid, n_blocks, first, *args):
        step = 0
        for size, idx in zip(grid, args):
            step = step * size + idx
        return first + jnp.minimum(step, n_blocks - 1), 0

    def __len__(self):
        return len(self.views)

    def wrap(self, body, n_in, n_out, n_prefetch=0):
        n = len(self)

        def kernel(*refs):
            ins_end = n_prefetch + n_in
            outs_start = ins_end + n
            outs_end = outs_start + n_out
            body(*refs[:ins_end], *refs[outs_start:outs_end], *refs[outs_end + n:])
            for src, dst in zip(refs[ins_end:outs_start], refs[outs_end:outs_end + n]):
                dst[...] = src[...].astype(BF16)

        return kernel

    def restore(self, outs):
        return [o.reshape(shape) for o, shape in zip(outs, self.shapes)]


def _norm_matmul_kernel(x_ref, g_ref, w_ref, o_ref, h_ref):
    @pl.when(pl.program_id(1) == 0)
    def _():
        h_ref[...] = _rms_norm(x_ref[...], g_ref[...]).astype(BF16)

    o_ref[...] = _dot(h_ref[...], w_ref[...])


def _norm_matmul(x, gain, w, cast_weights, *, tm, tn):
    t, d = x.shape
    n = w.shape[1]
    grid = (t // tm, n // tn)
    casts = _Casts(cast_weights, grid)
    out, *cast = pl.pallas_call(
        casts.wrap(_norm_matmul_kernel, n_in=3, n_out=1),
        grid=grid,
        in_specs=[pl.BlockSpec((tm, d), lambda i, j: (i, 0)),
                  pl.BlockSpec((1, d), lambda i, j: (0, 0)),
                  pl.BlockSpec((d, tn), lambda i, j: (0, j)),
                  *casts.in_specs],
        out_specs=[pl.BlockSpec((tm, tn), lambda i, j: (i, j)), *casts.out_specs],
        out_shape=[jax.ShapeDtypeStruct((t, n), F32), *casts.out_shapes],
        scratch_shapes=[pltpu.VMEM((tm, d), BF16)],
        compiler_params=_params("parallel", "arbitrary"),
        name="norm_matmul",
    )(x, gain.reshape(1, d), w, *casts.views)
    return out, casts.restore(cast)


def _hgrn_conv_kernel(tri_ref, q_ref, f_ref, i_ref, g_ref, gb_ref, gc_ref, vc_ref,
                      lbt_ref, gain_ref, cw_ref, a_ref, b_ref, *, rows, layer):
    s_len, dh = q_ref.shape
    ch = HGRN_CHUNK
    ct = tri_ref.shape[0]

    lbt = lbt_ref[...]
    e = jnp.exp(lbt - jnp.max(lbt, axis=0, keepdims=True))
    lb = (jnp.sum(e[0:layer + 1, :], axis=0, keepdims=True)
          / jnp.sum(e, axis=0, keepdims=True))
    gain = gain_ref[...]

    tri = tri_ref[...]
    row = lax.broadcasted_iota(jnp.int32, (ct, ct), 0)
    col = lax.broadcasted_iota(jnp.int32, (ct, ct), 1)
    causal = (row >= col) & (row // ch == col // ch)
    n_chunks = rows // ch

    def body(t, st):
        r0 = pl.multiple_of(t * rows, rows)
        qr = q_ref[pl.ds(r0, rows), :]
        q = qr * _sigmoid(qr)
        f = lb + (1.0 - lb) * _sigmoid(f_ref[pl.ds(r0, rows), :])
        k = 1.0 - f
        v = i_ref[pl.ds(r0, rows), :].astype(BF16)
        lf = jnp.log(f)
        hi = lf.astype(BF16)
        r1 = lf - hi.astype(F32)
        mid = r1.astype(BF16)
        lo = (r1 - mid.astype(F32)).astype(BF16)
        parts = jnp.concatenate([hi, mid, lo], axis=1)
        cums = []
        for j in range(rows // ct):
            c3 = _dot(tri, parts[j * ct:(j + 1) * ct])
            cums.append(c3[:, :dh] + c3[:, dh:2 * dh] + c3[:, 2 * dh:])
        cum = jnp.concatenate(cums, axis=0)

        cum3 = cum.reshape(n_chunks, ch, dh)
        ref = cum3[:, ch // 2:ch // 2 + 1, :]
        last = cum3[:, ch - 1:ch, :]
        q3 = q.reshape(n_chunks, ch, dh)
        k3 = k.reshape(n_chunks, ch, dh)
        qe = (q3 * jnp.exp(cum3 - ref)).astype(BF16).reshape(rows, dh)
        ke = (k3 * jnp.exp(ref - cum3)).astype(BF16).reshape(rows, dh)
        qc = (q3 * jnp.exp(cum3)).astype(BF16).reshape(rows, dh)
        kd = (k3 * jnp.exp(last - cum3)).astype(BF16).reshape(rows, dh)
        decay = jnp.exp(last)

        chunks = [slice(c * ch, (c + 1) * ch) for c in range(n_chunks)]
        kv_t = [lax.dot_general(v[cs], kd[cs], (((0,), (0,)), ((), ())),
                                preferred_element_type=F32) for cs in chunks]
        states = []
        for c in range(n_chunks):
            states.append(st.astype(BF16))
            st = decay[c] * st + kv_t[c]
        o_inter = jnp.concatenate(
            [_dot_nt(qc[cs], s_c) for cs, s_c in zip(chunks, states)], axis=0)

        o_intra = []
        for j in range(rows // ct):
            sl = slice(j * ct, (j + 1) * ct)
            scores = jnp.where(causal, _dot_nt(qe[sl], ke[sl]), 0.0)
            o_intra.append(_dot(scores.astype(BF16), v[sl]))
        o = jnp.concatenate(o_intra, axis=0) + o_inter
        o = o * lax.rsqrt(jnp.mean(o * o, axis=-1, keepdims=True) + EPS)
        gr = g_ref[pl.ds(r0, rows), :]
        a_ref[pl.ds(r0, rows), :] = (o * gain * (gr * _sigmoid(gr))).astype(BF16)
        return st

    lax.fori_loop(0, s_len // rows, body, jnp.zeros((dh, dh), F32))

    tt = gc_ref[...] * vc_ref[...]
    ridx = lax.broadcasted_iota(jnp.int32, tt.shape, 0)
    cw = cw_ref[...]
    conv = tt * cw[CONV_WIDTH - 1:CONV_WIDTH, :]
    for back in range(1, CONV_WIDTH):
        shifted = jnp.where(ridx >= back, pltpu.roll(tt, back, 0), 0.0)
        conv = conv + shifted * cw[CONV_WIDTH - 1 - back:CONV_WIDTH - back, :]
    b_ref[...] = (gb_ref[...] * conv).astype(BF16)


def _hgrn_conv(u, lb_table, hgrn_gain, conv_w, cast_weights, *, batch, rows, layer):
    t, n_in = u.shape
    s_len = t // batch
    d_hgrn = hgrn_gain.shape[0]
    dh = d_hgrn // HGRN_HEADS
    d_conv = conv_w.shape[1]
    nh = HGRN_HEADS
    assert d_conv // dh == nh and n_in == 4 * d_hgrn + 3 * d_conv
    u3 = u.reshape(batch, s_len, n_in)

    idx = jnp.arange(2 * HGRN_CHUNK)
    tri = ((idx[:, None] >= idx[None, :])
           & (idx[:, None] // HGRN_CHUNK == idx[None, :] // HGRN_CHUNK)).astype(BF16)

    def col(off):
        return pl.BlockSpec((None, s_len, dh), lambda b, h, off=off: (b, 0, off + h))

    n_lb = lb_table.shape[0]
    out_spec = pl.BlockSpec((None, s_len, dh), lambda b, h: (b, 0, h))
    grid = (batch, nh)
    casts = _Casts(cast_weights, grid)
    a, b, *cast = pl.pallas_call(
        casts.wrap(functools.partial(_hgrn_conv_kernel, rows=rows, layer=layer),
                   n_in=11, n_out=2),
        grid=grid,
        in_specs=[pl.BlockSpec(tri.shape, lambda b, h: (0, 0)),
                  col(0), col(nh), col(2 * nh), col(3 * nh),
                  col(4 * nh), col(5 * nh), col(6 * nh),
                  pl.BlockSpec((n_lb, dh), lambda b, h: (0, h)),
                  pl.BlockSpec((1, dh), lambda b, h: (0, h)),
                  pl.BlockSpec((CONV_WIDTH, dh), lambda b, h: (0, h)),
                  *casts.in_specs],
        out_specs=[out_spec, out_spec, *casts.out_specs],
        out_shape=[jax.ShapeDtypeStruct((batch, s_len, d_hgrn), BF16),
                   jax.ShapeDtypeStruct((batch, s_len, d_conv), BF16),
                   *casts.out_shapes],
        compiler_params=_params("parallel", "parallel"),
        name="hgrn_conv",
    )(tri, u3, u3, u3, u3, u3, u3, u3, lb_table, hgrn_gain.reshape(1, d_hgrn), conv_w,
      *casts.views)
    return a, b, casts.restore(cast)


def _out_proj_kernel(x_ref, a_ref, b_ref, wa_ref, wb_ref, o_ref):
    o_ref[...] = x_ref[...] + _dot(a_ref[...], wa_ref[...]) + _dot(b_ref[...], wb_ref[...])


def _out_proj(x, a, b, w, cast_weights, *, tm, tn):
    t, d = x.shape
    ka, kb = a.shape[1], b.shape[1]
    assert ka == kb
    grid = (t // tm, d // tn)
    casts = _Casts(cast_weights, grid)
    out, *cast = pl.pallas_call(
        casts.wrap(_out_proj_kernel, n_in=5, n_out=1),
        grid=grid,
        in_specs=[pl.BlockSpec((tm, tn), lambda i, j: (i, j)),
                  pl.BlockSpec((tm, ka), lambda i, j: (i, 0)),
                  pl.BlockSpec((tm, kb), lambda i, j: (i, 0)),
                  pl.BlockSpec((ka, tn), lambda i, j: (0, j)),
                  pl.BlockSpec((kb, tn), lambda i, j: (1, j)),
                  *casts.in_specs],
        out_specs=[pl.BlockSpec((tm, tn), lambda i, j: (i, j)), *casts.out_specs],
        out_shape=[jax.ShapeDtypeStruct((t, d), F32), *casts.out_shapes],
        compiler_params=_params("parallel", "arbitrary"),
        name="out_proj",
    )(x, a, b, w, w, *casts.views)
    return out, casts.restore(cast)


def _swiglu_kernel(x_ref, g_ref, wg_ref, wu_ref, wd_ref, o_ref, h_ref):
    @pl.when(pl.program_id(1) == 0)
    def _():
        x = x_ref[...]
        h_ref[...] = _rms_norm(x, g_ref[...]).astype(BF16)
        o_ref[...] = x

    h = h_ref[...]
    gate = _dot(h, wg_ref[...])
    up = _dot(h, wu_ref[...])
    act = (gate * _sigmoid(gate) * up).astype(BF16)
    o_ref[...] += _dot(act, wd_ref[...])


def _swiglu(x, gain, wg, wu, wd, cast_weights, *, tm, tf):
    t, d = x.shape
    ff = wg.shape[1]
    grid = (t // tm, ff // tf)
    casts = _Casts(cast_weights, grid)
    out, *cast = pl.pallas_call(
        casts.wrap(_swiglu_kernel, n_in=5, n_out=1),
        grid=grid,
        in_specs=[pl.BlockSpec((tm, d), lambda i, f: (i, 0)),
                  pl.BlockSpec((1, d), lambda i, f: (0, 0)),
                  pl.BlockSpec((d, tf), lambda i, f: (0, f)),
                  pl.BlockSpec((d, tf), lambda i, f: (0, f)),
                  pl.BlockSpec((tf, d), lambda i, f: (f, 0)),
                  *casts.in_specs],
        out_specs=[pl.BlockSpec((tm, d), lambda i, f: (i, 0)), *casts.out_specs],
        out_shape=[jax.ShapeDtypeStruct((t, d), F32), *casts.out_shapes],
        scratch_shapes=[pltpu.VMEM((tm, d), BF16)],
        compiler_params=_params("parallel", "arbitrary"),
        name="swiglu",
    )(x, gain.reshape(1, d), wg, wu, wd, *casts.views)
    return out, casts.restore(cast)


def _ple_math(x, g_ref, wg_ref, p_ref, wp_ref, fg_ref, final_norm):
    h = _rms_norm(x, g_ref[...]).astype(BF16)
    gate = _sigmoid(_dot(h, wg_ref[...]))
    y = x + gate * _dot(p_ref[...].astype(BF16), wp_ref[...])
    if final_norm:
        y = _rms_norm(y, fg_ref[...])
    return y


def _ple_kernel(x_ref, g_ref, wg_ref, p_ref, wp_ref, fg_ref, o_ref, *, final_norm):
    o_ref[...] = _ple_math(x_ref[...], g_ref, wg_ref, p_ref, wp_ref, fg_ref, final_norm)


def _ple(x, gain, wg, p, layer, wp, final_gain, *, tm, final_norm):
    t, d = x.shape
    dp = p.shape[-1]
    return pl.pallas_call(
        functools.partial(_ple_kernel, final_norm=final_norm),
        grid=(t // tm,),
        in_specs=[pl.BlockSpec((tm, d), lambda i: (i, 0)),
                  pl.BlockSpec((1, d), lambda i: (0, 0)),
                  pl.BlockSpec((d, d), lambda i: (0, 0)),
                  pl.BlockSpec((None, tm, dp), lambda i: (layer, i, 0)),
                  pl.BlockSpec((dp, d), lambda i: (0, 0)),
                  pl.BlockSpec((1, d), lambda i: (0, 0))],
        out_specs=pl.BlockSpec((tm, d), lambda i: (i, 0)),
        out_shape=jax.ShapeDtypeStruct((t, d), F32),
        compiler_params=_params("parallel"),
        name="ple",
    )(x, gain.reshape(1, d), wg, p, wp, final_gain.reshape(1, d))


def _pool_router_kernel(x_ref, g_ref, pw_ref, ps_ref, fg_ref, wr_ref, lt_ref,
                        x1_ref, hg_ref, route_ref, counts_ref, ext_ref, cnt_ref,
                        *, n_experts):
    s = pl.program_id(1)
    ts, d = x_ref.shape
    n_groups = len(POOL_WINDOWS)
    gd = d // n_groups

    @pl.when(s == 0)
    def _():
        ext_ref[0:POOL_HALO, :] = jnp.zeros((POOL_HALO, d), F32)

    @pl.when(s > 0)
    def _():
        ext_ref[0:POOL_HALO, :] = ext_ref[ts:ts + POOL_HALO, :]

    x = x_ref[...]
    h = _rms_norm(x, g_ref[...])
    ext_ref[POOL_HALO:, :] = h

    def window_sum(cols, w):
        acc = ext_ref[:, cols]
        span = 1
        while span < w:
            acc = acc + pltpu.roll(acc, span, 0)
            span *= 2
        return acc[POOL_HALO:, :]

    pos = (s * ts + 1 + lax.broadcasted_iota(jnp.int32, (ts, 1), 0)).astype(F32)
    ys = []
    for gi, w in enumerate(POOL_WINDOWS):
        cols = slice(gi * gd, (gi + 1) * gd)
        mean = window_sum(cols, w) / jnp.minimum(pos, float(w))
        diff = (mean - h[:, cols]).astype(BF16)
        ys.append(_dot(diff, pw_ref[gi]))
    y = jnp.concatenate(ys, axis=1)
    x1 = x + y * ps_ref[...]
    x1_ref[...] = x1

    h2 = _rms_norm(x1, fg_ref[...])
    hg_ref[:, 0:d] = h2

    hi = h2.astype(BF16)
    mid = (h2 - hi.astype(F32)).astype(BF16)
    p_hi = _dot(hi, wr_ref[...])
    p_mid = _dot(mid, wr_ref[...])
    logits = (p_hi + pltpu.roll(p_hi, LANES - n_experts, 1)
              + pltpu.roll(p_hi, LANES - 2 * n_experts, 1)
              + p_mid + pltpu.roll(p_mid, LANES - n_experts, 1))
    lane = lax.broadcasted_iota(jnp.int32, logits.shape, 1)
    valid = lane < n_experts
    logits = jnp.where(valid, logits, -jnp.inf)
    ex = jnp.exp(logits - jnp.max(logits, axis=-1, keepdims=True))
    probs = ex / jnp.sum(ex, axis=-1, keepdims=True)
    gates = jnp.zeros_like(probs)
    rest = jnp.where(valid, probs, -1.0)
    tops = []
    for _ in range(TOP_K):
        m = jnp.max(rest, axis=-1, keepdims=True)
        first = jnp.min(jnp.where(rest == m, lane, LANES), axis=-1, keepdims=True)
        sel = lane == first
        tops.append((m, sel))
        rest = jnp.where(sel, -1.0, rest)
    total = tops[0][0]
    for m, _ in tops[1:]:
        total = total + m
    for m, sel in tops:
        gates = jnp.where(sel, m / total, gates)
    hg_ref[:, d:d + LANES] = gates

    first_step = (pl.program_id(0) == 0) & (s == 0)

    @pl.when(first_step)
    def _():
        cnt_ref[...] = jnp.zeros_like(cnt_ref)

    chosen = tops[0][1]
    for _, sel in tops[1:]:
        chosen = chosen | sel
    chosen = jnp.where(chosen, 1.0, 0.0)
    before = cnt_ref[...] + _dot(lt_ref[...], chosen.astype(BF16))
    cnt_ref[...] = cnt_ref[...] + jnp.sum(chosen, axis=0, keepdims=True)
    counts_ref[...] = cnt_ref[...]
    lane_f = lane.astype(F32)
    route = jnp.zeros((ts, LANES), F32)
    for k, (_, sel) in enumerate(tops):
        expert_k = jnp.sum(jnp.where(sel, lane_f, 0.0), axis=-1, keepdims=True)
        order_k = jnp.sum(jnp.where(sel, before, 0.0), axis=-1, keepdims=True)
        route = jnp.where(lane == k, expert_k, route)
        route = jnp.where(lane == TOP_K + k, order_k, route)
    route_ref[...] = route.T[0:SUBLANES, :].astype(jnp.int32)


def _pool_router(x, gain, pool_w, pool_scale, ffn_gain, w_router, *, batch, ts):
    t, d = x.shape
    s_len = t // batch
    n_s = s_len // ts
    n_groups, gd, _ = pool_w.shape
    n_experts = w_router.shape[1]
    assert 3 * n_experts <= LANES
    w_hi = w_router.astype(BF16)
    r1 = w_router - w_hi.astype(F32)
    w_mid = r1.astype(BF16)
    w_lo = (r1 - w_mid.astype(F32)).astype(BF16)
    wr = jnp.zeros((d, LANES), BF16).at[:, :3 * n_experts].set(
        jnp.concatenate([w_hi, w_mid, w_lo], axis=1))
    idx = jnp.arange(ts)
    lower = (idx[None, :] < idx[:, None]).astype(BF16)
    row = lambda b, s: (b * n_s + s, 0)
    const = lambda b, s: (0, 0)
    return pl.pallas_call(
        functools.partial(_pool_router_kernel, n_experts=n_experts),
        grid=(batch, n_s),
        in_specs=[pl.BlockSpec((ts, d), row),
                  pl.BlockSpec((1, d), const),
                  pl.BlockSpec((n_groups, gd, gd), lambda b, s: (0, 0, 0)),
                  pl.BlockSpec((1, d), const),
                  pl.BlockSpec((1, d), const),
                  pl.BlockSpec((d, LANES), const),
                  pl.BlockSpec((ts, ts), const)],
        out_specs=[pl.BlockSpec((ts, d), row),
                   pl.BlockSpec((ts, d + LANES), row),
                   pl.BlockSpec((SUBLANES, ts), lambda b, s: (0, b * n_s + s)),
                   pl.BlockSpec((1, LANES), const)],
        out_shape=[jax.ShapeDtypeStruct((t, d), F32),
                   jax.ShapeDtypeStruct((t, d + LANES), F32),
                   jax.ShapeDtypeStruct((SUBLANES, t), jnp.int32),
                   jax.ShapeDtypeStruct((1, LANES), F32)],
        scratch_shapes=[pltpu.VMEM((POOL_HALO + ts, d), F32),
                        pltpu.VMEM((1, LANES), F32)],
        compiler_params=_params("arbitrary", "arbitrary"),
        name="pool_router",
    )(x, gain.reshape(1, d), pool_w, pool_scale.reshape(1, d), ffn_gain.reshape(1, d), wr, lower)


def _dispatch_kernel(slots_ref, bounds_ref, hg_ref, xs_ref, zero_ref, sem, zero_sem,
                     *, n_experts, tile):
    i = pl.program_id(0)
    td = hg_ref.shape[0]
    n_tokens = pl.num_programs(0) * td

    for r in range(td):
        for k in range(TOP_K):
            slot = slots_ref[k * n_tokens + i * td + r]
            pltpu.make_async_copy(hg_ref.at[pl.ds(r, 1), :],
                                  xs_ref.at[pl.ds(slot, 1), :], sem).start(priority=k % 2)

    def zero_copies():
        copies = []

        def block(first, size):
            return pltpu.make_async_copy(zero_ref.at[pl.ds(0, size), :],
                                         xs_ref.at[pl.ds(first, size), :], zero_sem)

        for e in range(n_experts):
            first_unused = bounds_ref[e]
            end = bounds_ref[n_experts + e]
            aligned = (first_unused + SUBLANES - 1) // SUBLANES * SUBLANES
            for r in range(SUBLANES - 1):
                copies.append((first_unused + r < aligned, block(first_unused + r, 1)))
            left = end - aligned
            pos = aligned
            size = tile // 2
            while size >= SUBLANES:
                needed = (left & size) != 0
                copies.append((needed, block(pl.multiple_of(pos, SUBLANES), size)))
                pos = pos + jnp.where(needed, size, 0)
                size //= 2
        total = bounds_ref[2 * n_experts]
        for b in range(n_experts):
            first = pl.multiple_of(total + b * tile, tile)
            copies.append((first < xs_ref.shape[0], block(first, tile)))
        return copies

    @pl.when(i == 0)
    def _():
        zero_ref[...] = jnp.zeros_like(zero_ref)
        for action in ("start", "wait"):
            for needed, copy in zero_copies():
                pl.when(needed)(getattr(copy, action))

    for _ in range(TOP_K):
        pltpu.make_async_copy(hg_ref, xs_ref.at[pl.ds(0, td), :], sem).wait()


def _dispatch(slots_flat, bounds, hg, *, td, n_experts, tile):
    t, width = hg.shape
    n_rows = t * TOP_K + n_experts * tile
    return pl.pallas_call(
        functools.partial(_dispatch_kernel, n_experts=n_experts, tile=tile),
        grid_spec=pltpu.PrefetchScalarGridSpec(
            num_scalar_prefetch=2,
            grid=(t // td,),
            in_specs=[pl.BlockSpec((td, width), lambda i, slots, bounds: (i, 0))],
            out_specs=pl.BlockSpec(memory_space=pl.ANY),
            scratch_shapes=[pltpu.VMEM((tile, width), F32),
                            pltpu.SemaphoreType.DMA,
                            pltpu.SemaphoreType.DMA]),
        out_shape=jax.ShapeDtypeStruct((n_rows, width), F32),
        compiler_params=_params("arbitrary"),
        name="moe_dispatch",
    )(slots_flat, bounds, hg)


def _experts_kernel(te_ref, nt_ref, xs_ref, wg_ref, wu_ref, wd_ref, ys_ref):
    i = pl.program_id(0)
    d = wg_ref.shape[0]

    @pl.when(i < nt_ref[0])
    def _():
        rows = xs_ref[:, 0:d].astype(BF16)
        gates = xs_ref[:, d:d + LANES]
        lane = lax.broadcasted_iota(jnp.int32, gates.shape, 1)
        ge = jnp.sum(jnp.where(lane == te_ref[i], gates, 0.0), axis=-1, keepdims=True)
        gate = _dot(rows, wg_ref[...])
        up = _dot(rows, wu_ref[...])
        act = (ge * (gate * _sigmoid(gate) * up)).astype(BF16)
        ys_ref[...] = _dot(act, wd_ref[...])

    @pl.when(i >= nt_ref[0])
    def _():
        ys_ref[...] = jnp.zeros_like(ys_ref)


def _experts(tile_expert, n_tiles, xs, wg, wu, wd, cast_weights, *, tile):
    n_rows, width = xs.shape
    n_experts, d, ffe = wg.shape
    row_map = lambda i, te, nt: (i, 0)
    w_map = lambda i, te, nt: (te[i], 0, 0)
    grid = (n_rows // tile,)
    casts = _Casts(cast_weights, grid)
    out, *cast = pl.pallas_call(
        casts.wrap(_experts_kernel, n_in=4, n_out=1, n_prefetch=2),
        grid_spec=pltpu.PrefetchScalarGridSpec(
            num_scalar_prefetch=2,
            grid=grid,
            in_specs=[pl.BlockSpec((tile, width), row_map),
                      pl.BlockSpec((None, d, ffe), w_map),
                      pl.BlockSpec((None, d, ffe), w_map),
                      pl.BlockSpec((None, ffe, d), w_map),
                      *casts.in_specs],
            out_specs=[pl.BlockSpec((tile, d), row_map), *casts.out_specs]),
        out_shape=[jax.ShapeDtypeStruct((n_rows, d), F32), *casts.out_shapes],
        compiler_params=_params("arbitrary"),
        name="moe_experts",
    )(tile_expert, n_tiles, xs, wg, wu, wd, *casts.views)
    return out, casts.restore(cast)


def _routing_tables(route, counts, *, n_experts, tile):
    t = route.shape[1]
    counts = counts[0, :n_experts].astype(jnp.int32)
    padded = (counts + tile - 1) // tile * tile
    ends = jnp.cumsum(padded)
    starts = ends - padded
    expert, order = route[:TOP_K], route[TOP_K:2 * TOP_K]
    slots = order
    for e in range(n_experts):
        slots = slots + jnp.where(expert == e, starts[e], 0)
    slots = slots.reshape(-1)
    bounds = jnp.concatenate([starts + counts, ends, ends[-1:]]).astype(jnp.int32)
    n_tiles_max = (t * TOP_K) // tile + n_experts
    first_row = jnp.arange(n_tiles_max, dtype=jnp.int32) * tile
    tile_expert = jnp.minimum(jnp.sum(first_row[:, None] >= ends[None, :], axis=1),
                              n_experts - 1).astype(jnp.int32)
    n_tiles = (ends[-1:] // tile).astype(jnp.int32)
    return slots.astype(jnp.int32), bounds, tile_expert, n_tiles


def _ple_combine_kernel(slots_ref, x_ref, g_ref, wg_ref, p_ref, wp_ref, fg_ref, ys_ref,
                        o_ref, ybuf, sem, *, final_norm):
    i = pl.program_id(0)
    n = pl.num_programs(0)
    tm, d = x_ref.shape
    n_tokens = n * tm
    chunk_cols = d // COMBINE_CHUNKS
    chunk_rows = tm // COMBINE_CHUNKS

    def start_rows(step, buf, first, last):
        for r in range(first, last):
            for k in range(TOP_K):
                slot = slots_ref[k * n_tokens + step * tm + r]
                pltpu.make_async_copy(ys_ref.at[pl.ds(slot, 1), :],
                                      ybuf.at[buf, k, pl.ds(r, 1), :],
                                      sem.at[buf]).start(priority=k % 2)

    def wait_tile(buf):
        for k in range(TOP_K):
            pltpu.make_async_copy(ys_ref.at[pl.ds(0, tm), :], ybuf.at[buf, k],
                                  sem.at[buf]).wait()

    cur = i % 2

    @pl.when(i == 0)
    def _():
        start_rows(i, cur, 0, tm)

    wait_tile(cur)
    x = x_ref[...]
    for k in range(TOP_K):
        x = x + ybuf[cur, k]
    h = _rms_norm(x, g_ref[...]).astype(BF16)
    pb = p_ref[...].astype(BF16)
    nxt = jnp.minimum(i + 1, n - 1)
    for c in range(COMBINE_CHUNKS):
        cols = slice(c * chunk_cols, (c + 1) * chunk_cols)
        gate = _sigmoid(_dot(h, wg_ref[:, cols]))
        o_ref[:, cols] = x[:, cols] + gate * _dot(pb, wp_ref[:, cols])
        start_rows(nxt, 1 - cur, c * chunk_rows, (c + 1) * chunk_rows)
    if final_norm:
        o_ref[...] = _rms_norm(o_ref[...], fg_ref[...])

    @pl.when(i == n - 1)
    def _():
        wait_tile(1 - cur)


def _ple_combine(slots_flat, x, gain, wg, p, layer, wp, final_gain, ys, *, tm, final_norm):
    t, d = x.shape
    dp = p.shape[-1]
    row = lambda i, slots: (i, 0)
    const = lambda i, slots: (0, 0)
    return pl.pallas_call(
        functools.partial(_ple_combine_kernel, final_norm=final_norm),
        grid_spec=pltpu.PrefetchScalarGridSpec(
            num_scalar_prefetch=1,
            grid=(t // tm,),
            in_specs=[pl.BlockSpec((tm, d), row),
                      pl.BlockSpec((1, d), const),
                      pl.BlockSpec((d, d), const),
                      pl.BlockSpec((None, tm, dp), lambda i, slots: (layer, i, 0)),
                      pl.BlockSpec((dp, d), const),
                      pl.BlockSpec((1, d), const),
                      pl.BlockSpec(memory_space=pl.ANY)],
            out_specs=pl.BlockSpec((tm, d), row),
            scratch_shapes=[pltpu.VMEM((2, TOP_K, tm, d), F32),
                            pltpu.SemaphoreType.DMA((2,))]),
        out_shape=jax.ShapeDtypeStruct((t, d), F32),
        compiler_params=_params("arbitrary"),
        name="ple_combine",
    )(slots_flat, x, gain.reshape(1, d), wg, p, wp, final_gain.reshape(1, d), ys)


def kernel(x, p, final_norm_gain, lb_table, mix_norm_even, w_in_even, hgrn_norm_gain, conv_w, w_out_even, ffn_norm_even, w_gate_dense, w_up_dense, w_down_dense, mix_norm_odd, pool_w, pool_scale, ffn_norm_odd, w_router, w_gate_exp, w_up_exp, w_down_exp, ple_norm, ple_gate_w, ple_proj):
    batch, s_len, d = x.shape
    t = batch * s_len
    depth = p.shape[0]
    assert depth == 2, "the weight-cast schedule below is written for one layer pair"
    n_experts = w_router.shape[-1]
    tile = EXPERT_ROW_TILE
    xs = x.reshape(t, d)

    pp = p.reshape(depth, t, -1)

    u, _ = _norm_matmul(xs, mix_norm_even[0], w_in_even[0].astype(BF16), [],
                        tm=IN_PROJ_TILE[0], tn=IN_PROJ_TILE[1])
    a, b, (w_out, w_gate, w_up, w_down) = _hgrn_conv(
        u, lb_table, hgrn_norm_gain[0], conv_w[0],
        [(w_out_even, 0), (w_gate_dense, 0), (w_up_dense, 0), (w_down_dense, 0)],
        batch=batch, rows=HGRN_ROWS, layer=0)
    xs, (ple_gate0, ple_proj0, pool_wb) = _out_proj(
        xs, a.reshape(t, -1), b.reshape(t, -1), w_out,
        [(ple_gate_w, 0), (ple_proj, 0), (pool_w, 0)], tm=OUT_PROJ_ROWS, tn=d)
    xs, (wg_exp, wu_exp, wd_exp) = _swiglu(
        xs, ffn_norm_even[0], w_gate, w_up, w_down,
        [(w_gate_exp, 0), (w_up_exp, 0), (w_down_exp, 0)],
        tm=SWIGLU_TILE[0], tf=SWIGLU_TILE[1])
    xs = _ple(xs, ple_norm[0], ple_gate0, pp, 0, ple_proj0, final_norm_gain,
              tm=PLE_ROWS, final_norm=False)

    xs, hg, route, counts = _pool_router(
        xs, mix_norm_odd[0], pool_wb, pool_scale[0], ffn_norm_odd[0], w_router[0],
        batch=batch, ts=POOL_ROWS)
    slots, bounds, tile_expert, n_tiles = _routing_tables(
        route, counts, n_experts=n_experts, tile=tile)
    rows = _dispatch(slots, bounds, hg, td=DISPATCH_ROWS, n_experts=n_experts, tile=tile)
    ys, (ple_gate1, ple_proj1) = _experts(
        tile_expert, n_tiles, rows, wg_exp, wu_exp, wd_exp, [(ple_gate_w, 1), (ple_proj, 1)],
        tile=tile)
    xs = _ple_combine(slots, xs, ple_norm[1], ple_gate1, pp, 1, ple_proj1,
                      final_norm_gain, ys, tm=COMBINE_ROWS, final_norm=True)
    return xs.reshape(batch, s_len, d)
```

```python
import functools
import math

import jax
import jax.numpy as jnp
from jax import lax
from jax.experimental import pallas as pl
from jax.experimental.pallas import tpu as pltpu

F32 = jnp.float32
BF16 = jnp.bfloat16

EPS = 1e-6
HGRN_HEADS = 8
HGRN_CHUNK = 64
CONV_WIDTH = 3
POOL_WINDOWS = (2, 4, 8, 16)
POOL_HALO = 16
TOP_K = 2
EXPERT_ROW_TILE = 512
COMBINE_CHUNKS = 8
LANES = 128
SUBLANES = 8
BF16_SUBLANES = 16
VMEM_LIMIT_BYTES = 56 * 1024 * 1024

IN_PROJ_TILE = (1024, 1792)
HGRN_ROWS = 1024
OUT_PROJ_ROWS = 512
SWIGLU_TILE = (1024, 256)
PLE_ROWS = 1024
POOL_ROWS = 512
DISPATCH_ROWS = 1024
COMBINE_ROWS = 256


def _params(*semantics):
    return pltpu.CompilerParams(dimension_semantics=semantics,
                                vmem_limit_bytes=VMEM_LIMIT_BYTES)


def _rms_norm(x, gain):
    ms = jnp.mean(x * x, axis=-1, keepdims=True)
    return x * lax.rsqrt(ms + EPS) * gain


def _sigmoid(x):
    return 1.0 / (1.0 + jnp.exp(-x))


def _dot(a, b):
    return jnp.dot(a, b, preferred_element_type=F32)


def _dot_nt(a, b):
    return lax.dot_general(a, b, (((1,), (1,)), ((), ())), preferred_element_type=F32)


class _Casts:
    def __init__(self, weights, grid):
        n_steps = math.prod(grid)
        self.shapes = [stacked.shape[1:] for stacked, _ in weights]
        self.views, self.in_specs, self.out_specs, self.out_shapes = [], [], [], []
        for stacked, layer in weights:
            cols = stacked.shape[-1]
            rows = stacked[0].size // cols
            block_rows = next(br for br in range(BF16_SUBLANES, rows + 1, BF16_SUBLANES)
                              if rows % br == 0 and rows // br <= n_steps)
            n_blocks = rows // block_rows
            self.views.append(stacked.reshape(stacked.shape[0] * rows, cols))
            for specs, first in ((self.in_specs, layer * n_blocks), (self.out_specs, 0)):
                specs.append(pl.BlockSpec(
                    (block_rows, cols),
                    functools.partial(self._index_map, grid, n_blocks, first)))
            self.out_shapes.append(jax.ShapeDtypeStruct((rows, cols), BF16))

    @staticmethod
    def _index_map(grid, n_blocks, first, *args):
        step = 0
        for size, idx in zip(grid, args):
            step = step * size + idx
        return first + jnp.minimum(step, n_blocks - 1), 0

    def __len__(self):
        return len(self.views)

    def wrap(self, body, n_in, n_out, n_prefetch=0):
        n = len(self)

        def kernel(*refs):
            ins_end = n_prefetch + n_in
            outs_start = ins_end + n
            outs_end = outs_start + n_out
            body(*refs[:ins_end], *refs[outs_start:outs_end], *refs[outs_end + n:])
            for src, dst in zip(refs[ins_end:outs_start], refs[outs_end:outs_end + n]):
                dst[...] = src[...].astype(BF16)

        return kernel

    def restore(self, outs):
        return [o.reshape(shape) for o, shape in zip(outs, self.shapes)]


def _norm_matmul_kernel(x_ref, g_ref, w_ref, o_ref, h_ref):
    @pl.when(pl.program_id(1) == 0)
    def _():
        h_ref[...] = _rms_norm(x_ref[...], g_ref[...]).astype(BF16)

    o_ref[...] = _dot(h_ref[...], w_ref[...])


def _norm_matmul(x, gain, w, cast_weights, *, tm, tn):
    t, d = x.shape
    n = w.shape[1]
    grid = (t // tm, n // tn)
    casts = _Casts(cast_weights, grid)
    out, *cast = pl.pallas_call(
        casts.wrap(_norm_matmul_kernel, n_in=3, n_out=1),
        grid=grid,
        in_specs=[pl.BlockSpec((tm, d), lambda i, j: (i, 0)),
                  pl.BlockSpec((1, d), lambda i, j: (0, 0)),
                  pl.BlockSpec((d, tn), lambda i, j: (0, j)),
                  *casts.in_specs],
        out_specs=[pl.BlockSpec((tm, tn), lambda i, j: (i, j)), *casts.out_specs],
        out_shape=[jax.ShapeDtypeStruct((t, n), F32), *casts.out_shapes],
        scratch_shapes=[pltpu.VMEM((tm, d), BF16)],
        compiler_params=_params("parallel", "arbitrary"),
        name="norm_matmul",
    )(x, gain.reshape(1, d), w, *casts.views)
    return out, casts.restore(cast)


def _hgrn_conv_kernel(tri_ref, q_ref, f_ref, i_ref, g_ref, gb_ref, gc_ref, vc_ref,
                      lbt_ref, gain_ref, cw_ref, a_ref, b_ref, *, rows, layer):
    s_len, dh = q_ref.shape
    ch = HGRN_CHUNK
    ct = tri_ref.shape[0]

    lbt = lbt_ref[...]
    e = jnp.exp(lbt - jnp.max(lbt, axis=0, keepdims=True))
    lb = (jnp.sum(e[0:layer + 1, :], axis=0, keepdims=True)
          / jnp.sum(e, axis=0, keepdims=True))
    gain = gain_ref[...]

    tri = tri_ref[...]
    row = lax.broadcasted_iota(jnp.int32, (ct, ct), 0)
    col = lax.broadcasted_iota(jnp.int32, (ct, ct), 1)
    causal = (row >= col) & (row // ch == col // ch)
    n_chunks = rows // ch

    def body(t, st):
        r0 = pl.multiple_of(t * rows, rows)
        qr = q_ref[pl.ds(r0, rows), :]
        q = qr * _sigmoid(qr)
        f = lb + (1.0 - lb) * _sigmoid(f_ref[pl.ds(r0, rows), :])
        k = 1.0 - f
        v = i_ref[pl.ds(r0, rows), :].astype(BF16)
        lf = jnp.log(f)
        hi = lf.astype(BF16)
        r1 = lf - hi.astype(F32)
        mid = r1.astype(BF16)
        lo = (r1 - mid.astype(F32)).astype(BF16)
        parts = jnp.concatenate([hi, mid, lo], axis=1)
        cums = []
        for j in range(rows // ct):
            c3 = _dot(tri, parts[j * ct:(j + 1) * ct])
            cums.append(c3[:, :dh] + c3[:, dh:2 * dh] + c3[:, 2 * dh:])
        cum = jnp.concatenate(cums, axis=0)

        cum3 = cum.reshape(n_chunks, ch, dh)
        ref = cum3[:, ch // 2:ch // 2 + 1, :]
        last = cum3[:, ch - 1:ch, :]
        q3 = q.reshape(n_chunks, ch, dh)
        k3 = k.reshape(n_chunks, ch, dh)
        qe = (q3 * jnp.exp(cum3 - ref)).astype(BF16).reshape(rows, dh)
        ke = (k3 * jnp.exp(ref - cum3)).astype(BF16).reshape(rows, dh)
        qc = (q3 * jnp.exp(cum3)).astype(BF16).reshape(rows, dh)
        kd = (k3 * jnp.exp(last - cum3)).astype(BF16).reshape(rows, dh)
        decay = jnp.exp(last)

        chunks = [slice(c * ch, (c + 1) * ch) for c in range(n_chunks)]
        kv_t = [lax.dot_general(v[cs], kd[cs], (((0,), (0,)), ((), ())),
                                preferred_element_type=F32) for cs in chunks]
        states = []
        for c in range(n_chunks):
            states.append(st.astype(BF16))
            st = decay[c] * st + kv_t[c]
        o_inter = jnp.concatenate(
            [_dot_nt(qc[cs], s_c) for cs, s_c in zip(chunks, states)], axis=0)

        o_intra = []
        for j in range(rows // ct):
            sl = slice(j * ct, (j + 1) * ct)
            scores = jnp.where(causal, _dot_nt(qe[sl], ke[sl]), 0.0)
            o_intra.append(_dot(scores.astype(BF16), v[sl]))
        o = jnp.concatenate(o_intra, axis=0) + o_inter
        o = o * lax.rsqrt(jnp.mean(o * o, axis=-1, keepdims=True) + EPS)
        gr = g_ref[pl.ds(r0, rows), :]
        a_ref[pl.ds(r0, rows), :] = (o * gain * (gr * _sigmoid(gr))).astype(BF16)
        return st

    lax.fori_loop(0, s_len // rows, body, jnp.zeros((dh, dh), F32))

    tt = gc_ref[...] * vc_ref[...]
    ridx = lax.broadcasted_iota(jnp.int32, tt.shape, 0)
    cw = cw_ref[...]
    conv = tt * cw[CONV_WIDTH - 1:CONV_WIDTH, :]
    for back in range(1, CONV_WIDTH):
        shifted = jnp.where(ridx >= back, pltpu.roll(tt, back, 0), 0.0)
        conv = conv + shifted * cw[CONV_WIDTH - 1 - back:CONV_WIDTH - back, :]
    b_ref[...] = (gb_ref[...] * conv).astype(BF16)


def _hgrn_conv(u, lb_table, hgrn_gain, conv_w, cast_weights, *, batch, rows, layer):
    t, n_in = u.shape
    s_len = t // batch
    d_hgrn = hgrn_gain.shape[0]
    dh = d_hgrn // HGRN_HEADS
    d_conv = conv_w.shape[1]
    nh = HGRN_HEADS
    assert d_conv // dh == nh and n_in == 4 * d_hgrn + 3 * d_conv
    u3 = u.reshape(batch, s_len, n_in)

    idx = jnp.arange(2 * HGRN_CHUNK)
    tri = ((idx[:, None] >= idx[None, :])
           & (idx[:, None] // HGRN_CHUNK == idx[None, :] // HGRN_CHUNK)).astype(BF16)

    def col(off):
        return pl.BlockSpec((None, s_len, dh), lambda b, h, off=off: (b, 0, off + h))

    n_lb = lb_table.shape[0]
    out_spec = pl.BlockSpec((None, s_len, dh), lambda b, h: (b, 0, h))
    grid = (batch, nh)
    casts = _Casts(cast_weights, grid)
    a, b, *cast = pl.pallas_call(
        casts.wrap(functools.partial(_hgrn_conv_kernel, rows=rows, layer=layer),
                   n_in=11, n_out=2),
        grid=grid,
        in_specs=[pl.BlockSpec(tri.shape, lambda b, h: (0, 0)),
                  col(0), col(nh), col(2 * nh), col(3 * nh),
                  col(4 * nh), col(5 * nh), col(6 * nh),
                  pl.BlockSpec((n_lb, dh), lambda b, h: (0, h)),
                  pl.BlockSpec((1, dh), lambda b, h: (0, h)),
                  pl.BlockSpec((CONV_WIDTH, dh), lambda b, h: (0, h)),
                  *casts.in_specs],
        out_specs=[out_spec, out_spec, *casts.out_specs],
        out_shape=[jax.ShapeDtypeStruct((batch, s_len, d_hgrn), BF16),
                   jax.ShapeDtypeStruct((batch, s_len, d_conv), BF16),
                   *casts.out_shapes],
        compiler_params=_params("parallel", "parallel"),
        name="hgrn_conv",
    )(tri, u3, u3, u3, u3, u3, u3, u3, lb_table, hgrn_gain.reshape(1, d_hgrn), conv_w,
      *casts.views)
    return a, b, casts.restore(cast)


def _out_proj_kernel(x_ref, a_ref, b_ref, wa_ref, wb_ref, o_ref):
    o_ref[...] = x_ref[...] + _dot(a_ref[...], wa_ref[...]) + _dot(b_ref[...], wb_ref[...])


def _out_proj(x, a, b, w, cast_weights, *, tm, tn):
    t, d = x.shape
    ka, kb = a.shape[1], b.shape[1]
    assert ka == kb
    grid = (t // tm, d // tn)
    casts = _Casts(cast_weights, grid)
    out, *cast = pl.pallas_call(
        casts.wrap(_out_proj_kernel, n_in=5, n_out=1),
        grid=grid,
        in_specs=[pl.BlockSpec((tm, tn), lambda i, j: (i, j)),
                  pl.BlockSpec((tm, ka), lambda i, j: (i, 0)),
                  pl.BlockSpec((tm, kb), lambda i, j: (i, 0)),
                  pl.BlockSpec((ka, tn), lambda i, j: (0, j)),
                  pl.BlockSpec((kb, tn), lambda i, j: (1, j)),
                  *casts.in_specs],
        out_specs=[pl.BlockSpec((tm, tn), lambda i, j: (i, j)), *casts.out_specs],
        out_shape=[jax.ShapeDtypeStruct((t, d), F32), *casts.out_shapes],
        compiler_params=_params("parallel", "arbitrary"),
        name="out_proj",
    )(x, a, b, w, w, *casts.views)
    return out, casts.restore(cast)


def _swiglu_kernel(x_ref, g_ref, wg_ref, wu_ref, wd_ref, o_ref, h_ref):
    @pl.when(pl.program_id(1) == 0)
    def _():
        x = x_ref[...]
        h_ref[...] = _rms_norm(x, g_ref[...]).astype(BF16)
        o_ref[...] = x

    h = h_ref[...]
    gate = _dot(h, wg_ref[...])
    up = _dot(h, wu_ref[...])
    act = (gate * _sigmoid(gate) * up).astype(BF16)
    o_ref[...] += _dot(act, wd_ref[...])


def _swiglu(x, gain, wg, wu, wd, cast_weights, *, tm, tf):
    t, d = x.shape
    ff = wg.shape[1]
    grid = (t // tm, ff // tf)
    casts = _Casts(cast_weights, grid)
    out, *cast = pl.pallas_call(
        casts.wrap(_swiglu_kernel, n_in=5, n_out=1),
        grid=grid,
        in_specs=[pl.BlockSpec((tm, d), lambda i, f: (i, 0)),
                  pl.BlockSpec((1, d), lambda i, f: (0, 0)),
                  pl.BlockSpec((d, tf), lambda i, f: (0, f)),
                  pl.BlockSpec((d, tf), lambda i, f: (0, f)),
                  pl.BlockSpec((tf, d), lambda i, f: (f, 0)),
                  *casts.in_specs],
        out_specs=[pl.BlockSpec((tm, d), lambda i, f: (i, 0)), *casts.out_specs],
        out_shape=[jax.ShapeDtypeStruct((t, d), F32), *casts.out_shapes],
        scratch_shapes=[pltpu.VMEM((tm, d), BF16)],
        compiler_params=_params("parallel", "arbitrary"),
        name="swiglu",
    )(x, gain.reshape(1, d), wg, wu, wd, *casts.views)
    return out, casts.restore(cast)


def _ple_math(x, g_ref, wg_ref, p_ref, wp_ref, fg_ref, final_norm):
    h = _rms_norm(x, g_ref[...]).astype(BF16)
    gate = _sigmoid(_dot(h, wg_ref[...]))
    y = x + gate * _dot(p_ref[...].astype(BF16), wp_ref[...])
    if final_norm:
        y = _rms_norm(y, fg_ref[...])
    return y


def _ple_kernel(x_ref, g_ref, wg_ref, p_ref, wp_ref, fg_ref, o_ref, *, final_norm):
    o_ref[...] = _ple_math(x_ref[...], g_ref, wg_ref, p_ref, wp_ref, fg_ref, final_norm)


def _ple(x, gain, wg, p, layer, wp, final_gain, *, tm, final_norm):
    t, d = x.shape
    dp = p.shape[-1]
    return pl.pallas_call(
        functools.partial(_ple_kernel, final_norm=final_norm),
        grid=(t // tm,),
        in_specs=[pl.BlockSpec((tm, d), lambda i: (i, 0)),
                  pl.BlockSpec((1, d), lambda i: (0, 0)),
                  pl.BlockSpec((d, d), lambda i: (0, 0), pipeline_mode=pl.Buffered(1)),
                  pl.BlockSpec((None, tm, dp), lambda i: (layer, i, 0)),
                  pl.BlockSpec((dp, d), lambda i: (0, 0), pipeline_mode=pl.Buffered(1)),
                  pl.BlockSpec((1, d), lambda i: (0, 0))],
        out_specs=pl.BlockSpec((tm, d), lambda i: (i, 0)),
        out_shape=jax.ShapeDtypeStruct((t, d), F32),
        compiler_params=_params("parallel"),
        name="ple",
    )(x, gain.reshape(1, d), wg, p, wp, final_gain.reshape(1, d))


def _pool_router_kernel(x_ref, g_ref, pw_ref, ps_ref, fg_ref, wr_ref, lt_ref,
                        x1_ref, hg_ref, route_ref, counts_ref, ext_ref, cnt_ref,
                        *, n_experts):
    s = pl.program_id(1)
    ts, d = x_ref.shape
    n_groups = len(POOL_WINDOWS)
    gd = d // n_groups

    @pl.when(s == 0)
    def _():
        ext_ref[0:POOL_HALO, :] = jnp.zeros((POOL_HALO, d), F32)

    @pl.when(s > 0)
    def _():
        ext_ref[0:POOL_HALO, :] = ext_ref[ts:ts + POOL_HALO, :]

    x = x_ref[...]
    h = _rms_norm(x, g_ref[...])
    ext_ref[POOL_HALO:, :] = h

    def window_sum(cols, w):
        acc = ext_ref[:, cols]
        span = 1
        while span < w:
            acc = acc + pltpu.roll(acc, span, 0)
            span *= 2
        return acc[POOL_HALO:, :]

    pos = (s * ts + 1 + lax.broadcasted_iota(jnp.int32, (ts, 1), 0)).astype(F32)
    ys = []
    for gi, w in enumerate(POOL_WINDOWS):
        cols = slice(gi * gd, (gi + 1) * gd)
        mean = window_sum(cols, w) / jnp.minimum(pos, float(w))
        diff = (mean - h[:, cols]).astype(BF16)
        ys.append(_dot(diff, pw_ref[gi]))
    y = jnp.concatenate(ys, axis=1)
    x1 = x + y * ps_ref[...]
    x1_ref[...] = x1

    h2 = _rms_norm(x1, fg_ref[...])
    hg_ref[:, 0:d] = h2

    hi = h2.astype(BF16)
    mid = (h2 - hi.astype(F32)).astype(BF16)
    p_hi = _dot(hi, wr_ref[...])
    p_mid = _dot(mid, wr_ref[...])
    logits = (p_hi + pltpu.roll(p_hi, LANES - n_experts, 1)
              + pltpu.roll(p_hi, LANES - 2 * n_experts, 1)
              + p_mid + pltpu.roll(p_mid, LANES - n_experts, 1))
    lane = lax.broadcasted_iota(jnp.int32, logits.shape, 1)
    valid = lane < n_experts
    logits = jnp.where(valid, logits, -jnp.inf)
    ex = jnp.exp(logits - jnp.max(logits, axis=-1, keepdims=True))
    probs = ex / jnp.sum(ex, axis=-1, keepdims=True)
    gates = jnp.zeros_like(probs)
    rest = jnp.where(valid, probs, -1.0)
    tops = []
    for _ in range(TOP_K):
        m = jnp.max(rest, axis=-1, keepdims=True)
        first = jnp.min(jnp.where(rest == m, lane, LANES), axis=-1, keepdims=True)
        sel = lane == first
        tops.append((m, sel))
        rest = jnp.where(sel, -1.0, rest)
    total = tops[0][0]
    for m, _ in tops[1:]:
        total = total + m
    for m, sel in tops:
        gates = jnp.where(sel, m / total, gates)
    hg_ref[:, d:d + LANES] = gates

    first_step = (pl.program_id(0) == 0) & (s == 0)

    @pl.when(first_step)
    def _():
        cnt_ref[...] = jnp.zeros_like(cnt_ref)

    chosen = tops[0][1]
    for _, sel in tops[1:]:
        chosen = chosen | sel
    chosen = jnp.where(chosen, 1.0, 0.0)
    before = cnt_ref[...] + _dot(lt_ref[...], chosen.astype(BF16))
    cnt_ref[...] = cnt_ref[...] + jnp.sum(chosen, axis=0, keepdims=True)
    counts_ref[...] = cnt_ref[...]
    lane_f = lane.astype(F32)
    route = jnp.zeros((ts, LANES), F32)
    for k, (_, sel) in enumerate(tops):
        expert_k = jnp.sum(jnp.where(sel, lane_f, 0.0), axis=-1, keepdims=True)
        order_k = jnp.sum(jnp.where(sel, before, 0.0), axis=-1, keepdims=True)
        route = jnp.where(lane == k, expert_k, route)
        route = jnp.where(lane == TOP_K + k, order_k, route)
    route_ref[...] = route.T[0:SUBLANES, :].astype(jnp.int32)


def _pool_router(x, gain, pool_w, pool_scale, ffn_gain, w_router, *, batch, ts):
    t, d = x.shape
    s_len = t // batch
    n_s = s_len // ts
    n_groups, gd, _ = pool_w.shape
    n_experts = w_router.shape[1]
    assert 3 * n_experts <= LANES
    w_hi = w_router.astype(BF16)
    r1 = w_router - w_hi.astype(F32)
    w_mid = r1.astype(BF16)
    w_lo = (r1 - w_mid.astype(F32)).astype(BF16)
    wr = jnp.zeros((d, LANES), BF16).at[:, :3 * n_experts].set(
        jnp.concatenate([w_hi, w_mid, w_lo], axis=1))
    idx = jnp.arange(ts)
    lower = (idx[None, :] < idx[:, None]).astype(BF16)
    row = lambda b, s: (b * n_s + s, 0)
    const = lambda b, s: (0, 0)
    return pl.pallas_call(
        functools.partial(_pool_router_kernel, n_experts=n_experts),
        grid=(batch, n_s),
        in_specs=[pl.BlockSpec((ts, d), row),
                  pl.BlockSpec((1, d), const),
                  pl.BlockSpec((n_groups, gd, gd), lambda b, s: (0, 0, 0)),
                  pl.BlockSpec((1, d), const),
                  pl.BlockSpec((1, d), const),
                  pl.BlockSpec((d, LANES), const),
                  pl.BlockSpec((ts, ts), const)],
        out_specs=[pl.BlockSpec((ts, d), row),
                   pl.BlockSpec((ts, d + LANES), row),
                   pl.BlockSpec((SUBLANES, ts), lambda b, s: (0, b * n_s + s)),
                   pl.BlockSpec((1, LANES), const)],
        out_shape=[jax.ShapeDtypeStruct((t, d), F32),
                   jax.ShapeDtypeStruct((t, d + LANES), F32),
                   jax.ShapeDtypeStruct((SUBLANES, t), jnp.int32),
                   jax.ShapeDtypeStruct((1, LANES), F32)],
        scratch_shapes=[pltpu.VMEM((POOL_HALO + ts, d), F32),
                        pltpu.VMEM((1, LANES), F32)],
        compiler_params=_params("arbitrary", "arbitrary"),
        name="pool_router",
    )(x, gain.reshape(1, d), pool_w, pool_scale.reshape(1, d), ffn_gain.reshape(1, d), wr, lower)


def _dispatch_kernel(slots_ref, bounds_ref, hg_ref, xs_ref, zero_ref, sem, zero_sem,
                     *, n_experts, tile):
    i = pl.program_id(0)
    td = hg_ref.shape[0]
    n_tokens = pl.num_programs(0) * td

    for r in range(td):
        for k in range(TOP_K):
            slot = slots_ref[k * n_tokens + i * td + r]
            pltpu.make_async_copy(hg_ref.at[pl.ds(r, 1), :],
                                  xs_ref.at[pl.ds(slot, 1), :], sem).start(priority=k % 2)

    def zero_copies():
        copies = []

        def block(first, size):
            return pltpu.make_async_copy(zero_ref.at[pl.ds(0, size), :],
                                         xs_ref.at[pl.ds(first, size), :], zero_sem)

        for e in range(n_experts):
            first_unused = bounds_ref[e]
            end = bounds_ref[n_experts + e]
            aligned = (first_unused + SUBLANES - 1) // SUBLANES * SUBLANES
            for r in range(SUBLANES - 1):
                copies.append((first_unused + r < aligned, block(first_unused + r, 1)))
            left = end - aligned
            pos = aligned
            size = tile // 2
            while size >= SUBLANES:
                needed = (left & size) != 0
                copies.append((needed, block(pl.multiple_of(pos, SUBLANES), size)))
                pos = pos + jnp.where(needed, size, 0)
                size //= 2
        total = bounds_ref[2 * n_experts]
        for b in range(n_experts):
            first = pl.multiple_of(total + b * tile, tile)
            copies.append((first < xs_ref.shape[0], block(first, tile)))
        return copies

    @pl.when(i == 0)
    def _():
        zero_ref[...] = jnp.zeros_like(zero_ref)
        for action in ("start", "wait"):
            for needed, copy in zero_copies():
                pl.when(needed)(getattr(copy, action))

    for _ in range(TOP_K):
        pltpu.make_async_copy(hg_ref, xs_ref.at[pl.ds(0, td), :], sem).wait()


def _dispatch(slots_flat, bounds, hg, *, td, n_experts, tile):
    t, width = hg.shape
    n_rows = t * TOP_K + n_experts * tile
    return pl.pallas_call(
        functools.partial(_dispatch_kernel, n_experts=n_experts, tile=tile),
        grid_spec=pltpu.PrefetchScalarGridSpec(
            num_scalar_prefetch=2,
            grid=(t // td,),
            in_specs=[pl.BlockSpec((td, width), lambda i, slots, bounds: (i, 0))],
            out_specs=pl.BlockSpec(memory_space=pl.ANY),
            scratch_shapes=[pltpu.VMEM((tile, width), F32),
                            pltpu.SemaphoreType.DMA,
                            pltpu.SemaphoreType.DMA]),
        out_shape=jax.ShapeDtypeStruct((n_rows, width), F32),
        compiler_params=_params("arbitrary"),
        name="moe_dispatch",
    )(slots_flat, bounds, hg)


def _experts_kernel(te_ref, nt_ref, xs_ref, wg_ref, wu_ref, wd_ref, ys_ref):
    i = pl.program_id(0)
    d = wg_ref.shape[0]

    @pl.when(i < nt_ref[0])
    def _():
        rows = xs_ref[:, 0:d].astype(BF16)
        gates = xs_ref[:, d:d + LANES]
        lane = lax.broadcasted_iota(jnp.int32, gates.shape, 1)
        ge = jnp.sum(jnp.where(lane == te_ref[i], gates, 0.0), axis=-1, keepdims=True)
        gate = _dot(rows, wg_ref[...])
        up = _dot(rows, wu_ref[...])
        act = (ge * (gate * _sigmoid(gate) * up)).astype(BF16)
        ys_ref[...] = _dot(act, wd_ref[...])

    @pl.when(i >= nt_ref[0])
    def _():
        ys_ref[...] = jnp.zeros_like(ys_ref)


def _experts(tile_expert, n_tiles, xs, wg, wu, wd, cast_weights, *, tile):
    n_rows, width = xs.shape
    n_experts, d, ffe = wg.shape
    row_map = lambda i, te, nt: (i, 0)
    w_map = lambda i, te, nt: (te[i], 0, 0)
    grid = (n_rows // tile,)
    casts = _Casts(cast_weights, grid)
    out, *cast = pl.pallas_call(
        casts.wrap(_experts_kernel, n_in=4, n_out=1, n_prefetch=2),
        grid_spec=pltpu.PrefetchScalarGridSpec(
            num_scalar_prefetch=2,
            grid=grid,
            in_specs=[pl.BlockSpec((tile, width), row_map),
                      pl.BlockSpec((None, d, ffe), w_map),
                      pl.BlockSpec((None, d, ffe), w_map),
                      pl.BlockSpec((None, ffe, d), w_map),
                      *casts.in_specs],
            out_specs=[pl.BlockSpec((tile, d), row_map), *casts.out_specs]),
        out_shape=[jax.ShapeDtypeStruct((n_rows, d), F32), *casts.out_shapes],
        compiler_params=_params("arbitrary"),
        name="moe_experts",
    )(tile_expert, n_tiles, xs, wg, wu, wd, *casts.views)
    return out, casts.restore(cast)


def _routing_tables(route, counts, *, n_experts, tile):
    t = route.shape[1]
    counts = counts[0, :n_experts].astype(jnp.int32)
    padded = (counts + tile - 1) // tile * tile
    ends = jnp.cumsum(padded)
    starts = ends - padded
    expert, order = route[:TOP_K], route[TOP_K:2 * TOP_K]
    slots = order
    for e in range(n_experts):
        slots = slots + jnp.where(expert == e, starts[e], 0)
    slots = slots.reshape(-1)
    bounds = jnp.concatenate([starts + counts, ends, ends[-1:]]).astype(jnp.int32)
    n_tiles_max = (t * TOP_K) // tile + n_experts
    first_row = jnp.arange(n_tiles_max, dtype=jnp.int32) * tile
    tile_expert = jnp.minimum(jnp.sum(first_row[:, None] >= ends[None, :], axis=1),
                              n_experts - 1).astype(jnp.int32)
    n_tiles = (ends[-1:] // tile).astype(jnp.int32)
    return slots.astype(jnp.int32), bounds, tile_expert, n_tiles


def _ple_combine_kernel(slots_ref, x_ref, g_ref, wg_ref, p_ref, wp_ref, fg_ref, ys_ref,
                        o_ref, ybuf, sem, *, final_norm):
    i = pl.program_id(0)
    n = pl.num_programs(0)
    tm, d = x_ref.shape
    n_tokens = n * tm
    chunk_cols = d // COMBINE_CHUNKS
    chunk_rows = tm // COMBINE_CHUNKS

    def start_rows(step, buf, first, last):
        for r in range(first, last):
            for k in range(TOP_K):
                slot = slots_ref[k * n_tokens + step * tm + r]
                pltpu.make_async_copy(ys_ref.at[pl.ds(slot, 1), :],
                                      ybuf.at[buf, k, pl.ds(r, 1), :],
                                      sem.at[buf]).start(priority=k % 2)

    def wait_tile(buf):
        for k in range(TOP_K):
            pltpu.make_async_copy(ys_ref.at[pl.ds(0, tm), :], ybuf.at[buf, k],
                                  sem.at[buf]).wait()

    cur = i % 2

    @pl.when(i == 0)
    def _():
        start_rows(i, cur, 0, tm)

    wait_tile(cur)
    x = x_ref[...]
    for k in range(TOP_K):
        x = x + ybuf[cur, k]
    h = _rms_norm(x, g_ref[...]).astype(BF16)
    pb = p_ref[...].astype(BF16)
    nxt = jnp.minimum(i + 1, n - 1)
    for c in range(COMBINE_CHUNKS):
        cols = slice(c * chunk_cols, (c + 1) * chunk_cols)
        gate = _sigmoid(_dot(h, wg_ref[:, cols]))
        o_ref[:, cols] = x[:, cols] + gate * _dot(pb, wp_ref[:, cols])
        start_rows(nxt, 1 - cur, c * chunk_rows, (c + 1) * chunk_rows)
    if final_norm:
        o_ref[...] = _rms_norm(o_ref[...], fg_ref[...])

    @pl.when(i == n - 1)
    def _():
        wait_tile(1 - cur)


def _ple_combine(slots_flat, x, gain, wg, p, layer, wp, final_gain, ys, *, tm, final_norm):
    t, d = x.shape
    dp = p.shape[-1]
    row = lambda i, slots: (i, 0)
    const = lambda i, slots: (0, 0)
    return pl.pallas_call(
        functools.partial(_ple_combine_kernel, final_norm=final_norm),
        grid_spec=pltpu.PrefetchScalarGridSpec(
            num_scalar_prefetch=1,
            grid=(t // tm,),
            in_specs=[pl.BlockSpec((tm, d), row),
                      pl.BlockSpec((1, d), const),
                      pl.BlockSpec((d, d), const),
                      pl.BlockSpec((None, tm, dp), lambda i, slots: (layer, i, 0)),
                      pl.BlockSpec((dp, d), const),
                      pl.BlockSpec((1, d), const),
                      pl.BlockSpec(memory_space=pl.ANY)],
            out_specs=pl.BlockSpec((tm, d), row),
            scratch_shapes=[pltpu.VMEM((2, TOP_K, tm, d), F32),
                            pltpu.SemaphoreType.DMA((2,))]),
        out_shape=jax.ShapeDtypeStruct((t, d), F32),
        compiler_params=_params("arbitrary"),
        name="ple_combine",
    )(slots_flat, x, gain.reshape(1, d), wg, p, wp, final_gain.reshape(1, d), ys)


def kernel(x, p, final_norm_gain, lb_table, mix_norm_even, w_in_even, hgrn_norm_gain, conv_w, w_out_even, ffn_norm_even, w_gate_dense, w_up_dense, w_down_dense, mix_norm_odd, pool_w, pool_scale, ffn_norm_odd, w_router, w_gate_exp, w_up_exp, w_down_exp, ple_norm, ple_gate_w, ple_proj):
    batch, s_len, d = x.shape
    t = batch * s_len
    depth = p.shape[0]
    assert depth == 2, "the weight-cast schedule below is written for one layer pair"
    n_experts = w_router.shape[-1]
    tile = EXPERT_ROW_TILE
    xs = x.reshape(t, d)

    pp = p.reshape(depth, t, -1)

    u, _ = _norm_matmul(xs, mix_norm_even[0], w_in_even[0].astype(BF16), [],
                        tm=IN_PROJ_TILE[0], tn=IN_PROJ_TILE[1])
    a, b, (w_out, w_gate, w_up, w_down) = _hgrn_conv(
        u, lb_table, hgrn_norm_gain[0], conv_w[0],
        [(w_out_even, 0), (w_gate_dense, 0), (w_up_dense, 0), (w_down_dense, 0)],
        batch=batch, rows=HGRN_ROWS, layer=0)
    xs, (ple_gate0, ple_proj0, pool_wb) = _out_proj(
        xs, a.reshape(t, -1), b.reshape(t, -1), w_out,
        [(ple_gate_w, 0), (ple_proj, 0), (pool_w, 0)], tm=OUT_PROJ_ROWS, tn=d)
    xs, (wg_exp, wu_exp, wd_exp) = _swiglu(
        xs, ffn_norm_even[0], w_gate, w_up, w_down,
        [(w_gate_exp, 0), (w_up_exp, 0), (w_down_exp, 0)],
        tm=SWIGLU_TILE[0], tf=SWIGLU_TILE[1])
    xs = _ple(xs, ple_norm[0], ple_gate0, pp, 0, ple_proj0, final_norm_gain,
              tm=PLE_ROWS, final_norm=False)

    xs, hg, route, counts = _pool_router(
        xs, mix_norm_odd[0], pool_wb, pool_scale[0], ffn_norm_odd[0], w_router[0],
        batch=batch, ts=POOL_ROWS)
    slots, bounds, tile_expert, n_tiles = _routing_tables(
        route, counts, n_experts=n_experts, tile=tile)
    rows = _dispatch(slots, bounds, hg, td=DISPATCH_ROWS, n_experts=n_experts, tile=tile)
    ys, (ple_gate1, ple_proj1) = _experts(
        tile_expert, n_tiles, rows, wg_exp, wu_exp, wd_exp, [(ple_gate_w, 1), (ple_proj, 1)],
        tile=tile)
    xs = _ple_combine(slots, xs, ple_norm[1], ple_gate1, pp, 1, ple_proj1,
                      final_norm_gain, ys, tm=COMBINE_ROWS, final_norm=True)
    return xs.reshape(batch, s_len, d)
```
